```python
import math
import jax, jax.numpy as jnp
from jax import lax
import numpy as np

D_MODEL = 2048
BATCH = 8
SEQ = 2048
DEPTH = 2

CHUNK = 64
Q_BLOCK = 128
N_MEM = 256
EPS = 1e-6
NEG = -1e30

A_WIDTH = D_MODEL // 2
CONV_WIDTH = 31
MLA_HEADS = 8
MLA_NOPE = 128
MLA_ROPE = 64
MLA_V = 128
MLA_Q_RANK = 512
MLA_KV_RANK = 512
ROPE_THETA = 10000.0
EV_IN = 2 * A_WIDTH + MLA_Q_RANK + MLA_KV_RANK + MLA_ROPE
EV_OUT = A_WIDTH + MLA_HEADS * MLA_V
DIFF_HEADS = 8
DIFF_HD = D_MODEL // (2 * DIFF_HEADS)
OD_IN = 3 * D_MODEL
REL_BUCKETS = 32
REL_MAX_DIST = 128
CROSS_HEADS = 4
CROSS_HD = 128
D_FF = 5632
N_EXPERTS = 8
TOP_K = 2
D_FF_EXPERT = 7168
MOE_BLOCK = 128

kernel_name = "hybrid_conv_mla_diffattn_moe_encoder"


def rmsnorm(x, g):
    xf = x.astype(jnp.float32)
    y = xf * lax.rsqrt(jnp.mean(xf * xf, axis=-1, keepdims=True) + EPS)
    return (y * g.astype(jnp.float32)).astype(x.dtype)


def layernorm(x, g, b):
    xf = x.astype(jnp.float32)
    mu = jnp.mean(xf, axis=-1, keepdims=True)
    xc = xf - mu
    y = xc * lax.rsqrt(jnp.mean(xc * xc, axis=-1, keepdims=True) + EPS)
    return (y * g.astype(jnp.float32) + b.astype(jnp.float32)).astype(x.dtype)


def rope_tables(seq):
    pos = jnp.arange(seq, dtype=jnp.float32)
    inv = jnp.power(ROPE_THETA, -jnp.arange(0, MLA_ROPE, 2, dtype=jnp.float32) / MLA_ROPE)
    ang = pos[:, None] * inv[None, :]
    return jnp.cos(ang)[:, None, :], jnp.sin(ang)[:, None, :]


def rope(x, cos, sin):
    half = x.shape[-1] // 2
    c, s = cos.astype(x.dtype), sin.astype(x.dtype)
    x1, x2 = x[..., :half], x[..., half:]
    return jnp.concatenate([x1 * c - x2 * s, x1 * s + x2 * c], axis=-1)


def t5_bucket(rel):
    half = REL_BUCKETS // 2
    max_exact = half // 2
    ret = (rel > 0).astype(jnp.int32) * half
    n = jnp.abs(rel)
    nf = jnp.maximum(n, 1).astype(jnp.float32)
    large = max_exact + (jnp.log(nf / max_exact) / math.log(REL_MAX_DIST / max_exact)
                         * (half - max_exact)).astype(jnp.int32)
    large = jnp.minimum(large, half - 1)
    return ret + jnp.where(n < max_exact, n, large)


def chunk_causal_sweep(block_fn, seq):
    outs = []
    for qb in range(seq // Q_BLOCK):
        q0 = qb * Q_BLOCK
        kv_end = q0 + Q_BLOCK
        q_pos = jnp.arange(q0, kv_end, dtype=jnp.int32)
        k_pos = jnp.arange(kv_end, dtype=jnp.int32)
        mask = (k_pos[None, :] // CHUNK) <= (q_pos[:, None] // CHUNK)
        rel = k_pos[None, :] - q_pos[:, None]
        outs.append(block_fn(q0, kv_end, rel, mask))
    return jnp.concatenate(outs, axis=1)


def conformer_conv(val, gate, conv_w, conv_b, ln_g, ln_b):
    u = val * jax.nn.sigmoid(gate)
    u = lax.conv_general_dilated(
        u, conv_w[:, None, :], window_strides=(1,), padding=[(CONV_WIDTH - 1, 0)],
        dimension_numbers=("NWC", "WIO", "NWC"), feature_group_count=A_WIDTH) + conv_b
    return jax.nn.silu(layernorm(u, ln_g, ln_b))


def mla_attention(cq_in, ckv_in, kr_in, q_norm_g, w_uq, kv_norm_g, w_ukv, cos, sin):
    B, S, _ = cq_in.shape
    q = (rmsnorm(cq_in, q_norm_g) @ w_uq).reshape(B, S, MLA_HEADS, MLA_NOPE + MLA_ROPE)
    q_nope, q_rope = q[..., :MLA_NOPE], rope(q[..., MLA_NOPE:], cos, sin)
    kv = (rmsnorm(ckv_in, kv_norm_g) @ w_ukv).reshape(B, S, MLA_HEADS, MLA_NOPE + MLA_V)
    k_nope, v = kv[..., :MLA_NOPE], kv[..., MLA_NOPE:]
    k_rope = rope(kr_in[:, :, None, :], cos, sin)[:, :, 0, :]
    scale = (MLA_NOPE + MLA_ROPE) ** -0.5

    def block(q0, kv_end, rel, mask):
        s = (jnp.einsum("bqhd,bkhd->bhqk", q_nope[:, q0:kv_end], k_nope[:, :kv_end])
             + jnp.einsum("bqhr,bkr->bhqk", q_rope[:, q0:kv_end], k_rope[:, :kv_end]))
        p = jax.nn.softmax(jnp.where(mask, s.astype(jnp.float32) * scale, NEG), axis=-1)
        return jnp.einsum("bhqk,bkhd->bqhd", p.astype(v.dtype), v[:, :kv_end])

    o = chunk_causal_sweep(block, S)
    return o.reshape(B, S, MLA_HEADS * MLA_V)


def diff_attention(qkv, lq1, lk1, lq2, lk2, subln_g, rel_bias, lambda_init):
    B, S, _ = qkv.shape
    q = qkv[..., :D_MODEL].reshape(B, S, DIFF_HEADS, 2, DIFF_HD)
    k = qkv[..., D_MODEL:2 * D_MODEL].reshape(B, S, DIFF_HEADS, 2, DIFF_HD)
    v = qkv[..., 2 * D_MODEL:].reshape(B, S, DIFF_HEADS, 2 * DIFF_HD)
    f32 = jnp.float32
    lam = (jnp.exp(jnp.sum(lq1.astype(f32) * lk1.astype(f32)))
           - jnp.exp(jnp.sum(lq2.astype(f32) * lk2.astype(f32))) + lambda_init)
    scale = DIFF_HD ** -0.5

    def block(q0, kv_end, rel, mask):
        bias = jnp.moveaxis(rel_bias[t5_bucket(rel)], -1, 0).astype(f32)
        s = jnp.einsum("bqhcd,bkhcd->bchqk", q[:, q0:kv_end], k[:, :kv_end]).astype(f32) * scale + bias
        p = jax.nn.softmax(jnp.where(mask, s, NEG), axis=-1)
        w = p[:, 0] - lam * p[:, 1]
        return jnp.einsum("bhqk,bkhd->bqhd", w.astype(v.dtype), v[:, :kv_end])

    o = chunk_causal_sweep(block, S)
    o = rmsnorm(o, subln_g) * (1.0 - lambda_init)
    return o.reshape(B, S, DIFF_HEADS * 2 * DIFF_HD)


def memory_cross_attention(h, memn, wq, wkv, wo):
    B, S, _ = h.shape
    M = memn.shape[1]
    q = (h @ wq).reshape(B, S, CROSS_HEADS, CROSS_HD)
    kv = (memn @ wkv).reshape(B, M, 2, CROSS_HEADS, CROSS_HD)
    k, v = kv[:, :, 0], kv[:, :, 1]
    s = jnp.einsum("bqhd,bkhd->bhqk", q, k).astype(jnp.float32) * (CROSS_HD ** -0.5)
    p = jax.nn.softmax(s, axis=-1)
    o = jnp.einsum("bhqk,bkhd->bqhd", p.astype(v.dtype), v).reshape(B, S, CROSS_HEADS * CROSS_HD)
    return o @ wo


def swiglu(x, wg, wu, wd):
    return (jax.nn.silu(x @ wg) * (x @ wu)) @ wd


def moe_swiglu(x, w_router, wg, wu, wd):
    B, S, D = x.shape
    T = B * S
    A = T * TOP_K
    xf = x.reshape(T, D)
    logits = (xf @ w_router).astype(jnp.float32)
    top_v, top_e = lax.top_k(logits, TOP_K)
    gates = jax.nn.softmax(top_v, axis=-1)
    flat_e = top_e.reshape(A)
    flat_tok = jnp.repeat(jnp.arange(T, dtype=jnp.int32), TOP_K)
    flat_g = gates.reshape(A)
    order = jnp.argsort(flat_e)
    sorted_e = flat_e[order]
    counts = jnp.bincount(flat_e, length=N_EXPERTS)
    starts = jnp.cumsum(counts) - counts
    padded = ((counts + MOE_BLOCK - 1) // MOE_BLOCK) * MOE_BLOCK
    padded_ends = jnp.cumsum(padded)
    padded_starts = padded_ends - padded
    dest = padded_starts[sorted_e] + (jnp.arange(A, dtype=jnp.int32) - starts[sorted_e])
    P = A + N_EXPERTS * MOE_BLOCK
    n_blocks = P // MOE_BLOCK
    slot_tok = jnp.full((P,), T, jnp.int32).at[dest].set(flat_tok[order])
    slot_gate = jnp.zeros((P,), jnp.float32).at[dest].set(flat_g[order])
    block_e = jnp.minimum(
        jnp.searchsorted(padded_ends, jnp.arange(n_blocks, dtype=jnp.int32) * MOE_BLOCK, side="right"),
        N_EXPERTS - 1)
    x_pad = jnp.concatenate([xf, jnp.zeros((1, D), xf.dtype)], axis=0)
    xb = x_pad[slot_tok].reshape(n_blocks, MOE_BLOCK, D)

    def expert_block(args):
        xblk, e = args
        return swiglu(xblk, wg[e], wu[e], wd[e])

    yb = lax.map(expert_block, (xb, block_e)).reshape(P, D)
    y = jnp.zeros((T + 1, D), jnp.float32).at[slot_tok].add(yb.astype(jnp.float32) * slot_gate[:, None])[:T]
    return y.astype(x.dtype).reshape(B, S, D)


def setup_inputs(seed: int = 0) -> dict:
    key = jax.random.key(seed)
    keys = iter(jax.random.split(key, 48))
    f32 = jnp.float32
    NE = (DEPTH + 1) // 2
    NO = DEPTH // 2

    def w(shape, fan_in):
        return jax.random.normal(next(keys), shape, f32) * (fan_in ** -0.5)

    def gain(shape):
        return 1.0 + 0.02 * jax.random.normal(next(keys), shape, f32)

    def small(shape, scale):
        return scale * jax.random.normal(next(keys), shape, f32)

    return {
        "x": jax.random.normal(next(keys), (BATCH, SEQ, D_MODEL), f32),
        "mem": jax.random.normal(next(keys), (BATCH, N_MEM, D_MODEL), f32),
        "rel_bias": small((REL_BUCKETS, DIFF_HEADS), 0.2),
        "mem_norm_g": gain((D_MODEL,)),
        "norm_mix_g": gain((DEPTH, D_MODEL)),
        "norm_cross_g": gain((DEPTH, D_MODEL)),
        "norm_ffn_g": gain((DEPTH, D_MODEL)),
        "cross_wq": w((DEPTH, D_MODEL, CROSS_HEADS * CROSS_HD), D_MODEL),
        "cross_wkv": w((DEPTH, D_MODEL, 2 * CROSS_HEADS * CROSS_HD), D_MODEL),
        "cross_wo": w((DEPTH, CROSS_HEADS * CROSS_HD, D_MODEL), CROSS_HEADS * CROSS_HD),
        "ev_w_in": w((NE, D_MODEL, EV_IN), D_MODEL),
        "ev_conv_w": w((NE, CONV_WIDTH, A_WIDTH), CONV_WIDTH),
        "ev_conv_b": small((NE, A_WIDTH), 0.02),
        "ev_ln_g": gain((NE, A_WIDTH)),
        "ev_ln_b": small((NE, A_WIDTH), 0.02),
        "ev_q_norm_g": gain((NE, MLA_Q_RANK)),
        "ev_w_uq": w((NE, MLA_Q_RANK, MLA_HEADS * (MLA_NOPE + MLA_ROPE)), MLA_Q_RANK),
        "ev_kv_norm_g": gain((NE, MLA_KV_RANK)),
        "ev_w_ukv": w((NE, MLA_KV_RANK, MLA_HEADS * (MLA_NOPE + MLA_V)), MLA_KV_RANK),
        "ev_w_out": w((NE, EV_OUT, D_MODEL), EV_OUT),
        "ev_ffn_wg": w((NE, D_MODEL, D_FF), D_MODEL),
        "ev_ffn_wu": w((NE, D_MODEL, D_FF), D_MODEL),
        "ev_ffn_wd": w((NE, D_FF, D_MODEL), D_FF),
        "od_w_in": w((NO, D_MODEL, OD_IN), D_MODEL),
        "od_lambda_q1": small((NO, DIFF_HD), 0.1),
        "od_lambda_k1": small((NO, DIFF_HD), 0.1),
        "od_lambda_q2": small((NO, DIFF_HD), 0.1),
        "od_lambda_k2": small((NO, DIFF_HD), 0.1),
        "od_subln_g": gain((NO, 2 * DIFF_HD)),
        "od_w_out": w((NO, D_MODEL, D_MODEL), D_MODEL),
        "od_router": w((NO, D_MODEL, N_EXPERTS), D_MODEL),
        "od_moe_wg": w((NO, N_EXPERTS, D_MODEL, D_FF_EXPERT), D_MODEL),
        "od_moe_wu": w((NO, N_EXPERTS, D_MODEL, D_FF_EXPERT), D_MODEL),
        "od_moe_wd": w((NO, N_EXPERTS, D_FF_EXPERT, D_MODEL), D_FF_EXPERT),
        "final_norm_g": gain((D_MODEL,)),
    }


def reference(x, mem, rel_bias, mem_norm_g, norm_mix_g, norm_cross_g, norm_ffn_g,
              cross_wq, cross_wkv, cross_wo,
              ev_w_in, ev_conv_w, ev_conv_b, ev_ln_g, ev_ln_b, ev_q_norm_g, ev_w_uq,
              ev_kv_norm_g, ev_w_ukv, ev_w_out, ev_ffn_wg, ev_ffn_wu, ev_ffn_wd,
              od_w_in, od_lambda_q1, od_lambda_k1, od_lambda_q2, od_lambda_k2, od_subln_g,
              od_w_out, od_router, od_moe_wg, od_moe_wu, od_moe_wd, final_norm_g):
    S = x.shape[1]
    cos, sin = rope_tables(S)
    memn = rmsnorm(mem, mem_norm_g)
    h = x
    o1 = 2 * A_WIDTH
    o2 = o1 + MLA_Q_RANK
    o3 = o2 + MLA_KV_RANK
    for layer in range(DEPTH):
        i = layer // 2
        hn = rmsnorm(h, norm_mix_g[layer])
        if layer % 2 == 0:
            z = hn @ ev_w_in[i]
            a_out = conformer_conv(z[..., :A_WIDTH], z[..., A_WIDTH:o1],
                                   ev_conv_w[i], ev_conv_b[i], ev_ln_g[i], ev_ln_b[i])
            b_out = mla_attention(z[..., o1:o2], z[..., o2:o3], z[..., o3:],
                                  ev_q_norm_g[i], ev_w_uq[i], ev_kv_norm_g[i], ev_w_ukv[i], cos, sin)
            mix = jnp.concatenate([a_out, b_out], axis=-1) @ ev_w_out[i]
        else:
            lambda_init = 0.8 - 0.6 * math.exp(-0.3 * layer)
            mix = diff_attention(hn @ od_w_in[i], od_lambda_q1[i], od_lambda_k1[i],
                                 od_lambda_q2[i], od_lambda_k2[i], od_subln_g[i],
                                 rel_bias, lambda_init) @ od_w_out[i]
        h = h + mix
        h = h + memory_cross_attention(rmsnorm(h, norm_cross_g[layer]), memn,
                                       cross_wq[layer], cross_wkv[layer], cross_wo[layer])
        hn = rmsnorm(h, norm_ffn_g[layer])
        if layer % 2 == 0:
            h = h + swiglu(hn, ev_ffn_wg[i], ev_ffn_wu[i], ev_ffn_wd[i])
        else:
            h = h + moe_swiglu(hn, od_router[i], od_moe_wg[i], od_moe_wu[i], od_moe_wd[i])
    return rmsnorm(h, final_norm_g)
```

```python
import functools
import math

import jax
import jax.numpy as jnp
from jax import lax
from jax.experimental import pallas as pl
from jax.experimental.pallas import tpu as pltpu

F32 = jnp.float32
BF16 = jnp.bfloat16

EPS = 1e-6
NEG = -1e30
CHUNK = 64

CONV_WIDTH = 31
CONV_HALO = 32
MLA_HEADS = 8
MLA_NOPE = 128
MLA_ROPE = 64
MLA_V = 128
MLA_RANK = 512
MLA_QK_PAD = 256
ROPE_THETA = 10000.0
DIFF_HEADS = 8
DIFF_HD = 128
REL_BUCKETS = 32
REL_MAX_DIST = 128
CROSS_HEADS = 4
CROSS_HD = 128
N_EXPERTS = 8
LANE = 128

VMEM_LIMIT = 56 * 1024 * 1024

NT_DIMS = (((1,), (1,)), ((), ()))


def _cparams(*sem):
    return pltpu.CompilerParams(dimension_semantics=sem, vmem_limit_bytes=VMEM_LIMIT)


def _rms(x, g):
    return x * lax.rsqrt(jnp.mean(x * x, axis=-1, keepdims=True) + EPS) * g


def _dot(a, b):
    return jnp.dot(a, b, preferred_element_type=F32)


def _norm_matmul_kernel(x_ref, g_ref, w_ref, o_ref, xn_ref):
    @pl.when(pl.program_id(1) == 0)
    def _():
        xn_ref[...] = _rms(x_ref[...], g_ref[...]).astype(BF16)

    o_ref[...] = _dot(xn_ref[...], w_ref[...]).astype(o_ref.dtype)


def norm_matmul(x, g, w, *, tm, tn):
    M, K = x.shape
    N = w.shape[1]
    tm = min(tm, M)
    return pl.pallas_call(
        _norm_matmul_kernel,
        grid=(M // tm, N // tn),
        in_specs=[
            pl.BlockSpec((tm, K), lambda i, j: (i, 0)),
            pl.BlockSpec((1, K), lambda i, j: (0, 0)),
            pl.BlockSpec((K, tn), lambda i, j: (0, j)),
        ],
        out_specs=pl.BlockSpec((tm, tn), lambda i, j: (i, j)),
        out_shape=jax.ShapeDtypeStruct((M, N), BF16),
        scratch_shapes=[pltpu.VMEM((tm, K), BF16)],
        compiler_params=_cparams("parallel", "arbitrary"),
        name="norm_matmul",
    )(x, g.reshape(1, K), w)


def _matmul_res_kernel(*refs, n):
    res_ref = refs[0]
    o_ref = refs[1 + 2 * n]
    acc = res_ref[...]
    for k in range(n):
        acc = acc + _dot(refs[1 + k][...], refs[1 + n + k][...])
    o_ref[...] = acc


def matmul_res(res, a_list, w_list, *, tm, tn):
    M, N = res.shape
    tm = min(tm, M)
    n = len(a_list)
    in_specs = [pl.BlockSpec((tm, tn), lambda i, j: (i, j))]
    in_specs += [pl.BlockSpec((tm, a.shape[1]), lambda i, j: (i, 0)) for a in a_list]
    in_specs += [pl.BlockSpec((w.shape[0], tn), lambda i, j: (0, j)) for w in w_list]
    return pl.pallas_call(
        functools.partial(_matmul_res_kernel, n=n),
        grid=(M // tm, N // tn),
        in_specs=in_specs,
        out_specs=pl.BlockSpec((tm, tn), lambda i, j: (i, j)),
        out_shape=jax.ShapeDtypeStruct((M, N), F32),
        compiler_params=_cparams("parallel", "arbitrary"),
        name="matmul_res",
    )(res, *a_list, *w_list)


def _conv_kernel(val_ref, gate_ref, w_ref, b_ref, lg_ref, lb_ref, o_ref, ubuf, *, ts):
    s = pl.program_id(1)

    @pl.when(s == 0)
    def _():
        ubuf[0:CONV_HALO, :] = jnp.zeros((CONV_HALO, ubuf.shape[1]), F32)

    @pl.when(s > 0)
    def _():
        ubuf[0:CONV_HALO, :] = ubuf[ts:ts + CONV_HALO, :]

    val = val_ref[0].astype(F32)
    gate = gate_ref[0].astype(F32)
    ubuf[CONV_HALO:CONV_HALO + ts, :] = val * jax.nn.sigmoid(gate)

    base = CONV_HALO - (CONV_WIDTH - 1)
    acc = jnp.zeros((ts, ubuf.shape[1]), F32) + b_ref[...]
    for j in range(CONV_WIDTH):
        acc = acc + w_ref[j:j + 1, :] * ubuf[base + j:base + j + ts, :]

    mu = jnp.mean(acc, axis=-1, keepdims=True)
    xc = acc - mu
    y = xc * lax.rsqrt(jnp.mean(xc * xc, axis=-1, keepdims=True) + EPS)
    y = y * lg_ref[...] + lb_ref[...]
    o_ref[0] = (y * jax.nn.sigmoid(y)).astype(o_ref.dtype)


def conformer_conv(z, conv_w, conv_b, ln_g, ln_b, *, ts):
    B, S, _ = z.shape
    C = conv_w.shape[1]
    ts = min(ts, S)
    wpad = jnp.zeros((CONV_HALO, C), F32).at[:CONV_WIDTH].set(conv_w)
    row = lambda v: v.reshape(1, C)
    const = lambda shape: pl.BlockSpec(shape, lambda b, s: (0, 0))
    return pl.pallas_call(
        functools.partial(_conv_kernel, ts=ts),
        grid=(B, S // ts),
        in_specs=[
            pl.BlockSpec((1, ts, C), lambda b, s: (b, s, 0)),
            pl.BlockSpec((1, ts, C), lambda b, s: (b, s, 1)),
            const((CONV_HALO, C)), const((1, C)), const((1, C)), const((1, C)),
        ],
        out_specs=pl.BlockSpec((1, ts, C), lambda b, s: (b, s, 0)),
        out_shape=jax.ShapeDtypeStruct((B, S, C), BF16),
        scratch_shapes=[pltpu.VMEM((ts + CONV_HALO, C), F32)],
        compiler_params=_cparams("parallel", "arbitrary"),
        name="conformer_conv",
    )(z, z, wpad, row(conv_b), row(ln_g), row(ln_b))


def _mla_proj_kernel(cq_ref, ckv_ref, kr_ref, qg_ref, kvg_ref, wqa_ref, wqb_ref, wk_ref, wv_ref,
                     ct_ref, st_ref, q_ref, k_ref, v_ref):
    cqn = _rms(cq_ref[...].astype(F32), qg_ref[...]).astype(BF16)
    ckvn = _rms(ckv_ref[...].astype(F32), kvg_ref[...]).astype(BF16)
    ct = ct_ref[...]
    st = st_ref[...]

    qa = _dot(cqn, wqa_ref[...])
    qb = _dot(cqn, wqb_ref[...])
    kn = _dot(ckvn, wk_ref[...])
    v_ref[...] = _dot(ckvn, wv_ref[...]).astype(BF16)

    kr = kr_ref[...].astype(F32)
    k_rope = (kr * ct + pltpu.roll(kr, LANE // 2, axis=1) * st).astype(BF16)

    for h in range(MLA_HEADS):
        o = h * MLA_QK_PAD
        q_ref[:, o:o + LANE] = qa[:, o:o + LANE].astype(BF16)
        q_ref[:, o + LANE:o + 2 * LANE] = (
            qa[:, o + LANE:o + 2 * LANE] * ct + qb[:, h * LANE:(h + 1) * LANE] * st).astype(BF16)
        k_ref[:, o:o + LANE] = kn[:, h * LANE:(h + 1) * LANE].astype(BF16)
        k_ref[:, o + LANE:o + 2 * LANE] = k_rope


def mla_proj(z, q_norm_g, kv_norm_g, wqa, wqb, wk, wv, ct, st, *, seq, tm):
    T = z.shape[0]
    tm = min(tm, seq)
    n_s = seq // tm
    R = MLA_RANK
    cq_blk = (2 * 1024) // R
    kr_blk = (2 * 1024 + 2 * R) // LANE
    const = lambda shape: pl.BlockSpec(shape, lambda i: (0, 0))
    HQ = MLA_HEADS * MLA_QK_PAD
    HV = MLA_HEADS * MLA_V
    return pl.pallas_call(
        _mla_proj_kernel,
        grid=(T // tm,),
        in_specs=[
            pl.BlockSpec((tm, R), lambda i: (i, cq_blk)),
            pl.BlockSpec((tm, R), lambda i: (i, cq_blk + 1)),
            pl.BlockSpec((tm, LANE), lambda i: (i, kr_blk)),
            const((1, R)), const((1, R)),
            const((R, HQ)), const((R, MLA_HEADS * LANE)), const((R, HV)), const((R, HV)),
            pl.BlockSpec((tm, LANE), lambda i: (i % n_s, 0)),
            pl.BlockSpec((tm, LANE), lambda i: (i % n_s, 0)),
        ],
        out_specs=[
            pl.BlockSpec((tm, HQ), lambda i: (i, 0)),
            pl.BlockSpec((tm, HQ), lambda i: (i, 0)),
            pl.BlockSpec((tm, HV), lambda i: (i, 0)),
        ],
        out_shape=[
            jax.ShapeDtypeStruct((T, HQ), BF16),
            jax.ShapeDtypeStruct((T, HQ), BF16),
            jax.ShapeDtypeStruct((T, HV), BF16),
        ],
        compiler_params=_cparams("parallel"),
        name="mla_proj",
    )(z, z, z, q_norm_g.reshape(1, R), kv_norm_g.reshape(1, R), wqa, wqb, wk, wv, ct, st)


def _online_update(carry, s, v):
    m, l, acc = carry
    m_new = jnp.maximum(m, jnp.max(s, axis=-1, keepdims=True))
    alpha = jnp.exp(m - m_new)
    p = jnp.exp(s - m_new)
    l = alpha * l + jnp.sum(p, axis=-1, keepdims=True)
    acc = alpha * acc + _dot(p.astype(BF16), v)
    return m_new, l, acc


def _init_carry(tq, dv):
    return (jnp.full((tq, 1), NEG, F32), jnp.zeros((tq, 1), F32), jnp.zeros((tq, dv), F32))


def _mla_attn_kernel(q_ref, k_ref, v_ref, o_ref, *, tq):
    i = pl.program_id(2)
    q = q_ref[0]

    def scores(j):
        start = pl.multiple_of(j * tq, tq)
        k = k_ref[0, pl.ds(start, tq), :]
        v = v_ref[0, pl.ds(start, tq), :]
        return lax.dot_general(q, k, NT_DIMS, preferred_element_type=F32), v

    def body(j, carry):
        s, v = scores(j)
        return _online_update(carry, s, v)

    carry = lax.fori_loop(0, i, body, _init_carry(tq, MLA_V))
    s, v = scores(i)
    qc = lax.broadcasted_iota(jnp.int32, (tq, tq), 0) // CHUNK
    kc = lax.broadcasted_iota(jnp.int32, (tq, tq), 1) // CHUNK
    s = jnp.where(kc <= qc, s, NEG)
    _, l, acc = _online_update(carry, s, v)
    o_ref[0] = (acc / l).astype(o_ref.dtype)


def mla_attention(q, k, v, *, tq):
    B, S, _ = q.shape
    tq = min(tq, S)
    return pl.pallas_call(
        functools.partial(_mla_attn_kernel, tq=tq),
        grid=(B, MLA_HEADS, S // tq),
        in_specs=[
            pl.BlockSpec((1, tq, MLA_QK_PAD), lambda b, h, i: (b, i, h)),
            pl.BlockSpec((1, S, MLA_QK_PAD), lambda b, h, i: (b, 0, h)),
            pl.BlockSpec((1, S, MLA_V), lambda b, h, i: (b, 0, h)),
        ],
        out_specs=pl.BlockSpec((1, tq, MLA_V), lambda b, h, i: (b, i, h)),
        out_shape=jax.ShapeDtypeStruct((B, S, MLA_HEADS * MLA_V), BF16),
        compiler_params=_cparams("parallel", "parallel", "arbitrary"),
        name="mla_attention",
    )(q, k, v)


def _diff_attn_kernel(lam_ref, q_ref, k_ref, v_ref, bias_ref, g_ref, o_ref, *, tq, out_scale):
    i = pl.program_id(2)
    q = q_ref[0]
    q1 = q[:, :DIFF_HD]
    q2 = q[:, DIFF_HD:]

    def body(j, carry):
        c1, c2 = carry
        start = pl.multiple_of(j * tq, tq)
        k = k_ref[0, pl.ds(start, tq), :]
        v = v_ref[0, pl.ds(start, tq), :]
        bias = bias_ref[0, jnp.clip(j - i + 2, 0, 2)]
        s1 = lax.dot_general(q1, k[:, :DIFF_HD], NT_DIMS, preferred_element_type=F32) + bias
        s2 = lax.dot_general(q2, k[:, DIFF_HD:], NT_DIMS, preferred_element_type=F32) + bias
        return _online_update(c1, s1, v), _online_update(c2, s2, v)

    dv = 2 * DIFF_HD
    (_, l1, a1), (_, l2, a2) = lax.fori_loop(0, i + 1, body, (_init_carry(tq, dv), _init_carry(tq, dv)))
    o = a1 / l1 - lam_ref[0] * (a2 / l2)
    o_ref[0] = (_rms(o, g_ref[...]) * out_scale).astype(o_ref.dtype)


def diff_attention(qkv, lam, bias_tiles, subln_g, *, tq, out_scale):
    B, S, _ = qkv.shape
    H = DIFF_HEADS
    dv = 2 * DIFF_HD
    return pl.pallas_call(
        functools.partial(_diff_attn_kernel, tq=tq, out_scale=out_scale),
        grid=(B, H, S // tq),
        in_specs=[
            pl.BlockSpec(memory_space=pltpu.SMEM),
            pl.BlockSpec((1, tq, dv), lambda b, h, i: (b, i, h)),
            pl.BlockSpec((1, S, dv), lambda b, h, i: (b, 0, H + h)),
            pl.BlockSpec((1, S, dv), lambda b, h, i: (b, 0, 2 * H + h)),
            pl.BlockSpec((1, 3, tq, tq), lambda b, h, i: (h, 0, 0, 0)),
            pl.BlockSpec((1, dv), lambda b, h, i: (0, 0)),
        ],
        out_specs=pl.BlockSpec((1, tq, dv), lambda b, h, i: (b, i, h)),
        out_shape=jax.ShapeDtypeStruct((B, S, H * dv), BF16),
        compiler_params=_cparams("parallel", "parallel", "arbitrary"),
        name="diff_attention",
    )(lam, qkv, qkv, qkv, bias_tiles, subln_g.reshape(1, dv))


def _cross_kernel(h_ref, g_ref, wq_ref, kv_ref, wo_ref, o_ref):
    h = h_ref[0]
    hn = _rms(h, g_ref[...]).astype(BF16)
    q = _dot(hn, wq_ref[...]).astype(BF16)
    kv = kv_ref[0]
    HD = CROSS_HEADS * CROSS_HD
    outs = []
    for hd in range(CROSS_HEADS):
        lo = hd * CROSS_HD
        s = lax.dot_general(q[:, lo:lo + CROSS_HD], kv[:, lo:lo + CROSS_HD], NT_DIMS,
                            preferred_element_type=F32)
        m = jnp.max(s, axis=-1, keepdims=True)
        p = jnp.exp(s - m)
        l = jnp.sum(p, axis=-1, keepdims=True)
        o = _dot(p.astype(BF16), kv[:, HD + lo:HD + lo + CROSS_HD]) / l
        outs.append(o.astype(BF16))
    o_all = jnp.concatenate(outs, axis=-1)
    o_ref[0] = h + _dot(o_all, wo_ref[...])


def cross_attention(h, g, wq, kv, wo, *, kv_blk, tm):
    B, S, D = h.shape
    M = kv.shape[1]
    HD = CROSS_HEADS * CROSS_HD
    tm = min(tm, S)
    const = lambda shape: pl.BlockSpec(shape, lambda b, s: (0, 0))
    return pl.pallas_call(
        _cross_kernel,
        grid=(B, S // tm),
        in_specs=[
            pl.BlockSpec((1, tm, D), lambda b, s: (b, s, 0)),
            const((1, D)), const((D, HD)),
            pl.BlockSpec((1, M, 2 * HD), lambda b, s: (b, 0, kv_blk)),
            const((HD, D)),
        ],
        out_specs=pl.BlockSpec((1, tm, D), lambda b, s: (b, s, 0)),
        out_shape=jax.ShapeDtypeStruct((B, S, D), F32),
        compiler_params=_cparams("parallel", "arbitrary"),
        name="cross_attention",
    )(h, g.reshape(1, D), wq, kv, wo)


def _ffn_kernel(h_ref, g_ref, wg_ref, wu_ref, wd_ref, o_ref, xn_ref, acc_ref):
    f = pl.program_id(1)

    @pl.when(f == 0)
    def _():
        xn_ref[...] = _rms(h_ref[...], g_ref[...]).astype(BF16)
        acc_ref[...] = h_ref[...]

    x = xn_ref[...]
    gt = _dot(x, wg_ref[...])
    up = _dot(x, wu_ref[...])
    hm = (gt * jax.nn.sigmoid(gt) * up).astype(BF16)
    acc_ref[...] += _dot(hm, wd_ref[...])

    @pl.when(f == pl.num_programs(1) - 1)
    def _():
        o_ref[...] = acc_ref[...]


def dense_ffn(h, g, wg, wu, wd, *, tm, tf):
    T, D = h.shape
    F = wg.shape[1]
    tm = min(tm, T)
    return pl.pallas_call(
        _ffn_kernel,
        grid=(T // tm, F // tf),
        in_specs=[
            pl.BlockSpec((tm, D), lambda i, f: (i, 0)),
            pl.BlockSpec((1, D), lambda i, f: (0, 0)),
            pl.BlockSpec((D, tf), lambda i, f: (0, f)),
            pl.BlockSpec((D, tf), lambda i, f: (0, f)),
            pl.BlockSpec((tf, D), lambda i, f: (f, 0)),
        ],
        out_specs=pl.BlockSpec((tm, D), lambda i, f: (i, 0)),
        out_shape=jax.ShapeDtypeStruct((T, D), F32),
        scratch_shapes=[pltpu.VMEM((tm, D), BF16), pltpu.VMEM((tm, D), F32)],
        compiler_params=_cparams("parallel", "arbitrary"),
        name="dense_ffn",
    )(h, g.reshape(1, D), wg, wu, wd)


def _router_kernel(h_ref, g_ref, wr_ref, xn_ref, route_ref):
    xn = _rms(h_ref[...], g_ref[...])
    xn_ref[...] = xn
    logits = jnp.dot(xn, wr_ref[...], preferred_element_type=F32, precision=lax.Precision.HIGHEST)
    lane = lax.broadcasted_iota(jnp.int32, logits.shape, 1)
    logits = jnp.where(lane < N_EXPERTS, logits, -jnp.inf)
    v1 = jnp.max(logits, axis=-1, keepdims=True)
    i1 = jnp.min(jnp.where(logits == v1, lane, LANE), axis=-1, keepdims=True)
    rest = jnp.where(lane == i1, -jnp.inf, logits)
    v2 = jnp.max(rest, axis=-1, keepdims=True)
    i2 = jnp.min(jnp.where(rest == v2, lane, LANE), axis=-1, keepdims=True)
    e2 = jnp.exp(v2 - v1)
    g1 = 1.0 / (1.0 + e2)
    g2 = e2 / (1.0 + e2)
    route = jnp.where(lane == 0, i1.astype(F32), 0.0)
    route = jnp.where(lane == 1, i2.astype(F32), route)
    route = jnp.where(lane == 2, g1, route)
    route = jnp.where(lane == 3, g2, route)
    route_ref[...] = route


def moe_router(h, g, w_router, *, tm):
    T, D = h.shape
    tm = min(tm, T)
    wr = jnp.zeros((D, LANE), F32).at[:, :N_EXPERTS].set(w_router)
    return pl.pallas_call(
        _router_kernel,
        grid=(T // tm,),
        in_specs=[
            pl.BlockSpec((tm, D), lambda i: (i, 0)),
            pl.BlockSpec((1, D), lambda i: (0, 0)),
            pl.BlockSpec((D, LANE), lambda i: (0, 0)),
        ],
        out_specs=[pl.BlockSpec((tm, D), lambda i: (i, 0)), pl.BlockSpec((tm, LANE), lambda i: (i, 0))],
        out_shape=[jax.ShapeDtypeStruct((T, D), F32), jax.ShapeDtypeStruct((T, LANE), F32)],
        compiler_params=_cparams("parallel"),
        name="moe_router",
    )(h, g.reshape(1, D), wr)


def _moe_kernel(tok_ref, be_ref, nu_ref, x_hbm, wg_ref, wu_ref, wd_ref, o_ref, xbuf, xb, acc, sem, *, tm):
    i = pl.program_id(0)
    f = pl.program_id(1)
    nf = pl.num_programs(1)
    nused = nu_ref[0]

    def row_copy(blk, slot, r):
        tok = tok_ref[blk * tm + r]
        return pltpu.make_async_copy(x_hbm.at[pl.ds(tok, 1)], xbuf.at[slot, pl.ds(r, 1)], sem.at[slot])

    def issue(blk, slot):
        def body(r, c):
            row_copy(blk, slot, r).start()
            return c
        lax.fori_loop(0, tm, body, 0)

    def wait(blk, slot):
        def body(r, c):
            row_copy(blk, slot, r).wait()
            return c
        lax.fori_loop(0, tm, body, 0)

    @pl.when(i < nused)
    def _():
        @pl.when(f == 0)
        def _():
            slot = i % 2

            @pl.when(i == 0)
            def _():
                issue(0, 0)

            wait(i, slot)

            @pl.when(i + 1 < nused)
            def _():
                issue(i + 1, 1 - slot)

            xb[...] = xbuf[slot].astype(BF16)
            acc[...] = jnp.zeros(acc.shape, F32)

        x = xb[...]
        gt = _dot(x, wg_ref[0])
        up = _dot(x, wu_ref[0])
        hm = (gt * jax.nn.sigmoid(gt) * up).astype(BF16)
        acc[...] += _dot(hm, wd_ref[0])

        @pl.when(f == nf - 1)
        def _():
            o_ref[...] = acc[...]

    @pl.when((i >= nused) & (f == nf - 1))
    def _():
        o_ref[...] = jnp.zeros(o_ref.shape, F32)


def moe_experts(xn, slot_tok, block_e, nused, wg, wu, wd, *, tm, tf):
    T, D = xn.shape
    P = slot_tok.shape[0]
    F = wg.shape[2]
    nf = F // tf

    def w_cols(i, f, tok, be, nu):
        return be[jnp.minimum(i, nu[0] - 1)], 0, jnp.where(i < nu[0], f, nf - 1)

    def w_rows(i, f, tok, be, nu):
        return be[jnp.minimum(i, nu[0] - 1)], jnp.where(i < nu[0], f, nf - 1), 0

    return pl.pallas_call(
        functools.partial(_moe_kernel, tm=tm),
        grid_spec=pltpu.PrefetchScalarGridSpec(
            num_scalar_prefetch=3,
            grid=(P // tm, nf),
            in_specs=[
                pl.BlockSpec(memory_space=pl.ANY),
                pl.BlockSpec((1, D, tf), w_cols),
                pl.BlockSpec((1, D, tf), w_cols),
                pl.BlockSpec((1, tf, D), w_rows),
            ],
            out_specs=pl.BlockSpec((tm, D), lambda i, f, tok, be, nu: (i, 0)),
            scratch_shapes=[
                pltpu.VMEM((2, tm, D), F32),
                pltpu.VMEM((tm, D), BF16),
                pltpu.VMEM((tm, D), F32),
                pltpu.SemaphoreType.DMA((2,)),
            ],
        ),
        out_shape=jax.ShapeDtypeStruct((P, D), F32),
        compiler_params=_cparams("arbitrary", "arbitrary"),
        name="moe_experts",
    )(slot_tok, block_e, nused, xn, wg, wu, wd)


def _combine_kernel(pos_ref, h_ref, route_ref, g_ref, ys_hbm, o_ref, buf, sem, *, tm):
    i = pl.program_id(0)

    def row_copy(r, k):
        p = pos_ref[2 * (i * tm + r) + k]
        return pltpu.make_async_copy(ys_hbm.at[pl.ds(p, 1)], buf.at[k, pl.ds(r, 1)], sem.at[k])

    def issue(r, c):
        row_copy(r, 0).start()
        row_copy(r, 1).start()
        return c

    def wait(r, c):
        row_copy(r, 0).wait()
        row_copy(r, 1).wait()
        return c

    lax.fori_loop(0, tm, issue, 0)
    lax.fori_loop(0, tm, wait, 0)
    route = route_ref[...]
    y = route[:, 2:3] * buf[0] + route[:, 3:4] * buf[1]
    o_ref[...] = _rms(h_ref[...] + y, g_ref[...])


def moe_combine_norm(h, route, ys, pos, g, *, tm):
    T, D = h.shape
    tm = min(tm, T)
    return pl.pallas_call(
        functools.partial(_combine_kernel, tm=tm),
        grid_spec=pltpu.PrefetchScalarGridSpec(
            num_scalar_prefetch=1,
            grid=(T // tm,),
            in_specs=[
                pl.BlockSpec((tm, D), lambda i, pos: (i, 0)),
                pl.BlockSpec((tm, LANE), lambda i, pos: (i, 0)),
                pl.BlockSpec((1, D), lambda i, pos: (0, 0)),
                pl.BlockSpec(memory_space=pl.ANY),
            ],
            out_specs=pl.BlockSpec((tm, D), lambda i, pos: (i, 0)),
            scratch_shapes=[pltpu.VMEM((2, tm, D), F32), pltpu.SemaphoreType.DMA((2,))],
        ),
        out_shape=jax.ShapeDtypeStruct((T, D), F32),
        compiler_params=_cparams("arbitrary"),
        name="moe_combine_norm",
    )(pos, h, route, g.reshape(1, D), ys)


def _dispatch(route, tm):
    T = route.shape[0]
    A = 2 * T
    P = A + N_EXPERTS * tm
    flat_e = route[:, :2].astype(jnp.int32).reshape(A)
    onehot = (flat_e[:, None] == jnp.arange(N_EXPERTS, dtype=jnp.int32)[None, :]).astype(jnp.int32)
    csum = jnp.cumsum(onehot, axis=0)
    counts = csum[-1]
    rank = jnp.sum(csum * onehot, axis=1) - 1
    padded = ((counts + tm - 1) // tm) * tm
    pend = jnp.cumsum(padded)
    pstart = pend - padded
    dest = (jnp.sum(onehot * pstart[None, :], axis=1) + rank).astype(jnp.int32)
    slot_tok = jnp.zeros((P,), jnp.int32).at[dest].set(jnp.arange(A, dtype=jnp.int32) // 2)
    nblk = P // tm
    block_e = jnp.minimum(
        jnp.searchsorted(pend, jnp.arange(nblk, dtype=jnp.int32) * tm, side="right"),
        N_EXPERTS - 1).astype(jnp.int32)
    nused = (pend[-1] // tm).astype(jnp.int32).reshape(1)
    return slot_tok, dest, block_e, nused


def _rope_slabs(seq):
    pos = jnp.arange(seq, dtype=F32)
    inv = jnp.power(ROPE_THETA, -jnp.arange(0, MLA_ROPE, 2, dtype=F32) / MLA_ROPE)
    ang = pos[:, None] * inv[None, :]
    z = jnp.zeros((seq, LANE - MLA_ROPE), F32)
    ct = jnp.concatenate([jnp.cos(ang), jnp.cos(ang), z], axis=1)
    st = jnp.concatenate([jnp.sin(ang), jnp.sin(ang), z], axis=1)
    return ct, st


def _t5_bucket(rel):
    half = REL_BUCKETS // 2
    max_exact = half // 2
    ret = (rel > 0).astype(jnp.int32) * half
    n = jnp.abs(rel)
    nf = jnp.maximum(n, 1).astype(F32)
    large = max_exact + (jnp.log(nf / max_exact) / math.log(REL_MAX_DIST / max_exact)
                         * (half - max_exact)).astype(jnp.int32)
    large = jnp.minimum(large, half - 1)
    return ret + jnp.where(n < max_exact, n, large)


def _bias_tiles(rel_bias, tq):
    assert tq >= REL_MAX_DIST
    qi = jnp.arange(tq, dtype=jnp.int32)[:, None]
    ki = jnp.arange(tq, dtype=jnp.int32)[None, :]
    far = jnp.broadcast_to(jnp.int32(-2 * tq), (tq, tq))
    prev = ki - qi - tq
    diag = ki - qi
    tiles = []
    for rel in (far, prev, diag):
        tiles.append(jnp.moveaxis(rel_bias[_t5_bucket(rel)], -1, 0).astype(F32))
    mask = (ki // CHUNK) <= (qi // CHUNK)
    tiles[2] = jnp.where(mask[None], tiles[2], NEG)
    return jnp.stack(tiles, axis=1)


def kernel(x, mem, rel_bias, mem_norm_g, norm_mix_g, norm_cross_g, norm_ffn_g, cross_wq, cross_wkv, cross_wo, ev_w_in, ev_conv_w, ev_conv_b, ev_ln_g, ev_ln_b, ev_q_norm_g, ev_w_uq, ev_kv_norm_g, ev_w_ukv, ev_w_out, ev_ffn_wg, ev_ffn_wu, ev_ffn_wd, od_w_in, od_lambda_q1, od_lambda_k1, od_lambda_q2, od_lambda_k2, od_subln_g, od_w_out, od_router, od_moe_wg, od_moe_wu, od_moe_wd, final_norm_g):
    B, S, D = x.shape
    T = B * S
    M = mem.shape[1]
    AW = ev_conv_w.shape[2]
    H = MLA_HEADS
    R = MLA_RANK
    h = x.reshape(T, D)

    w_in0 = ev_w_in[0]
    kr0 = 2 * AW + 2 * R
    half = MLA_ROPE // 2
    w_in0 = jnp.concatenate(
        [w_in0, -w_in0[:, kr0 + half:kr0 + MLA_ROPE], w_in0[:, kr0:kr0 + half]], axis=1).astype(BF16)

    q_scale = (MLA_NOPE + MLA_ROPE) ** -0.5
    wuq = (ev_w_uq[0] * q_scale).reshape(R, H, MLA_NOPE + MLA_ROPE)
    w_nope, w_r1, w_r2 = wuq[..., :MLA_NOPE], wuq[..., MLA_NOPE:MLA_NOPE + half], wuq[..., MLA_NOPE + half:]
    zq = jnp.zeros((R, H, MLA_QK_PAD - MLA_NOPE - MLA_ROPE), F32)
    wqa = jnp.concatenate([w_nope, w_r1, w_r2, zq], axis=-1).reshape(R, H * MLA_QK_PAD).astype(BF16)
    wqb = jnp.concatenate([-w_r2, w_r1, zq], axis=-1).reshape(R, H * LANE).astype(BF16)
    wukv = ev_w_ukv[0].reshape(R, H, MLA_NOPE + MLA_V)
    wk = wukv[..., :MLA_NOPE].reshape(R, H * MLA_NOPE).astype(BF16)
    wv = wukv[..., MLA_NOPE:].reshape(R, H * MLA_V).astype(BF16)
    ct, st = _rope_slabs(S)

    w_out0 = ev_w_out[0].astype(BF16)
    c_scale = CROSS_HD ** -0.5
    wq_c = (cross_wq * c_scale).astype(BF16)
    wkv_c = jnp.concatenate([cross_wkv[0], cross_wkv[1]], axis=1).astype(BF16)
    wo_c = cross_wo.astype(BF16)

    d_scale = DIFF_HD ** -0.5
    w_in1 = jnp.concatenate([od_w_in[0][:, :D] * d_scale, od_w_in[0][:, D:]], axis=1).astype(BF16)
    layer = 1
    lambda_init = 0.8 - 0.6 * math.exp(-0.3 * layer)
    lam = (jnp.exp(jnp.sum(od_lambda_q1[0] * od_lambda_k1[0]))
           - jnp.exp(jnp.sum(od_lambda_q2[0] * od_lambda_k2[0])) + lambda_init).reshape(1).astype(F32)
    tq = min(256, S)
    bias_tiles = _bias_tiles(rel_bias, tq)

    kv_mem = norm_matmul(mem.reshape(B * M, D), mem_norm_g, wkv_c, tm=512, tn=512)
    kv_mem = kv_mem.reshape(B, M, -1)

    z = norm_matmul(h, norm_mix_g[0], w_in0, tm=512, tn=640)
    a_out = conformer_conv(z.reshape(B, S, -1), ev_conv_w[0], ev_conv_b[0], ev_ln_g[0], ev_ln_b[0], ts=256)
    q, k, v = mla_proj(z, ev_q_norm_g[0], ev_kv_norm_g[0], wqa, wqb, wk, wv, ct, st, seq=S, tm=512)
    b_out = mla_attention(q.reshape(B, S, -1), k.reshape(B, S, -1), v.reshape(B, S, -1), tq=tq)
    h = matmul_res(h, [a_out.reshape(T, AW), b_out.reshape(T, H * MLA_V)], [w_out0[:AW], w_out0[AW:]],
                   tm=512, tn=512)
    h = cross_attention(h.reshape(B, S, D), norm_cross_g[0], wq_c[0], kv_mem, wo_c[0], kv_blk=0, tm=512)
    h = dense_ffn(h.reshape(T, D), norm_ffn_g[0], ev_ffn_wg[0].astype(BF16), ev_ffn_wu[0].astype(BF16),
                  ev_ffn_wd[0].astype(BF16), tm=512, tf=512)

    qkv = norm_matmul(h, norm_mix_g[1], w_in1, tm=512, tn=512)
    o = diff_attention(qkv.reshape(B, S, -1), lam, bias_tiles, od_subln_g[0], tq=tq,
                       out_scale=1.0 - lambda_init)
    h = matmul_res(h, [o.reshape(T, D)], [od_w_out[0].astype(BF16)], tm=512, tn=512)
    h = cross_attention(h.reshape(B, S, D), norm_cross_g[1], wq_c[1], kv_mem, wo_c[1], kv_blk=1, tm=512)
    h = h.reshape(T, D)

    tm_moe = 512
    xn, route = moe_router(h, norm_ffn_g[1], od_router[0], tm=512)
    slot_tok, dest, block_e, nused = _dispatch(route, tm_moe)
    ys = moe_experts(xn, slot_tok, block_e, nused, od_moe_wg[0].astype(BF16), od_moe_wu[0].astype(BF16),
                     od_moe_wd[0].astype(BF16), tm=tm_moe, tf=512)
    out = moe_combine_norm(h, route, ys, dest, final_norm_g, tm=256)
    return out.reshape(B, S, D)
```

```python
import functools
import math

import jax
import jax.numpy as jnp
from jax import lax
from jax.experimental import pallas as pl
from jax.experimental.pallas import tpu as pltpu

F32 = jnp.float32
BF16 = jnp.bfloat16

EPS = 1e-6
NEG = -1e30
CHUNK = 64

CONV_WIDTH = 31
CONV_HALO = 32
MLA_HEADS = 8
MLA_NOPE = 128
MLA_ROPE = 64
MLA_V = 128
MLA_RANK = 512
MLA_QK_PAD = 256
ROPE_THETA = 10000.0
DIFF_HEADS = 8
DIFF_HD = 128
REL_BUCKETS = 32
REL_MAX_DIST = 128
CROSS_HEADS = 4
CROSS_HD = 128
N_EXPERTS = 8
LANE = 128

VMEM_LIMIT = 56 * 1024 * 1024

NT_DIMS = (((1,), (1,)), ((), ()))


def _cparams(*sem):
    return pltpu.CompilerParams(dimension_semantics=sem, vmem_limit_bytes=VMEM_LIMIT)


def _rms(x, g):
    return x * lax.rsqrt(jnp.mean(x * x, axis=-1, keepdims=True) + EPS) * g


def _dot(a, b):
    return jnp.dot(a, b, preferred_element_type=F32)


def _norm_matmul_kernel(x_ref, g_ref, w_ref, o_ref, xn_ref):
    @pl.when(pl.program_id(1) == 0)
    def _():
        xn_ref[...] = _rms(x_ref[...], g_ref[...]).astype(BF16)

    o_ref[...] = _dot(xn_ref[...], w_ref[...]).astype(o_ref.dtype)


def norm_matmul(x, g, w, *, tm, tn):
    M, K = x.shape
    N = w.shape[1]
    tm = min(tm, M)
    return pl.pallas_call(
        _norm_matmul_kernel,
        grid=(M // tm, N // tn),
        in_specs=[
            pl.BlockSpec((tm, K), lambda i, j: (i, 0)),
            pl.BlockSpec((1, K), lambda i, j: (0, 0)),
            pl.BlockSpec((K, tn), lambda i, j: (0, j)),
        ],
        out_specs=pl.BlockSpec((tm, tn), lambda i, j: (i, j)),
        out_shape=jax.ShapeDtypeStruct((M, N), BF16),
        scratch_shapes=[pltpu.VMEM((tm, K), BF16)],
        compiler_params=_cparams("parallel", "arbitrary"),
        name="norm_matmul",
    )(x, g.reshape(1, K), w)


def _matmul_res_kernel(*refs, n):
    res_ref = refs[0]
    o_ref = refs[1 + 2 * n]
    acc = res_ref[...]
    for k in range(n):
        acc = acc + _dot(refs[1 + k][...], refs[1 + n + k][...])
    o_ref[...] = acc


def matmul_res(res, a_list, w_list, *, tm, tn):
    M, N = res.shape
    tm = min(tm, M)
    n = len(a_list)
    in_specs = [pl.BlockSpec((tm, tn), lambda i, j: (i, j))]
    in_specs += [pl.BlockSpec((tm, a.shape[1]), lambda i, j: (i, 0)) for a in a_list]
    in_specs += [pl.BlockSpec((w.shape[0], tn), lambda i, j: (0, j)) for w in w_list]
    return pl.pallas_call(
        functools.partial(_matmul_res_kernel, n=n),
        grid=(M // tm, N // tn),
        in_specs=in_specs,
        out_specs=pl.BlockSpec((tm, tn), lambda i, j: (i, j)),
        out_shape=jax.ShapeDtypeStruct((M, N), F32),
        compiler_params=_cparams("parallel", "arbitrary"),
        name="matmul_res",
    )(res, *a_list, *w_list)


def _conv_kernel(val_ref, gate_ref, w_ref, b_ref, lg_ref, lb_ref, o_ref, ubuf, *, ts):
    s = pl.program_id(1)

    @pl.when(s == 0)
    def _():
        ubuf[0:CONV_HALO, :] = jnp.zeros((CONV_HALO, ubuf.shape[1]), F32)

    @pl.when(s > 0)
    def _():
        ubuf[0:CONV_HALO, :] = ubuf[ts:ts + CONV_HALO, :]

    val = val_ref[0].astype(F32)
    gate = gate_ref[0].astype(F32)
    ubuf[CONV_HALO:CONV_HALO + ts, :] = val * jax.nn.sigmoid(gate)

    base = CONV_HALO - (CONV_WIDTH - 1)
    acc = jnp.zeros((ts, ubuf.shape[1]), F32) + b_ref[...]
    for j in range(CONV_WIDTH):
        acc = acc + w_ref[j:j + 1, :] * ubuf[base + j:base + j + ts, :]

    mu = jnp.mean(acc, axis=-1, keepdims=True)
    xc = acc - mu
    y = xc * lax.rsqrt(jnp.mean(xc * xc, axis=-1, keepdims=True) + EPS)
    y = y * lg_ref[...] + lb_ref[...]
    o_ref[0] = (y * jax.nn.sigmoid(y)).astype(o_ref.dtype)


def conformer_conv(z, conv_w, conv_b, ln_g, ln_b, *, ts):
    B, S, _ = z.shape
    C = conv_w.shape[1]
    ts = min(ts, S)
    wpad = jnp.zeros((CONV_HALO, C), F32).at[:CONV_WIDTH].set(conv_w)
    row = lambda v: v.reshape(1, C)
    const = lambda shape: pl.BlockSpec(shape, lambda b, s: (0, 0))
    return pl.pallas_call(
        functools.partial(_conv_kernel, ts=ts),
        grid=(B, S // ts),
        in_specs=[
            pl.BlockSpec((1, ts, C), lambda b, s: (b, s, 0)),
            pl.BlockSpec((1, ts, C), lambda b, s: (b, s, 1)),
            const((CONV_HALO, C)), const((1, C)), const((1, C)), const((1, C)),
        ],
        out_specs=pl.BlockSpec((1, ts, C), lambda b, s: (b, s, 0)),
        out_shape=jax.ShapeDtypeStruct((B, S, C), BF16),
        scratch_shapes=[pltpu.VMEM((ts + CONV_HALO, C), F32)],
        compiler_params=_cparams("parallel", "arbitrary"),
        name="conformer_conv",
    )(z, z, wpad, row(conv_b), row(ln_g), row(ln_b))


def _mla_proj_kernel(cq_ref, ckv_ref, kr_ref, qg_ref, kvg_ref, wqa_ref, wqb_ref, wk_ref, wv_ref,
                     ct_ref, st_ref, q_ref, k_ref, v_ref):
    cqn = _rms(cq_ref[...].astype(F32), qg_ref[...]).astype(BF16)
    ckvn = _rms(ckv_ref[...].astype(F32), kvg_ref[...]).astype(BF16)
    ct = ct_ref[...]
    st = st_ref[...]

    qa = _dot(cqn, wqa_ref[...])
    qb = _dot(cqn, wqb_ref[...])
    kn = _dot(ckvn, wk_ref[...])
    v_ref[...] = _dot(ckvn, wv_ref[...]).astype(BF16)

    kr = kr_ref[...].astype(F32)
    k_rope = (kr * ct + pltpu.roll(kr, LANE // 2, axis=1) * st).astype(BF16)

    for h in range(MLA_HEADS):
        o = h * MLA_QK_PAD
        q_ref[:, o:o + LANE] = qa[:, o:o + LANE].astype(BF16)
        q_ref[:, o + LANE:o + 2 * LANE] = (
            qa[:, o + LANE:o + 2 * LANE] * ct + qb[:, h * LANE:(h + 1) * LANE] * st).astype(BF16)
        k_ref[:, o:o + LANE] = kn[:, h * LANE:(h + 1) * LANE].astype(BF16)
        k_ref[:, o + LANE:o + 2 * LANE] = k_rope


def mla_proj(z, q_norm_g, kv_norm_g, wqa, wqb, wk, wv, ct, st, *, seq, tm):
    T = z.shape[0]
    tm = min(tm, seq)
    n_s = seq // tm
    R = MLA_RANK
    cq_blk = (2 * 1024) // R
    kr_blk = (2 * 1024 + 2 * R) // LANE
    const = lambda shape: pl.BlockSpec(shape, lambda i: (0, 0))
    HQ = MLA_HEADS * MLA_QK_PAD
    HV = MLA_HEADS * MLA_V
    return pl.pallas_call(
        _mla_proj_kernel,
        grid=(T // tm,),
        in_specs=[
            pl.BlockSpec((tm, R), lambda i: (i, cq_blk)),
            pl.BlockSpec((tm, R), lambda i: (i, cq_blk + 1)),
            pl.BlockSpec((tm, LANE), lambda i: (i, kr_blk)),
            const((1, R)), const((1, R)),
            const((R, HQ)), const((R, MLA_HEADS * LANE)), const((R, HV)), const((R, HV)),
            pl.BlockSpec((tm, LANE), lambda i: (i % n_s, 0)),
            pl.BlockSpec((tm, LANE), lambda i: (i % n_s, 0)),
        ],
        out_specs=[
            pl.BlockSpec((tm, HQ), lambda i: (i, 0)),
            pl.BlockSpec((tm, HQ), lambda i: (i, 0)),
            pl.BlockSpec((tm, HV), lambda i: (i, 0)),
        ],
        out_shape=[
            jax.ShapeDtypeStruct((T, HQ), BF16),
            jax.ShapeDtypeStruct((T, HQ), BF16),
            jax.ShapeDtypeStruct((T, HV), BF16),
        ],
        compiler_params=_cparams("parallel"),
        name="mla_proj",
    )(z, z, z, q_norm_g.reshape(1, R), kv_norm_g.reshape(1, R), wqa, wqb, wk, wv, ct, st)


def _softmax_pv(s, v):
    m = jnp.max(s, axis=-1, keepdims=True)
    p = jnp.exp(s - m)
    l = jnp.sum(p, axis=-1, keepdims=True)
    return _dot(p.astype(BF16), v) / l


def _for_each_tile(i, n_tiles, fn):
    for c in range(n_tiles):
        pl.when(i == c)(functools.partial(fn, c))


def _mla_attn_kernel(q_ref, k_ref, v_ref, o_ref, *, tq, n_tiles):
    q = q_ref[0]
    qc = lax.broadcasted_iota(jnp.int32, (tq, tq), 0) // CHUNK
    kc = lax.broadcasted_iota(jnp.int32, (tq, tq), 1) // CHUNK
    visible = kc <= qc

    def tile(c):
        n = (c + 1) * tq
        s = lax.dot_general(q, k_ref[0, :n, :], NT_DIMS, preferred_element_type=F32)
        parts = [s[:, :c * tq]] if c else []
        parts.append(jnp.where(visible, s[:, c * tq:], NEG))
        s = jnp.concatenate(parts, axis=1) if c else parts[0]
        o_ref[0] = _softmax_pv(s, v_ref[0, :n, :]).astype(o_ref.dtype)

    _for_each_tile(pl.program_id(2), n_tiles, tile)


def mla_attention(q, k, v, *, tq):
    B, S, _ = q.shape
    tq = min(tq, S)
    return pl.pallas_call(
        functools.partial(_mla_attn_kernel, tq=tq, n_tiles=S // tq),
        grid=(B, MLA_HEADS, S // tq),
        in_specs=[
            pl.BlockSpec((1, tq, MLA_QK_PAD), lambda b, h, i: (b, i, h)),
            pl.BlockSpec((1, S, MLA_QK_PAD), lambda b, h, i: (b, 0, h)),
            pl.BlockSpec((1, S, MLA_V), lambda b, h, i: (b, 0, h)),
        ],
        out_specs=pl.BlockSpec((1, tq, MLA_V), lambda b, h, i: (b, i, h)),
        out_shape=jax.ShapeDtypeStruct((B, S, MLA_HEADS * MLA_V), BF16),
        compiler_params=_cparams("parallel", "parallel", "arbitrary"),
        name="mla_attention",
    )(q, k, v)


def _diff_attn_kernel(lam_ref, far_ref, q_ref, k_ref, v_ref, bias_ref, g_ref, o_ref, *, tq, n_tiles, out_scale):
    far = far_ref[pl.program_id(1)]
    lam = lam_ref[0]

    def tile(c):
        n = (c + 1) * tq
        v = v_ref[0, :n, :]

        def half(lo):
            s = lax.dot_general(q_ref[0, :, lo:lo + DIFF_HD], k_ref[0, :n, lo:lo + DIFF_HD], NT_DIMS,
                                preferred_element_type=F32)
            parts = []
            if c >= 2:
                parts.append(s[:, :(c - 1) * tq] + far)
            if c >= 1:
                parts.append(s[:, (c - 1) * tq:c * tq] + bias_ref[0, 0])
            parts.append(s[:, c * tq:] + bias_ref[0, 1])
            s = jnp.concatenate(parts, axis=1) if c else parts[0]
            return _softmax_pv(s, v)

        o = half(0) - lam * half(DIFF_HD)
        o_ref[0] = (_rms(o, g_ref[...]) * out_scale).astype(o_ref.dtype)

    _for_each_tile(pl.program_id(2), n_tiles, tile)


def diff_attention(qkv, lam, far_bias, bias_tiles, subln_g, *, tq, out_scale):
    B, S, _ = qkv.shape
    H = DIFF_HEADS
    dv = 2 * DIFF_HD
    smem = pl.BlockSpec(memory_space=pltpu.SMEM)
    return pl.pallas_call(
        functools.partial(_diff_attn_kernel, tq=tq, n_tiles=S // tq, out_scale=out_scale),
        grid=(B, H, S // tq),
        in_specs=[
            smem, smem,
            pl.BlockSpec((1, tq, dv), lambda b, h, i: (b, i, h)),
            pl.BlockSpec((1, S, dv), lambda b, h, i: (b, 0, H + h)),
            pl.BlockSpec((1, S, dv), lambda b, h, i: (b, 0, 2 * H + h)),
            pl.BlockSpec((1, 2, tq, tq), lambda b, h, i: (h, 0, 0, 0)),
            pl.BlockSpec((1, dv), lambda b, h, i: (0, 0)),
        ],
        out_specs=pl.BlockSpec((1, tq, dv), lambda b, h, i: (b, i, h)),
        out_shape=jax.ShapeDtypeStruct((B, S, H * dv), BF16),
        compiler_params=_cparams("parallel", "parallel", "arbitrary"),
        name="diff_attention",
    )(lam, far_bias, qkv, qkv, qkv, bias_tiles, subln_g.reshape(1, dv))


def _cross_kernel(h_ref, g_ref, wq_ref, kv_ref, wo_ref, o_ref):
    h = h_ref[0]
    hn = _rms(h, g_ref[...]).astype(BF16)
    q = _dot(hn, wq_ref[...]).astype(BF16)
    kv = kv_ref[0]
    HD = CROSS_HEADS * CROSS_HD
    outs = []
    for hd in range(CROSS_HEADS):
        lo = hd * CROSS_HD
        s = lax.dot_general(q[:, lo:lo + CROSS_HD], kv[:, lo:lo + CROSS_HD], NT_DIMS,
                            preferred_element_type=F32)
        m = jnp.max(s, axis=-1, keepdims=True)
        p = jnp.exp(s - m)
        l = jnp.sum(p, axis=-1, keepdims=True)
        o = _dot(p.astype(BF16), kv[:, HD + lo:HD + lo + CROSS_HD]) / l
        outs.append(o.astype(BF16))
    o_all = jnp.concatenate(outs, axis=-1)
    o_ref[0] = h + _dot(o_all, wo_ref[...])


def cross_attention(h, g, wq, kv, wo, *, kv_blk, tm):
    B, S, D = h.shape
    M = kv.shape[1]
    HD = CROSS_HEADS * CROSS_HD
    tm = min(tm, S)
    const = lambda shape: pl.BlockSpec(shape, lambda b, s: (0, 0))
    return pl.pallas_call(
        _cross_kernel,
        grid=(B, S // tm),
        in_specs=[
            pl.BlockSpec((1, tm, D), lambda b, s: (b, s, 0)),
            const((1, D)), const((D, HD)),
            pl.BlockSpec((1, M, 2 * HD), lambda b, s: (b, 0, kv_blk)),
            const((HD, D)),
        ],
        out_specs=pl.BlockSpec((1, tm, D), lambda b, s: (b, s, 0)),
        out_shape=jax.ShapeDtypeStruct((B, S, D), F32),
        compiler_params=_cparams("parallel", "arbitrary"),
        name="cross_attention",
    )(h, g.reshape(1, D), wq, kv, wo)


def _ffn_kernel(h_ref, g_ref, wg_ref, wu_ref, wd_ref, o_ref, xn_ref, acc_ref):
    f = pl.program_id(1)

    @pl.when(f == 0)
    def _():
        xn_ref[...] = _rms(h_ref[...], g_ref[...]).astype(BF16)
        acc_ref[...] = h_ref[...]

    x = xn_ref[...]
    gt = _dot(x, wg_ref[...])
    up = _dot(x, wu_ref[...])
    hm = (gt * jax.nn.sigmoid(gt) * up).astype(BF16)
    acc_ref[...] += _dot(hm, wd_ref[...])

    @pl.when(f == pl.num_programs(1) - 1)
    def _():
        o_ref[...] = acc_ref[...]


def dense_ffn(h, g, wg, wu, wd, *, tm, tf):
    T, D = h.shape
    F = wg.shape[1]
    tm = min(tm, T)
    return pl.pallas_call(
        _ffn_kernel,
        grid=(T // tm, F // tf),
        in_specs=[
            pl.BlockSpec((tm, D), lambda i, f: (i, 0)),
            pl.BlockSpec((1, D), lambda i, f: (0, 0)),
            pl.BlockSpec((D, tf), lambda i, f: (0, f)),
            pl.BlockSpec((D, tf), lambda i, f: (0, f)),
            pl.BlockSpec((tf, D), lambda i, f: (f, 0)),
        ],
        out_specs=pl.BlockSpec((tm, D), lambda i, f: (i, 0)),
        out_shape=jax.ShapeDtypeStruct((T, D), F32),
        scratch_shapes=[pltpu.VMEM((tm, D), BF16), pltpu.VMEM((tm, D), F32)],
        compiler_params=_cparams("parallel", "arbitrary"),
        name="dense_ffn",
    )(h, g.reshape(1, D), wg, wu, wd)


def _router_kernel(h_ref, g_ref, wr_ref, xn_ref, route_ref):
    xn = _rms(h_ref[...], g_ref[...])
    xn_ref[...] = xn
    logits = jnp.dot(xn, wr_ref[...], preferred_element_type=F32, precision=lax.Precision.HIGHEST)
    lane = lax.broadcasted_iota(jnp.int32, logits.shape, 1)
    logits = jnp.where(lane < N_EXPERTS, logits, -jnp.inf)
    v1 = jnp.max(logits, axis=-1, keepdims=True)
    i1 = jnp.min(jnp.where(logits == v1, lane, LANE), axis=-1, keepdims=True)
    rest = jnp.where(lane == i1, -jnp.inf, logits)
    v2 = jnp.max(rest, axis=-1, keepdims=True)
    i2 = jnp.min(jnp.where(rest == v2, lane, LANE), axis=-1, keepdims=True)
    e2 = jnp.exp(v2 - v1)
    g1 = 1.0 / (1.0 + e2)
    g2 = e2 / (1.0 + e2)
    route = jnp.where(lane == 0, i1.astype(F32), 0.0)
    route = jnp.where(lane == 1, i2.astype(F32), route)
    route = jnp.where(lane == 2, g1, route)
    route = jnp.where(lane == 3, g2, route)
    route_ref[...] = route


def moe_router(h, g, w_router, *, tm):
    T, D = h.shape
    tm = min(tm, T)
    wr = jnp.zeros((D, LANE), F32).at[:, :N_EXPERTS].set(w_router)
    return pl.pallas_call(
        _router_kernel,
        grid=(T // tm,),
        in_specs=[
            pl.BlockSpec((tm, D), lambda i: (i, 0)),
            pl.BlockSpec((1, D), lambda i: (0, 0)),
            pl.BlockSpec((D, LANE), lambda i: (0, 0)),
        ],
        out_specs=[pl.BlockSpec((tm, D), lambda i: (i, 0)), pl.BlockSpec((tm, LANE), lambda i: (i, 0))],
        out_shape=[jax.ShapeDtypeStruct((T, D), F32), jax.ShapeDtypeStruct((T, LANE), F32)],
        compiler_params=_cparams("parallel"),
        name="moe_router",
    )(h, g.reshape(1, D), wr)


def _moe_kernel(tok_ref, be_ref, nu_ref, x_hbm, wg_ref, wu_ref, wd_ref, o_ref, xbuf, xb, acc, sem, *, tm):
    i = pl.program_id(0)
    f = pl.program_id(1)
    nf = pl.num_programs(1)
    nused = nu_ref[0]

    def row_copy(blk, slot, r):
        tok = tok_ref[blk * tm + r]
        return pltpu.make_async_copy(x_hbm.at[pl.ds(tok, 1)], xbuf.at[slot, pl.ds(r, 1)], sem.at[slot])

    def issue(blk, slot):
        def body(r, c):
            row_copy(blk, slot, r).start()
            return c
        lax.fori_loop(0, tm, body, 0)

    def wait(blk, slot):
        def body(r, c):
            row_copy(blk, slot, r).wait()
            return c
        lax.fori_loop(0, tm, body, 0)

    @pl.when(i < nused)
    def _():
        @pl.when(f == 0)
        def _():
            slot = i % 2

            @pl.when(i == 0)
            def _():
                issue(0, 0)

            wait(i, slot)

            @pl.when(i + 1 < nused)
            def _():
                issue(i + 1, 1 - slot)

            xb[...] = xbuf[slot].astype(BF16)
            acc[...] = jnp.zeros(acc.shape, F32)

        x = xb[...]
        gt = _dot(x, wg_ref[0])
        up = _dot(x, wu_ref[0])
        hm = (gt * jax.nn.sigmoid(gt) * up).astype(BF16)
        acc[...] += _dot(hm, wd_ref[0])

        @pl.when(f == nf - 1)
        def _():
            o_ref[...] = acc[...]

    @pl.when((i >= nused) & (f == nf - 1))
    def _():
        o_ref[...] = jnp.zeros(o_ref.shape, F32)


def moe_experts(xn, slot_tok, block_e, nused, wg, wu, wd, *, tm, tf):
    T, D = xn.shape
    P = slot_tok.shape[0]
    F = wg.shape[2]
    nf = F // tf

    def w_cols(i, f, tok, be, nu):
        return be[jnp.minimum(i, nu[0] - 1)], 0, jnp.where(i < nu[0], f, nf - 1)

    def w_rows(i, f, tok, be, nu):
        return be[jnp.minimum(i, nu[0] - 1)], jnp.where(i < nu[0], f, nf - 1), 0

    return pl.pallas_call(
        functools.partial(_moe_kernel, tm=tm),
        grid_spec=pltpu.PrefetchScalarGridSpec(
            num_scalar_prefetch=3,
            grid=(P // tm, nf),
            in_specs=[
                pl.BlockSpec(memory_space=pl.ANY),
                pl.BlockSpec((1, D, tf), w_cols),
                pl.BlockSpec((1, D, tf), w_cols),
                pl.BlockSpec((1, tf, D), w_rows),
            ],
            out_specs=pl.BlockSpec((tm, D), lambda i, f, tok, be, nu: (i, 0)),
            scratch_shapes=[
                pltpu.VMEM((2, tm, D), F32),
                pltpu.VMEM((tm, D), BF16),
                pltpu.VMEM((tm, D), F32),
                pltpu.SemaphoreType.DMA((2,)),
            ],
        ),
        out_shape=jax.ShapeDtypeStruct((P, D), F32),
        compiler_params=_cparams("arbitrary", "arbitrary"),
        name="moe_experts",
    )(slot_tok, block_e, nused, xn, wg, wu, wd)


def _combine_kernel(pos_ref, h_ref, route_ref, g_ref, ys_hbm, o_ref, buf, sem, *, tm):
    i = pl.program_id(0)

    def row_copy(r, k):
        p = pos_ref[2 * (i * tm + r) + k]
        return pltpu.make_async_copy(ys_hbm.at[pl.ds(p, 1)], buf.at[k, pl.ds(r, 1)], sem.at[k])

    def issue(r, c):
        row_copy(r, 0).start()
        row_copy(r, 1).start()
        return c

    def wait(r, c):
        row_copy(r, 0).wait()
        row_copy(r, 1).wait()
        return c

    lax.fori_loop(0, tm, issue, 0)
    lax.fori_loop(0, tm, wait, 0)
    route = route_ref[...]
    y = route[:, 2:3] * buf[0] + route[:, 3:4] * buf[1]
    o_ref[...] = _rms(h_ref[...] + y, g_ref[...])


def moe_combine_norm(h, route, ys, pos, g, *, tm):
    T, D = h.shape
    tm = min(tm, T)
    return pl.pallas_call(
        functools.partial(_combine_kernel, tm=tm),
        grid_spec=pltpu.PrefetchScalarGridSpec(
            num_scalar_prefetch=1,
            grid=(T // tm,),
            in_specs=[
                pl.BlockSpec((tm, D), lambda i, pos: (i, 0)),
                pl.BlockSpec((tm, LANE), lambda i, pos: (i, 0)),
                pl.BlockSpec((1, D), lambda i, pos: (0, 0)),
                pl.BlockSpec(memory_space=pl.ANY),
            ],
            out_specs=pl.BlockSpec((tm, D), lambda i, pos: (i, 0)),
            scratch_shapes=[pltpu.VMEM((2, tm, D), F32), pltpu.SemaphoreType.DMA((2,))],
        ),
        out_shape=jax.ShapeDtypeStruct((T, D), F32),
        compiler_params=_cparams("arbitrary"),
        name="moe_combine_norm",
    )(pos, h, route, g.reshape(1, D), ys)


def _dispatch(route, tm):
    T = route.shape[0]
    A = 2 * T
    P = A + N_EXPERTS * tm
    flat_e = route[:, :2].astype(jnp.int32).reshape(A)
    onehot = (flat_e[:, None] == jnp.arange(N_EXPERTS, dtype=jnp.int32)[None, :]).astype(jnp.int32)
    csum = jnp.cumsum(onehot, axis=0)
    counts = csum[-1]
    rank = jnp.sum(csum * onehot, axis=1) - 1
    padded = ((counts + tm - 1) // tm) * tm
    pend = jnp.cumsum(padded)
    pstart = pend - padded
    dest = (jnp.sum(onehot * pstart[None, :], axis=1) + rank).astype(jnp.int32)
    slot_tok = jnp.zeros((P,), jnp.int32).at[dest].set(jnp.arange(A, dtype=jnp.int32) // 2)
    nblk = P // tm
    block_e = jnp.minimum(
        jnp.searchsorted(pend, jnp.arange(nblk, dtype=jnp.int32) * tm, side="right"),
        N_EXPERTS - 1).astype(jnp.int32)
    nused = (pend[-1] // tm).astype(jnp.int32).reshape(1)
    return slot_tok, dest, block_e, nused


def _rope_slabs(seq):
    pos = jnp.arange(seq, dtype=F32)
    inv = jnp.power(ROPE_THETA, -jnp.arange(0, MLA_ROPE, 2, dtype=F32) / MLA_ROPE)
    ang = pos[:, None] * inv[None, :]
    z = jnp.zeros((seq, LANE - MLA_ROPE), F32)
    ct = jnp.concatenate([jnp.cos(ang), jnp.cos(ang), z], axis=1)
    st = jnp.concatenate([jnp.sin(ang), jnp.sin(ang), z], axis=1)
    return ct, st


def _t5_bucket(rel):
    half = REL_BUCKETS // 2
    max_exact = half // 2
    ret = (rel > 0).astype(jnp.int32) * half
    n = jnp.abs(rel)
    nf = jnp.maximum(n, 1).astype(F32)
    large = max_exact + (jnp.log(nf / max_exact) / math.log(REL_MAX_DIST / max_exact)
                         * (half - max_exact)).astype(jnp.int32)
    large = jnp.minimum(large, half - 1)
    return ret + jnp.where(n < max_exact, n, large)


def _bias_tiles(rel_bias, tq):
    assert tq >= REL_MAX_DIST
    qi = jnp.arange(tq, dtype=jnp.int32)[:, None]
    ki = jnp.arange(tq, dtype=jnp.int32)[None, :]

    def lookup(rel):
        bucket = _t5_bucket(rel)[None]
        out = jnp.zeros((rel_bias.shape[1],) + rel.shape, F32)
        for b in range(REL_BUCKETS):
            out = jnp.where(bucket == b, rel_bias[b][:, None, None], out)
        return out

    prev = lookup(ki - qi - tq)
    diag = jnp.where(((ki // CHUNK) <= (qi // CHUNK))[None], lookup(ki - qi), NEG)
    far = lookup(jnp.full((1, 1), -2 * tq, jnp.int32))[:, 0, 0]
    return jnp.stack([prev, diag], axis=1), far


def kernel(x, mem, rel_bias, mem_norm_g, norm_mix_g, norm_cross_g, norm_ffn_g, cross_wq, cross_wkv, cross_wo, ev_w_in, ev_conv_w, ev_conv_b, ev_ln_g, ev_ln_b, ev_q_norm_g, ev_w_uq, ev_kv_norm_g, ev_w_ukv, ev_w_out, ev_ffn_wg, ev_ffn_wu, ev_ffn_wd, od_w_in, od_lambda_q1, od_lambda_k1, od_lambda_q2, od_lambda_k2, od_subln_g, od_w_out, od_router, od_moe_wg, od_moe_wu, od_moe_wd, final_norm_g):
    B, S, D = x.shape
    T = B * S
    M = mem.shape[1]
    AW = ev_conv_w.shape[2]
    H = MLA_HEADS
    R = MLA_RANK
    h = x.reshape(T, D)

    w_in0 = ev_w_in[0]
    kr0 = 2 * AW + 2 * R
    half = MLA_ROPE // 2
    w_in0 = jnp.concatenate(
        [w_in0, -w_in0[:, kr0 + half:kr0 + MLA_ROPE], w_in0[:, kr0:kr0 + half]], axis=1).astype(BF16)

    q_scale = (MLA_NOPE + MLA_ROPE) ** -0.5
    wuq = (ev_w_uq[0] * q_scale).reshape(R, H, MLA_NOPE + MLA_ROPE)
    w_nope, w_r1, w_r2 = wuq[..., :MLA_NOPE], wuq[..., MLA_NOPE:MLA_NOPE + half], wuq[..., MLA_NOPE + half:]
    zq = jnp.zeros((R, H, MLA_QK_PAD - MLA_NOPE - MLA_ROPE), F32)
    wqa = jnp.concatenate([w_nope, w_r1, w_r2, zq], axis=-1).reshape(R, H * MLA_QK_PAD).astype(BF16)
    wqb = jnp.concatenate([-w_r2, w_r1, zq], axis=-1).reshape(R, H * LANE).astype(BF16)
    wukv = ev_w_ukv[0].reshape(R, H, MLA_NOPE + MLA_V)
    wk = wukv[..., :MLA_NOPE].reshape(R, H * MLA_NOPE).astype(BF16)
    wv = wukv[..., MLA_NOPE:].reshape(R, H * MLA_V).astype(BF16)
    ct, st = _rope_slabs(S)

    w_out0 = ev_w_out[0].astype(BF16)
    c_scale = CROSS_HD ** -0.5
    wq_c = (cross_wq * c_scale).astype(BF16)
    wkv_c = jnp.concatenate([cross_wkv[0], cross_wkv[1]], axis=1).astype(BF16)
    wo_c = cross_wo.astype(BF16)

    d_scale = DIFF_HD ** -0.5
    w_in1 = jnp.concatenate([od_w_in[0][:, :D] * d_scale, od_w_in[0][:, D:]], axis=1).astype(BF16)
    layer = 1
    lambda_init = 0.8 - 0.6 * math.exp(-0.3 * layer)
    lam = (jnp.exp(jnp.sum(od_lambda_q1[0] * od_lambda_k1[0]))
           - jnp.exp(jnp.sum(od_lambda_q2[0] * od_lambda_k2[0])) + lambda_init).reshape(1).astype(F32)
    tq = min(256, S)
    bias_tiles, far_bias = _bias_tiles(rel_bias, tq)

    kv_mem = norm_matmul(mem.reshape(B * M, D), mem_norm_g, wkv_c, tm=512, tn=512)
    kv_mem = kv_mem.reshape(B, M, -1)

    z = norm_matmul(h, norm_mix_g[0], w_in0, tm=512, tn=640)
    a_out = conformer_conv(z.reshape(B, S, -1), ev_conv_w[0], ev_conv_b[0], ev_ln_g[0], ev_ln_b[0], ts=256)
    q, k, v = mla_proj(z, ev_q_norm_g[0], ev_kv_norm_g[0], wqa, wqb, wk, wv, ct, st, seq=S, tm=512)
    b_out = mla_attention(q.reshape(B, S, -1), k.reshape(B, S, -1), v.reshape(B, S, -1), tq=tq)
    h = matmul_res(h, [a_out.reshape(T, AW), b_out.reshape(T, H * MLA_V)], [w_out0[:AW], w_out0[AW:]],
                   tm=512, tn=512)
    h = cross_attention(h.reshape(B, S, D), norm_cross_g[0], wq_c[0], kv_mem, wo_c[0], kv_blk=0, tm=512)
    h = dense_ffn(h.reshape(T, D), norm_ffn_g[0], ev_ffn_wg[0].astype(BF16), ev_ffn_wu[0].astype(BF16),
                  ev_ffn_wd[0].astype(BF16), tm=512, tf=512)

    qkv = norm_matmul(h, norm_mix_g[1], w_in1, tm=512, tn=512)
    o = diff_attention(qkv.reshape(B, S, -1), lam, far_bias, bias_tiles, od_subln_g[0], tq=tq,
                       out_scale=1.0 - lambda_init)
    h = matmul_res(h, [o.reshape(T, D)], [od_w_out[0].astype(BF16)], tm=512, tn=512)
    h = cross_attention(h.reshape(B, S, D), norm_cross_g[1], wq_c[1], kv_mem, wo_c[1], kv_blk=1, tm=512)
    h = h.reshape(T, D)

    tm_moe = 512
    xn, route = moe_router(h, norm_ffn_g[1], od_router[0], tm=512)
    slot_tok, dest, block_e, nused = _dispatch(route, tm_moe)
    ys = moe_experts(xn, slot_tok, block_e, nused, od_moe_wg[0].astype(BF16), od_moe_wu[0].astype(BF16),
                     od_moe_wd[0].astype(BF16), tm=tm_moe, tf=512)
    out = moe_combine_norm(h, route, ys, dest, final_norm_g, tm=256)
    return out.reshape(B, S, D)
```

```python
import functools
import math

import jax
import jax.numpy as jnp
from jax import lax
from jax.experimental import pallas as pl
from jax.experimental.pallas import tpu as pltpu

F32 = jnp.float32
BF16 = jnp.bfloat16

EPS = 1e-6
NEG = -1e30
CHUNK = 64

CONV_WIDTH = 31
CONV_HALO = 32
MLA_HEADS = 8
MLA_NOPE = 128
MLA_ROPE = 64
MLA_V = 128
MLA_RANK = 512
MLA_QK_PAD = 256
ROPE_THETA = 10000.0
DIFF_HEADS = 8
DIFF_HD = 128
REL_BUCKETS = 32
REL_MAX_DIST = 128
CROSS_HEADS = 4
CROSS_HD = 128
N_EXPERTS = 8
LANE = 128

VMEM_LIMIT = 56 * 1024 * 1024

NT_DIMS = (((1,), (1,)), ((), ()))


def _cparams(*sem):
    return pltpu.CompilerParams(dimension_semantics=sem, vmem_limit_bytes=VMEM_LIMIT)


def _rms(x, g):
    return x * lax.rsqrt(jnp.mean(x * x, axis=-1, keepdims=True) + EPS) * g


def _dot(a, b):
    return jnp.dot(a, b, preferred_element_type=F32)


def _norm_matmul_kernel(x_ref, g_ref, w_ref, o_ref, xn_ref, *, scaled_tiles, scale):
    j = pl.program_id(1)

    @pl.when(j == 0)
    def _():
        xn_ref[...] = _rms(x_ref[...], g_ref[...]).astype(BF16)

    acc = _dot(xn_ref[...], w_ref[...].astype(BF16))
    if scaled_tiles:
        acc = acc * jnp.where(j < scaled_tiles, scale, 1.0)
    o_ref[...] = acc.astype(o_ref.dtype)


def norm_matmul(x, g, w, *, tm, tn, scaled_cols=0, scale=1.0):
    M, K = x.shape
    N = w.shape[1]
    tm = min(tm, M)
    assert scaled_cols % tn == 0
    return pl.pallas_call(
        functools.partial(_norm_matmul_kernel, scaled_tiles=scaled_cols // tn, scale=scale),
        grid=(M // tm, N // tn),
        in_specs=[
            pl.BlockSpec((tm, K), lambda i, j: (i, 0)),
            pl.BlockSpec((1, K), lambda i, j: (0, 0)),
            pl.BlockSpec((K, tn), lambda i, j: (0, j)),
        ],
        out_specs=pl.BlockSpec((tm, tn), lambda i, j: (i, j)),
        out_shape=jax.ShapeDtypeStruct((M, N), BF16),
        scratch_shapes=[pltpu.VMEM((tm, K), BF16)],
        compiler_params=_cparams("parallel", "arbitrary"),
        name="norm_matmul",
    )(x, g.reshape(1, K), w)


def _matmul_res_kernel(*refs, n):
    res_ref = refs[0]
    o_ref = refs[1 + 2 * n]
    acc = res_ref[...]
    for k in range(n):
        acc = acc + _dot(refs[1 + k][...], refs[1 + n + k][...].astype(BF16))
    o_ref[...] = acc


def matmul_res(res, a_list, w, *, tm, tn):
    M, N = res.shape
    tm = min(tm, M)
    n = len(a_list)
    ka = a_list[0].shape[1]
    assert all(a.shape[1] == ka for a in a_list) and w.shape[0] == n * ka
    in_specs = [pl.BlockSpec((tm, tn), lambda i, j: (i, j))]
    in_specs += [pl.BlockSpec((tm, ka), lambda i, j: (i, 0)) for _ in a_list]
    in_specs += [pl.BlockSpec((ka, tn), lambda i, j, k=k: (k, j)) for k in range(n)]
    return pl.pallas_call(
        functools.partial(_matmul_res_kernel, n=n),
        grid=(M // tm, N // tn),
        in_specs=in_specs,
        out_specs=pl.BlockSpec((tm, tn), lambda i, j: (i, j)),
        out_shape=jax.ShapeDtypeStruct((M, N), F32),
        compiler_params=_cparams("parallel", "arbitrary"),
        name="matmul_res",
    )(res, *a_list, *([w] * n))


def _conv_kernel(val_ref, gate_ref, w_ref, b_ref, lg_ref, lb_ref, o_ref, ubuf, *, ts):
    s = pl.program_id(1)

    @pl.when(s == 0)
    def _():
        ubuf[0:CONV_HALO, :] = jnp.zeros((CONV_HALO, ubuf.shape[1]), F32)

    @pl.when(s > 0)
    def _():
        ubuf[0:CONV_HALO, :] = ubuf[ts:ts + CONV_HALO, :]

    val = val_ref[0].astype(F32)
    gate = gate_ref[0].astype(F32)
    ubuf[CONV_HALO:CONV_HALO + ts, :] = val * jax.nn.sigmoid(gate)

    base = CONV_HALO - (CONV_WIDTH - 1)
    acc = jnp.zeros((ts, ubuf.shape[1]), F32) + b_ref[...]
    for j in range(CONV_WIDTH):
        acc = acc + w_ref[j:j + 1, :] * ubuf[base + j:base + j + ts, :]

    mu = jnp.mean(acc, axis=-1, keepdims=True)
    xc = acc - mu
    y = xc * lax.rsqrt(jnp.mean(xc * xc, axis=-1, keepdims=True) + EPS)
    y = y * lg_ref[...] + lb_ref[...]
    o_ref[0] = (y * jax.nn.sigmoid(y)).astype(o_ref.dtype)


def conformer_conv(z, conv_w, conv_b, ln_g, ln_b, *, ts):
    B, S, _ = z.shape
    C = conv_w.shape[1]
    ts = min(ts, S)
    wpad = jnp.zeros((CONV_HALO, C), F32).at[:CONV_WIDTH].set(conv_w)
    row = lambda v: v.reshape(1, C)
    const = lambda shape: pl.BlockSpec(shape, lambda b, s: (0, 0))
    return pl.pallas_call(
        functools.partial(_conv_kernel, ts=ts),
        grid=(B, S // ts),
        in_specs=[
            pl.BlockSpec((1, ts, C), lambda b, s: (b, s, 0)),
            pl.BlockSpec((1, ts, C), lambda b, s: (b, s, 1)),
            const((CONV_HALO, C)), const((1, C)), const((1, C)), const((1, C)),
        ],
        out_specs=pl.BlockSpec((1, ts, C), lambda b, s: (b, s, 0)),
        out_shape=jax.ShapeDtypeStruct((B, S, C), BF16),
        scratch_shapes=[pltpu.VMEM((ts + CONV_HALO, C), F32)],
        compiler_params=_cparams("parallel", "arbitrary"),
        name="conformer_conv",
    )(z, z, wpad, row(conv_b), row(ln_g), row(ln_b))


def _mla_proj_kernel(cq_ref, ckv_ref, kr_ref, qg_ref, kvg_ref, wqa_ref, wqb_ref, wk_ref, wv_ref,
                     ct_ref, st_ref, q_ref, k_ref, v_ref):
    cqn = _rms(cq_ref[...].astype(F32), qg_ref[...]).astype(BF16)
    ckvn = _rms(ckv_ref[...].astype(F32), kvg_ref[...]).astype(BF16)
    ct = ct_ref[...]
    st = st_ref[...]

    qa = _dot(cqn, wqa_ref[...])
    qb = _dot(cqn, wqb_ref[...])
    kn = _dot(ckvn, wk_ref[...])
    v_ref[...] = _dot(ckvn, wv_ref[...]).astype(BF16)

    kr = kr_ref[...].astype(F32)
    k_rope = (kr * ct + pltpu.roll(kr, LANE // 2, axis=1) * st).astype(BF16)

    for h in range(MLA_HEADS):
        o = h * MLA_QK_PAD
        q_ref[:, o:o + LANE] = qa[:, o:o + LANE].astype(BF16)
        q_ref[:, o + LANE:o + 2 * LANE] = (
            qa[:, o + LANE:o + 2 * LANE] * ct + qb[:, h * LANE:(h + 1) * LANE] * st).astype(BF16)
        k_ref[:, o:o + LANE] = kn[:, h * LANE:(h + 1) * LANE].astype(BF16)
        k_ref[:, o + LANE:o + 2 * LANE] = k_rope


def mla_proj(z, q_norm_g, kv_norm_g, wqa, wqb, wk, wv, ct, st, *, seq, tm):
    T = z.shape[0]
    tm = min(tm, seq)
    n_s = seq // tm
    R = MLA_RANK
    cq_blk = (2 * 1024) // R
    kr_blk = (2 * 1024 + 2 * R) // LANE
    const = lambda shape: pl.BlockSpec(shape, lambda i: (0, 0))
    HQ = MLA_HEADS * MLA_QK_PAD
    HV = MLA_HEADS * MLA_V
    return pl.pallas_call(
        _mla_proj_kernel,
        grid=(T // tm,),
        in_specs=[
            pl.BlockSpec((tm, R), lambda i: (i, cq_blk)),
            pl.BlockSpec((tm, R), lambda i: (i, cq_blk + 1)),
            pl.BlockSpec((tm, LANE), lambda i: (i, kr_blk)),
            const((1, R)), const((1, R)),
            const((R, HQ)), const((R, MLA_HEADS * LANE)), const((R, HV)), const((R, HV)),
            pl.BlockSpec((tm, LANE), lambda i: (i % n_s, 0)),
            pl.BlockSpec((tm, LANE), lambda i: (i % n_s, 0)),
        ],
        out_specs=[
            pl.BlockSpec((tm, HQ), lambda i: (i, 0)),
            pl.BlockSpec((tm, HQ), lambda i: (i, 0)),
            pl.BlockSpec((tm, HV), lambda i: (i, 0)),
        ],
        out_shape=[
            jax.ShapeDtypeStruct((T, HQ), BF16),
            jax.ShapeDtypeStruct((T, HQ), BF16),
            jax.ShapeDtypeStruct((T, HV), BF16),
        ],
        compiler_params=_cparams("parallel"),
        name="mla_proj",
    )(z, z, z, q_norm_g.reshape(1, R), kv_norm_g.reshape(1, R), wqa, wqb, wk, wv, ct, st)


def _softmax_pv(s, v):
    m = jnp.max(s, axis=-1, keepdims=True)
    p = jnp.exp(s - m)
    l = jnp.sum(p, axis=-1, keepdims=True)
    return _dot(p.astype(BF16), v) / l


def _for_each_tile(i, n_tiles, fn):
    for c in range(n_tiles):
        pl.when(i == c)(functools.partial(fn, c))


def _mla_attn_kernel(q_ref, k_ref, v_ref, o_ref, *, tq, n_tiles):
    q = q_ref[0]
    qc = lax.broadcasted_iota(jnp.int32, (tq, tq), 0) // CHUNK
    kc = lax.broadcasted_iota(jnp.int32, (tq, tq), 1) // CHUNK
    visible = kc <= qc

    def tile(c):
        n = (c + 1) * tq
        s = lax.dot_general(q, k_ref[0, :n, :], NT_DIMS, preferred_element_type=F32)
        parts = [s[:, :c * tq]] if c else []
        parts.append(jnp.where(visible, s[:, c * tq:], NEG))
        s = jnp.concatenate(parts, axis=1) if c else parts[0]
        o_ref[0] = _softmax_pv(s, v_ref[0, :n, :]).astype(o_ref.dtype)

    _for_each_tile(pl.program_id(2), n_tiles, tile)


def mla_attention(q, k, v, *, tq):
    B, S, _ = q.shape
    tq = min(tq, S)
    return pl.pallas_call(
        functools.partial(_mla_attn_kernel, tq=tq, n_tiles=S // tq),
        grid=(B, MLA_HEADS, S // tq),
        in_specs=[
            pl.BlockSpec((1, tq, MLA_QK_PAD), lambda b, h, i: (b, i, h)),
            pl.BlockSpec((1, S, MLA_QK_PAD), lambda b, h, i: (b, 0, h)),
            pl.BlockSpec((1, S, MLA_V), lambda b, h, i: (b, 0, h)),
        ],
        out_specs=pl.BlockSpec((1, tq, MLA_V), lambda b, h, i: (b, i, h)),
        out_shape=jax.ShapeDtypeStruct((B, S, MLA_HEADS * MLA_V), BF16),
        compiler_params=_cparams("parallel", "parallel", "arbitrary"),
        name="mla_attention",
    )(q, k, v)


def _diff_attn_kernel(lam_ref, far_ref, q_ref, k_ref, v_ref, bias_ref, g_ref, o_ref, *, tq, n_tiles, out_scale):
    far = far_ref[pl.program_id(1)]
    lam = lam_ref[0]

    def tile(c):
        n = (c + 1) * tq
        v = v_ref[0, :n, :]

        def half(lo):
            s = lax.dot_general(q_ref[0, :, lo:lo + DIFF_HD], k_ref[0, :n, lo:lo + DIFF_HD], NT_DIMS,
                                preferred_element_type=F32)
            parts = []
            if c >= 2:
                parts.append(s[:, :(c - 1) * tq] + far)
            if c >= 1:
                parts.append(s[:, (c - 1) * tq:c * tq] + bias_ref[0, 0])
            parts.append(s[:, c * tq:] + bias_ref[0, 1])
            s = jnp.concatenate(parts, axis=1) if c else parts[0]
            return _softmax_pv(s, v)

        o = half(0) - lam * half(DIFF_HD)
        o_ref[0] = (_rms(o, g_ref[...]) * out_scale).astype(o_ref.dtype)

    _for_each_tile(pl.program_id(2), n_tiles, tile)


def diff_attention(qkv, lam, far_bias, bias_tiles, subln_g, *, tq, out_scale):
    B, S, _ = qkv.shape
    H = DIFF_HEADS
    dv = 2 * DIFF_HD
    smem = pl.BlockSpec(memory_space=pltpu.SMEM)
    return pl.pallas_call(
        functools.partial(_diff_attn_kernel, tq=tq, n_tiles=S // tq, out_scale=out_scale),
        grid=(B, H, S // tq),
        in_specs=[
            smem, smem,
            pl.BlockSpec((1, tq, dv), lambda b, h, i: (b, i, h)),
            pl.BlockSpec((1, S, dv), lambda b, h, i: (b, 0, H + h)),
            pl.BlockSpec((1, S, dv), lambda b, h, i: (b, 0, 2 * H + h)),
            pl.BlockSpec((1, 2, tq, tq), lambda b, h, i: (h, 0, 0, 0)),
            pl.BlockSpec((1, dv), lambda b, h, i: (0, 0)),
        ],
        out_specs=pl.BlockSpec((1, tq, dv), lambda b, h, i: (b, i, h)),
        out_shape=jax.ShapeDtypeStruct((B, S, H * dv), BF16),
        compiler_params=_cparams("parallel", "parallel", "arbitrary"),
        name="diff_attention",
    )(lam, far_bias, qkv, qkv, qkv, bias_tiles, subln_g.reshape(1, dv))


def _cross_kernel(h_ref, g_ref, wq_ref, kv_ref, wo_ref, o_ref):
    h = h_ref[0]
    hn = _rms(h, g_ref[...]).astype(BF16)
    q = _dot(hn, wq_ref[...]).astype(BF16)
    kv = kv_ref[0]
    HD = CROSS_HEADS * CROSS_HD
    outs = []
    for hd in range(CROSS_HEADS):
        lo = hd * CROSS_HD
        s = lax.dot_general(q[:, lo:lo + CROSS_HD], kv[:, lo:lo + CROSS_HD], NT_DIMS,
                            preferred_element_type=F32)
        m = jnp.max(s, axis=-1, keepdims=True)
        p = jnp.exp(s - m)
        l = jnp.sum(p, axis=-1, keepdims=True)
        o = _dot(p.astype(BF16), kv[:, HD + lo:HD + lo + CROSS_HD]) / l
        outs.append(o.astype(BF16))
    o_all = jnp.concatenate(outs, axis=-1)
    o_ref[0] = h + _dot(o_all, wo_ref[...])


def cross_attention(h, g, wq, kv, wo, *, kv_blk, tm):
    B, S, D = h.shape
    M = kv.shape[1]
    HD = CROSS_HEADS * CROSS_HD
    tm = min(tm, S)
    const = lambda shape: pl.BlockSpec(shape, lambda b, s: (0, 0))
    return pl.pallas_call(
        _cross_kernel,
        grid=(B, S // tm),
        in_specs=[
            pl.BlockSpec((1, tm, D), lambda b, s: (b, s, 0)),
            const((1, D)), const((D, HD)),
            pl.BlockSpec((1, M, 2 * HD), lambda b, s: (b, 0, kv_blk)),
            const((HD, D)),
        ],
        out_specs=pl.BlockSpec((1, tm, D), lambda b, s: (b, s, 0)),
        out_shape=jax.ShapeDtypeStruct((B, S, D), F32),
        compiler_params=_cparams("parallel", "arbitrary"),
        name="cross_attention",
    )(h, g.reshape(1, D), wq, kv, wo)


def _ffn_kernel(h_ref, g_ref, wg_ref, wu_ref, wd_ref, o_ref, xn_ref):
    @pl.when(pl.program_id(1) == 0)
    def _():
        xn_ref[...] = _rms(h_ref[...], g_ref[...]).astype(BF16)
        o_ref[...] = h_ref[...]

    x = xn_ref[...]
    gt = _dot(x, wg_ref[...].astype(BF16))
    up = _dot(x, wu_ref[...].astype(BF16))
    hm = (gt * jax.nn.sigmoid(gt) * up).astype(BF16)
    o_ref[...] += _dot(hm, wd_ref[...].astype(BF16))


def dense_ffn(h, g, wg, wu, wd, *, tm, tf):
    T, D = h.shape
    F = wg.shape[1]
    tm = min(tm, T)
    return pl.pallas_call(
        _ffn_kernel,
        grid=(T // tm, F // tf),
        in_specs=[
            pl.BlockSpec((tm, D), lambda i, f: (i, 0), pipeline_mode=pl.Buffered(1)),
            pl.BlockSpec((1, D), lambda i, f: (0, 0)),
            pl.BlockSpec((D, tf), lambda i, f: (0, f)),
            pl.BlockSpec((D, tf), lambda i, f: (0, f)),
            pl.BlockSpec((tf, D), lambda i, f: (f, 0)),
        ],
        out_specs=pl.BlockSpec((tm, D), lambda i, f: (i, 0)),
        out_shape=jax.ShapeDtypeStruct((T, D), F32),
        scratch_shapes=[pltpu.VMEM((tm, D), BF16)],
        compiler_params=_cparams("parallel", "arbitrary"),
        name="dense_ffn",
    )(h, g.reshape(1, D), wg, wu, wd)


def _router_kernel(h_ref, g_ref, wr_ref, xn_ref, route_ref):
    xn = _rms(h_ref[...], g_ref[...])
    xn_ref[...] = xn
    logits = jnp.dot(xn, wr_ref[...], preferred_element_type=F32, precision=lax.Precision.HIGHEST)
    lane = lax.broadcasted_iota(jnp.int32, logits.shape, 1)
    logits = jnp.where(lane < N_EXPERTS, logits, -jnp.inf)
    v1 = jnp.max(logits, axis=-1, keepdims=True)
    i1 = jnp.min(jnp.where(logits == v1, lane, LANE), axis=-1, keepdims=True)
    rest = jnp.where(lane == i1, -jnp.inf, logits)
    v2 = jnp.max(rest, axis=-1, keepdims=True)
    i2 = jnp.min(jnp.where(rest == v2, lane, LANE), axis=-1, keepdims=True)
    e2 = jnp.exp(v2 - v1)
    g1 = 1.0 / (1.0 + e2)
    g2 = e2 / (1.0 + e2)
    route = jnp.where(lane == 0, i1.astype(F32), 0.0)
    route = jnp.where(lane == 1, i2.astype(F32), route)
    route = jnp.where(lane == 2, g1, route)
    route = jnp.where(lane == 3, g2, route)
    route_ref[...] = route


def moe_router(h, g, w_router, *, tm):
    T, D = h.shape
    tm = min(tm, T)
    wr = jnp.zeros((D, LANE), F32).at[:, :N_EXPERTS].set(w_router)
    return pl.pallas_call(
        _router_kernel,
        grid=(T // tm,),
        in_specs=[
            pl.BlockSpec((tm, D), lambda i: (i, 0)),
            pl.BlockSpec((1, D), lambda i: (0, 0)),
            pl.BlockSpec((D, LANE), lambda i: (0, 0)),
        ],
        out_specs=[pl.BlockSpec((tm, D), lambda i: (i, 0)), pl.BlockSpec((tm, LANE), lambda i: (i, 0))],
        out_shape=[jax.ShapeDtypeStruct((T, D), F32), jax.ShapeDtypeStruct((T, LANE), F32)],
        compiler_params=_cparams("parallel"),
        name="moe_router",
    )(h, g.reshape(1, D), wr)


def _moe_kernel(tok_ref, be_ref, ns_ref, nu_ref, x_hbm, wg_ref, wu_ref, wd_ref, o_ref, xbuf, xb, sem, *, tm, sub):
    i = pl.program_id(0)
    f = pl.program_id(1)
    nused = nu_ref[0]

    def row_copy(blk, r):
        tok = tok_ref[blk * tm + r]
        return pltpu.make_async_copy(x_hbm.at[pl.ds(tok, 1)], xbuf.at[pl.ds(r, 1)], sem.at[0])

    def issue(blk):
        def body(r, c):
            row_copy(blk, r).start()
            return c
        lax.fori_loop(0, tm, body, 0, unroll=8)

    def wait(blk):
        def body(r, c):
            row_copy(blk, r).wait()
            return c
        lax.fori_loop(0, tm, body, 0, unroll=8)

    @pl.when(f == 0)
    def _():
        o_ref[...] = jnp.zeros(o_ref.shape, F32)

    @pl.when(i < nused)
    def _():
        @pl.when(f == 0)
        def _():
            @pl.when(i == 0)
            def _():
                issue(0)

            wait(i)
            xb[...] = xbuf[...].astype(BF16)

            @pl.when(i + 1 < nused)
            def _():
                issue(i + 1)

        wg = wg_ref[0].astype(BF16)
        wu = wu_ref[0].astype(BF16)
        wd = wd_ref[0].astype(BF16)

        def sub_block(s):
            rows = pl.ds(s * sub, sub)
            x = xb[rows, :]
            gt = _dot(x, wg)
            up = _dot(x, wu)
            hm = (gt * jax.nn.sigmoid(gt) * up).astype(BF16)
            o_ref[rows, :] += _dot(hm, wd)

        sub_block(0)
        for s in range(1, tm // sub):
            pl.when(s < ns_ref[i])(functools.partial(sub_block, s))


def moe_experts(xn, slot_tok, block_e, block_nsub, nused, wg, wu, wd, *, tm, sub, tf):
    T, D = xn.shape
    P = slot_tok.shape[0]
    F = wg.shape[2]
    nf = F // tf

    def w_cols(i, f, tok, be, ns, nu):
        return be[jnp.minimum(i, nu[0] - 1)], 0, jnp.where(i < nu[0], f, nf - 1)

    def w_rows(i, f, tok, be, ns, nu):
        return be[jnp.minimum(i, nu[0] - 1)], jnp.where(i < nu[0], f, nf - 1), 0

    return pl.pallas_call(
        functools.partial(_moe_kernel, tm=tm, sub=sub),
        grid_spec=pltpu.PrefetchScalarGridSpec(
            num_scalar_prefetch=4,
            grid=(P // tm, nf),
            in_specs=[
                pl.BlockSpec(memory_space=pl.ANY),
                pl.BlockSpec((1, D, tf), w_cols),
                pl.BlockSpec((1, D, tf), w_cols),
                pl.BlockSpec((1, tf, D), w_rows),
            ],
            out_specs=pl.BlockSpec((tm, D), lambda i, f, tok, be, ns, nu: (i, 0)),
            scratch_shapes=[
                pltpu.VMEM((tm, D), F32),
                pltpu.VMEM((tm, D), BF16),
                pltpu.SemaphoreType.DMA((1,)),
            ],
        ),
        out_shape=jax.ShapeDtypeStruct((P, D), F32),
        compiler_params=_cparams("arbitrary", "arbitrary"),
        name="moe_experts",
    )(slot_tok, block_e, block_nsub, nused, xn, wg, wu, wd)


def _combine_kernel(pos_ref, h_ref, route_ref, g_ref, ys_hbm, o_ref, buf, sem, *, tm):
    i = pl.program_id(0)

    def row_copy(r, k):
        p = pos_ref[2 * (i * tm + r) + k]
        return pltpu.make_async_copy(ys_hbm.at[pl.ds(p, 1)], buf.at[k, pl.ds(r, 1)], sem.at[k])

    def issue(r, c):
        row_copy(r, 0).start()
        row_copy(r, 1).start()
        return c

    def wait(r, c):
        row_copy(r, 0).wait()
        row_copy(r, 1).wait()
        return c

    lax.fori_loop(0, tm, issue, 0, unroll=8)
    lax.fori_loop(0, tm, wait, 0, unroll=8)
    route = route_ref[...]
    y = route[:, 2:3] * buf[0] + route[:, 3:4] * buf[1]
    o_ref[...] = _rms(h_ref[...] + y, g_ref[...])


def moe_combine_norm(h, route, ys, pos, g, *, tm):
    T, D = h.shape
    tm = min(tm, T)
    return pl.pallas_call(
        functools.partial(_combine_kernel, tm=tm),
        grid_spec=pltpu.PrefetchScalarGridSpec(
            num_scalar_prefetch=1,
            grid=(T // tm,),
            in_specs=[
                pl.BlockSpec((tm, D), lambda i, pos: (i, 0)),
                pl.BlockSpec((tm, LANE), lambda i, pos: (i, 0)),
                pl.BlockSpec((1, D), lambda i, pos: (0, 0)),
                pl.BlockSpec(memory_space=pl.ANY),
            ],
            out_specs=pl.BlockSpec((tm, D), lambda i, pos: (i, 0)),
            scratch_shapes=[pltpu.VMEM((2, tm, D), ys.dtype), pltpu.SemaphoreType.DMA((2,))],
        ),
        out_shape=jax.ShapeDtypeStruct((T, D), F32),
        compiler_params=_cparams("arbitrary"),
        name="moe_combine_norm",
    )(pos, h, route, g.reshape(1, D), ys)


def _dispatch(route, tm, sub):
    T = route.shape[0]
    A = 2 * T
    P = A + N_EXPERTS * tm
    nblk = P // tm
    flat_e = route[:, :2].astype(jnp.int32).reshape(A)
    onehot = (flat_e[:, None] == jnp.arange(N_EXPERTS, dtype=jnp.int32)[None, :]).astype(jnp.int32)
    csum = jnp.cumsum(onehot, axis=0)
    counts = csum[-1]
    rank = jnp.sum(csum * onehot, axis=1) - 1
    padded = ((counts + tm - 1) // tm) * tm
    pend = jnp.cumsum(padded)
    pstart = pend - padded
    dest = (jnp.sum(onehot * pstart[None, :], axis=1) + rank).astype(jnp.int32)
    slot_tok = jnp.zeros((P,), jnp.int32).at[dest].set(jnp.arange(A, dtype=jnp.int32) // 2)
    blk_row0 = jnp.arange(nblk, dtype=jnp.int32) * tm
    block_e = jnp.minimum(jnp.searchsorted(pend, blk_row0, side="right"), N_EXPERTS - 1).astype(jnp.int32)
    live_end = pstart + ((counts + sub - 1) // sub) * sub
    block_nsub = (jnp.clip(live_end[block_e] - blk_row0, 0, tm) // sub).astype(jnp.int32)
    nused = (pend[-1] // tm).astype(jnp.int32).reshape(1)
    return slot_tok, dest, block_e, block_nsub, nused


def _rope_slabs(seq):
    pos = jnp.arange(seq, dtype=F32)
    inv = jnp.power(ROPE_THETA, -jnp.arange(0, MLA_ROPE, 2, dtype=F32) / MLA_ROPE)
    ang = pos[:, None] * inv[None, :]
    z = jnp.zeros((seq, LANE - MLA_ROPE), F32)
    ct = jnp.concatenate([jnp.cos(ang), jnp.cos(ang), z], axis=1)
    st = jnp.concatenate([jnp.sin(ang), jnp.sin(ang), z], axis=1)
    return ct, st


def _t5_bucket(rel):
    half = REL_BUCKETS // 2
    max_exact = half // 2
    ret = (rel > 0).astype(jnp.int32) * half
    n = jnp.abs(rel)
    nf = jnp.maximum(n, 1).astype(F32)
    large = max_exact + (jnp.log(nf / max_exact) / math.log(REL_MAX_DIST / max_exact)
                         * (half - max_exact)).astype(jnp.int32)
    large = jnp.minimum(large, half - 1)
    return ret + jnp.where(n < max_exact, n, large)


def _bias_tiles(rel_bias, tq):
    assert tq >= REL_MAX_DIST
    qi = jnp.arange(tq, dtype=jnp.int32)[:, None]
    ki = jnp.arange(tq, dtype=jnp.int32)[None, :]

    def lookup(rel):
        bucket = _t5_bucket(rel)[None]
        out = jnp.zeros((rel_bias.shape[1],) + rel.shape, F32)
        for b in range(REL_BUCKETS):
            out = jnp.where(bucket == b, rel_bias[b][:, None, None], out)
        return out

    prev = lookup(ki - qi - tq)
    diag = jnp.where(((ki // CHUNK) <= (qi // CHUNK))[None], lookup(ki - qi), NEG)
    far = lookup(jnp.full((1, 1), -2 * tq, jnp.int32))[:, 0, 0]
    return jnp.stack([prev, diag], axis=1), far


def kernel(x, mem, rel_bias, mem_norm_g, norm_mix_g, norm_cross_g, norm_ffn_g, cross_wq, cross_wkv, cross_wo, ev_w_in, ev_conv_w, ev_conv_b, ev_ln_g, ev_ln_b, ev_q_norm_g, ev_w_uq, ev_kv_norm_g, ev_w_ukv, ev_w_out, ev_ffn_wg, ev_ffn_wu, ev_ffn_wd, od_w_in, od_lambda_q1, od_lambda_k1, od_lambda_q2, od_lambda_k2, od_subln_g, od_w_out, od_router, od_moe_wg, od_moe_wu, od_moe_wd, final_norm_g):
    B, S, D = x.shape
    T = B * S
    M = mem.shape[1]
    AW = ev_conv_w.shape[2]
    H = MLA_HEADS
    R = MLA_RANK
    h = x.reshape(T, D)

    w_in0 = ev_w_in[0]
    kr0 = 2 * AW + 2 * R
    half = MLA_ROPE // 2
    w_in0 = jnp.concatenate(
        [w_in0, -w_in0[:, kr0 + half:kr0 + MLA_ROPE], w_in0[:, kr0:kr0 + half]], axis=1).astype(BF16)

    q_scale = (MLA_NOPE + MLA_ROPE) ** -0.5
    wuq = (ev_w_uq[0] * q_scale).reshape(R, H, MLA_NOPE + MLA_ROPE)
    w_nope, w_r1, w_r2 = wuq[..., :MLA_NOPE], wuq[..., MLA_NOPE:MLA_NOPE + half], wuq[..., MLA_NOPE + half:]
    zq = jnp.zeros((R, H, MLA_QK_PAD - MLA_NOPE - MLA_ROPE), F32)
    wqa = jnp.concatenate([w_nope, w_r1, w_r2, zq], axis=-1).reshape(R, H * MLA_QK_PAD).astype(BF16)
    wqb = jnp.concatenate([-w_r2, w_r1, zq], axis=-1).reshape(R, H * LANE).astype(BF16)
    wukv = ev_w_ukv[0].reshape(R, H, MLA_NOPE + MLA_V)
    wk = wukv[..., :MLA_NOPE].reshape(R, H * MLA_NOPE).astype(BF16)
    wv = wukv[..., MLA_NOPE:].reshape(R, H * MLA_V).astype(BF16)
    ct, st = _rope_slabs(S)

    c_scale = CROSS_HD ** -0.5
    wq_c = (cross_wq * c_scale).astype(BF16)
    wkv_c = jnp.concatenate([cross_wkv[0], cross_wkv[1]], axis=1).astype(BF16)
    wo_c = cross_wo.astype(BF16)

    d_scale = DIFF_HD ** -0.5
    layer = 1
    lambda_init = 0.8 - 0.6 * math.exp(-0.3 * layer)
    lam = (jnp.exp(jnp.sum(od_lambda_q1[0] * od_lambda_k1[0]))
           - jnp.exp(jnp.sum(od_lambda_q2[0] * od_lambda_k2[0])) + lambda_init).reshape(1).astype(F32)
    tq = min(256, S)
    bias_tiles, far_bias = _bias_tiles(rel_bias, tq)

    kv_mem = norm_matmul(mem.reshape(B * M, D), mem_norm_g, wkv_c, tm=512, tn=512)
    kv_mem = kv_mem.reshape(B, M, -1)

    z = norm_matmul(h, norm_mix_g[0], w_in0, tm=1024, tn=640)
    a_out = conformer_conv(z.reshape(B, S, -1), ev_conv_w[0], ev_conv_b[0], ev_ln_g[0], ev_ln_b[0], ts=256)
    q, k, v = mla_proj(z, ev_q_norm_g[0], ev_kv_norm_g[0], wqa, wqb, wk, wv, ct, st, seq=S, tm=512)
    b_out = mla_attention(q.reshape(B, S, -1), k.reshape(B, S, -1), v.reshape(B, S, -1), tq=tq)
    h = matmul_res(h, [a_out.reshape(T, AW), b_out.reshape(T, H * MLA_V)], ev_w_out[0], tm=1024, tn=1024)
    h = cross_attention(h.reshape(B, S, D), norm_cross_g[0], wq_c[0], kv_mem, wo_c[0], kv_blk=0, tm=512)
    h = dense_ffn(h.reshape(T, D), norm_ffn_g[0], ev_ffn_wg[0], ev_ffn_wu[0], ev_ffn_wd[0], tm=1024, tf=256)

    qkv = norm_matmul(h, norm_mix_g[1], od_w_in[0], tm=1024, tn=1024, scaled_cols=D, scale=d_scale)
    o = diff_attention(qkv.reshape(B, S, -1), lam, far_bias, bias_tiles, od_subln_g[0], tq=tq,
                       out_scale=1.0 - lambda_init)
    h = matmul_res(h, [o.reshape(T, D)], od_w_out[0], tm=1024, tn=1024)
    h = cross_attention(h.reshape(B, S, D), norm_cross_g[1], wq_c[1], kv_mem, wo_c[1], kv_blk=1, tm=512)
    h = h.reshape(T, D)

    tm_moe, sub_moe = 1024, 512
    xn, route = moe_router(h, norm_ffn_g[1], od_router[0], tm=512)
    slot_tok, dest, block_e, block_nsub, nused = _dispatch(route, tm_moe, sub_moe)
    ys = moe_experts(xn, slot_tok, block_e, block_nsub, nused, od_moe_wg[0], od_moe_wu[0], od_moe_wd[0],
                     tm=tm_moe, sub=sub_moe, tf=256)
    out = moe_combine_norm(h, route, ys, dest, final_norm_g, tm=256)
    return out.reshape(B, S, D)
```

```python
import functools
import math

import jax
import jax.numpy as jnp
from jax import lax
from jax.experimental import pallas as pl
from jax.experimental.pallas import tpu as pltpu

F32 = jnp.float32
BF16 = jnp.bfloat16

EPS = 1e-6
NEG = -1e30
CHUNK = 64

CONV_WIDTH = 31
CONV_HALO = 32
MLA_HEADS = 8
MLA_NOPE = 128
MLA_ROPE = 64
MLA_V = 128
MLA_RANK = 512
MLA_QK_PAD = 256
ROPE_THETA = 10000.0
DIFF_HEADS = 8
DIFF_HD = 128
REL_BUCKETS = 32
REL_MAX_DIST = 128
CROSS_HEADS = 4
CROSS_HD = 128
N_EXPERTS = 8
LANE = 128

VMEM_LIMIT = 56 * 1024 * 1024

NT_DIMS = (((1,), (1,)), ((), ()))


def _cparams(*sem):
    return pltpu.CompilerParams(dimension_semantics=sem, vmem_limit_bytes=VMEM_LIMIT)


def _rms(x, g):
    return x * lax.rsqrt(jnp.mean(x * x, axis=-1, keepdims=True) + EPS) * g


def _dot(a, b):
    return jnp.dot(a, b, preferred_element_type=F32)


W_SPLIT = 4


def _load_bf16(w_ref):
    w = w_ref[0] if len(w_ref.shape) == 3 else w_ref[...]
    return w.astype(BF16)


def _dot_ksplit(x, w_refs):
    kc = w_refs[0].shape[-2]
    acc = None
    for c, w_ref in enumerate(w_refs):
        part = _dot(x[:, c * kc:(c + 1) * kc], _load_bf16(w_ref))
        acc = part if acc is None else acc + part
    return acc


def _split_specs(block, index_map, axis, n=None):
    n = n or W_SPLIT
    shape = list(block)
    assert shape[axis] % n == 0
    shape[axis] //= n

    def chunk_map(c):
        def im(*args):
            idx = list(index_map(*args))
            idx[axis] = idx[axis] * n + c
            return tuple(idx)
        return im

    return [pl.BlockSpec(tuple(shape), chunk_map(c)) for c in range(n)]


def _norm_matmul_kernel(x_ref, g_ref, *refs, scaled_tiles, scale):
    *w_refs, o_ref, xn_ref = refs
    j = pl.program_id(1)

    @pl.when(j == 0)
    def _():
        xn_ref[...] = _rms(x_ref[...], g_ref[...]).astype(BF16)

    acc = _dot_ksplit(xn_ref[...], w_refs)
    if scaled_tiles:
        acc = acc * jnp.where(j < scaled_tiles, scale, 1.0)
    o_ref[...] = acc.astype(o_ref.dtype)


def norm_matmul(x, g, w, *, tm, tn, scaled_cols=0, scale=1.0):
    M, K = x.shape
    N = w.shape[1]
    tm = min(tm, M)
    assert scaled_cols % tn == 0
    w_specs = _split_specs((K, tn), lambda i, j: (0, j), axis=0)
    return pl.pallas_call(
        functools.partial(_norm_matmul_kernel, scaled_tiles=scaled_cols // tn, scale=scale),
        grid=(M // tm, N // tn),
        in_specs=[
            pl.BlockSpec((tm, K), lambda i, j: (i, 0)),
            pl.BlockSpec((1, K), lambda i, j: (0, 0)),
            *w_specs,
        ],
        out_specs=pl.BlockSpec((tm, tn), lambda i, j: (i, j)),
        out_shape=jax.ShapeDtypeStruct((M, N), BF16),
        scratch_shapes=[pltpu.VMEM((tm, K), BF16)],
        compiler_params=_cparams("parallel", "arbitrary"),
        name="norm_matmul",
    )(x, g.reshape(1, K), *([w] * len(w_specs)))


def _matmul_res_kernel(res_ref, *refs, n):
    a_refs, w_refs, o_ref = refs[:n], refs[n:-1], refs[-1]
    per_a = len(w_refs) // n
    acc = res_ref[...]
    for k in range(n):
        acc = acc + _dot_ksplit(a_refs[k][...], w_refs[k * per_a:(k + 1) * per_a])
    o_ref[...] = acc


def matmul_res(res, a_list, w, *, tm, tn):
    M, N = res.shape
    K = w.shape[0]
    tm = min(tm, M)
    n = len(a_list)
    ka = a_list[0].shape[1]
    assert all(a.shape[1] == ka for a in a_list) and K == n * ka and W_SPLIT % n == 0
    w_specs = _split_specs((K, tn), lambda i, j: (0, j), axis=0)
    in_specs = [pl.BlockSpec((tm, tn), lambda i, j: (i, j))]
    in_specs += [pl.BlockSpec((tm, ka), lambda i, j: (i, 0)) for _ in a_list]
    return pl.pallas_call(
        functools.partial(_matmul_res_kernel, n=n),
        grid=(M // tm, N // tn),
        in_specs=in_specs + w_specs,
        out_specs=pl.BlockSpec((tm, tn), lambda i, j: (i, j)),
        out_shape=jax.ShapeDtypeStruct((M, N), F32),
        compiler_params=_cparams("parallel", "arbitrary"),
        name="matmul_res",
    )(res, *a_list, *([w] * len(w_specs)))


def _conv_kernel(val_ref, gate_ref, w_ref, b_ref, lg_ref, lb_ref, o_ref, ubuf, *, ts):
    s = pl.program_id(1)

    @pl.when(s == 0)
    def _():
        ubuf[0:CONV_HALO, :] = jnp.zeros((CONV_HALO, ubuf.shape[1]), F32)

    @pl.when(s > 0)
    def _():
        ubuf[0:CONV_HALO, :] = ubuf[ts:ts + CONV_HALO, :]

    val = val_ref[0].astype(F32)
    gate = gate_ref[0].astype(F32)
    ubuf[CONV_HALO:CONV_HALO + ts, :] = val * jax.nn.sigmoid(gate)

    base = CONV_HALO - (CONV_WIDTH - 1)
    acc = jnp.zeros((ts, ubuf.shape[1]), F32) + b_ref[...]
    for j in range(CONV_WIDTH):
        acc = acc + w_ref[j:j + 1, :] * ubuf[base + j:base + j + ts, :]

    mu = jnp.mean(acc, axis=-1, keepdims=True)
    xc = acc - mu
    y = xc * lax.rsqrt(jnp.mean(xc * xc, axis=-1, keepdims=True) + EPS)
    y = y * lg_ref[...] + lb_ref[...]
    o_ref[0] = (y * jax.nn.sigmoid(y)).astype(o_ref.dtype)


def conformer_conv(z, conv_w, conv_b, ln_g, ln_b, *, ts):
    B, S, _ = z.shape
    C = conv_w.shape[1]
    ts = min(ts, S)
    wpad = jnp.zeros((CONV_HALO, C), F32).at[:CONV_WIDTH].set(conv_w)
    row = lambda v: v.reshape(1, C)
    const = lambda shape: pl.BlockSpec(shape, lambda b, s: (0, 0))
    return pl.pallas_call(
        functools.partial(_conv_kernel, ts=ts),
        grid=(B, S // ts),
        in_specs=[
            pl.BlockSpec((1, ts, C), lambda b, s: (b, s, 0)),
            pl.BlockSpec((1, ts, C), lambda b, s: (b, s, 1)),
            const((CONV_HALO, C)), const((1, C)), const((1, C)), const((1, C)),
        ],
        out_specs=pl.BlockSpec((1, ts, C), lambda b, s: (b, s, 0)),
        out_shape=jax.ShapeDtypeStruct((B, S, C), BF16),
        scratch_shapes=[pltpu.VMEM((ts + CONV_HALO, C), F32)],
        compiler_params=_cparams("parallel", "arbitrary"),
        name="conformer_conv",
    )(z, z, wpad, row(conv_b), row(ln_g), row(ln_b))


def _mla_proj_kernel(cq_ref, ckv_ref, kr_ref, qg_ref, kvg_ref, wqa_ref, wqb_ref, wk_ref, wv_ref,
                     ct_ref, st_ref, q_ref, k_ref, v_ref):
    cqn = _rms(cq_ref[...].astype(F32), qg_ref[...]).astype(BF16)
    ckvn = _rms(ckv_ref[...].astype(F32), kvg_ref[...]).astype(BF16)
    ct = ct_ref[...]
    st = st_ref[...]

    qa = _dot(cqn, wqa_ref[...])
    qb = _dot(cqn, wqb_ref[...])
    kn = _dot(ckvn, wk_ref[...])
    v_ref[...] = _dot(ckvn, wv_ref[...]).astype(BF16)

    kr = kr_ref[...].astype(F32)
    k_rope = (kr * ct + pltpu.roll(kr, LANE // 2, axis=1) * st).astype(BF16)

    for h in range(MLA_HEADS):
        o = h * MLA_QK_PAD
        q_ref[:, o:o + LANE] = qa[:, o:o + LANE].astype(BF16)
        q_ref[:, o + LANE:o + 2 * LANE] = (
            qa[:, o + LANE:o + 2 * LANE] * ct + qb[:, h * LANE:(h + 1) * LANE] * st).astype(BF16)
        k_ref[:, o:o + LANE] = kn[:, h * LANE:(h + 1) * LANE].astype(BF16)
        k_ref[:, o + LANE:o + 2 * LANE] = k_rope


def mla_proj(z, q_norm_g, kv_norm_g, wqa, wqb, wk, wv, ct, st, *, seq, tm):
    T = z.shape[0]
    tm = min(tm, seq)
    n_s = seq // tm
    R = MLA_RANK
    cq_blk = (2 * 1024) // R
    kr_blk = (2 * 1024 + 2 * R) // LANE
    const = lambda shape: pl.BlockSpec(shape, lambda i: (0, 0))
    HQ = MLA_HEADS * MLA_QK_PAD
    HV = MLA_HEADS * MLA_V
    return pl.pallas_call(
        _mla_proj_kernel,
        grid=(T // tm,),
        in_specs=[
            pl.BlockSpec((tm, R), lambda i: (i, cq_blk)),
            pl.BlockSpec((tm, R), lambda i: (i, cq_blk + 1)),
            pl.BlockSpec((tm, LANE), lambda i: (i, kr_blk)),
            const((1, R)), const((1, R)),
            const((R, HQ)), const((R, MLA_HEADS * LANE)), const((R, HV)), const((R, HV)),
            pl.BlockSpec((tm, LANE), lambda i: (i % n_s, 0)),
            pl.BlockSpec((tm, LANE), lambda i: (i % n_s, 0)),
        ],
        out_specs=[
            pl.BlockSpec((tm, HQ), lambda i: (i, 0)),
            pl.BlockSpec((tm, HQ), lambda i: (i, 0)),
            pl.BlockSpec((tm, HV), lambda i: (i, 0)),
        ],
        out_shape=[
            jax.ShapeDtypeStruct((T, HQ), BF16),
            jax.ShapeDtypeStruct((T, HQ), BF16),
            jax.ShapeDtypeStruct((T, HV), BF16),
        ],
        compiler_params=_cparams("parallel"),
        name="mla_proj",
    )(z, z, z, q_norm_g.reshape(1, R), kv_norm_g.reshape(1, R), wqa, wqb, wk, wv, ct, st)


def _softmax_pv(s, v):
    m = jnp.max(s, axis=-1, keepdims=True)
    p = jnp.exp(s - m)
    l = jnp.sum(p, axis=-1, keepdims=True)
    return _dot(p.astype(BF16), v) / l


def _for_each_tile(i, n_tiles, fn):
    for c in range(n_tiles):
        pl.when(i == c)(functools.partial(fn, c))


def _mla_attn_kernel(q_ref, k_ref, v_ref, o_ref, *, tq, n_tiles):
    q = q_ref[0]
    qc = lax.broadcasted_iota(jnp.int32, (tq, tq), 0) // CHUNK
    kc = lax.broadcasted_iota(jnp.int32, (tq, tq), 1) // CHUNK
    visible = kc <= qc

    def tile(c):
        n = (c + 1) * tq
        s = lax.dot_general(q, k_ref[0, :n, :], NT_DIMS, preferred_element_type=F32)
        parts = [s[:, :c * tq]] if c else []
        parts.append(jnp.where(visible, s[:, c * tq:], NEG))
        s = jnp.concatenate(parts, axis=1) if c else parts[0]
        o_ref[0] = _softmax_pv(s, v_ref[0, :n, :]).astype(o_ref.dtype)

    _for_each_tile(pl.program_id(2), n_tiles, tile)


def mla_attention(q, k, v, *, tq):
    B, S, _ = q.shape
    tq = min(tq, S)
    return pl.pallas_call(
        functools.partial(_mla_attn_kernel, tq=tq, n_tiles=S // tq),
        grid=(B, MLA_HEADS, S // tq),
        in_specs=[
            pl.BlockSpec((1, tq, MLA_QK_PAD), lambda b, h, i: (b, i, h)),
            pl.BlockSpec((1, S, MLA_QK_PAD), lambda b, h, i: (b, 0, h)),
            pl.BlockSpec((1, S, MLA_V), lambda b, h, i: (b, 0, h)),
        ],
        out_specs=pl.BlockSpec((1, tq, MLA_V), lambda b, h, i: (b, i, h)),
        out_shape=jax.ShapeDtypeStruct((B, S, MLA_HEADS * MLA_V), BF16),
        compiler_params=_cparams("parallel", "parallel", "arbitrary"),
        name="mla_attention",
    )(q, k, v)


def _diff_attn_kernel(lam_ref, far_ref, q_ref, k_ref, v_ref, bias_ref, g_ref, o_ref, *, tq, n_tiles, out_scale):
    far = far_ref[pl.program_id(1)]
    lam = lam_ref[0]

    def tile(c):
        n = (c + 1) * tq
        v = v_ref[0, :n, :]

        def half(lo):
            s = lax.dot_general(q_ref[0, :, lo:lo + DIFF_HD], k_ref[0, :n, lo:lo + DIFF_HD], NT_DIMS,
                                preferred_element_type=F32)
            parts = []
            if c >= 2:
                parts.append(s[:, :(c - 1) * tq] + far)
            if c >= 1:
                parts.append(s[:, (c - 1) * tq:c * tq] + bias_ref[0, 0])
            parts.append(s[:, c * tq:] + bias_ref[0, 1])
            s = jnp.concatenate(parts, axis=1) if c else parts[0]
            return _softmax_pv(s, v)

        o = half(0) - lam * half(DIFF_HD)
        o_ref[0] = (_rms(o, g_ref[...]) * out_scale).astype(o_ref.dtype)

    _for_each_tile(pl.program_id(2), n_tiles, tile)


def diff_attention(qkv, lam, far_bias, bias_tiles, subln_g, *, tq, out_scale):
    B, S, _ = qkv.shape
    H = DIFF_HEADS
    dv = 2 * DIFF_HD
    smem = pl.BlockSpec(memory_space=pltpu.SMEM)
    return pl.pallas_call(
        functools.partial(_diff_attn_kernel, tq=tq, n_tiles=S // tq, out_scale=out_scale),
        grid=(B, H, S // tq),
        in_specs=[
            smem, smem,
            pl.BlockSpec((1, tq, dv), lambda b, h, i: (b, i, h)),
            pl.BlockSpec((1, S, dv), lambda b, h, i: (b, 0, H + h)),
            pl.BlockSpec((1, S, dv), lambda b, h, i: (b, 0, 2 * H + h)),
            pl.BlockSpec((1, 2, tq, tq), lambda b, h, i: (h, 0, 0, 0)),
            pl.BlockSpec((1, dv), lambda b, h, i: (0, 0)),
        ],
        out_specs=pl.BlockSpec((1, tq, dv), lambda b, h, i: (b, i, h)),
        out_shape=jax.ShapeDtypeStruct((B, S, H * dv), BF16),
        compiler_params=_cparams("parallel", "parallel", "arbitrary"),
        name="diff_attention",
    )(lam, far_bias, qkv, qkv, qkv, bias_tiles, subln_g.reshape(1, dv))


def _cross_kernel(h_ref, g_ref, wq_ref, kv_ref, wo_ref, o_ref):
    h = h_ref[0]
    hn = _rms(h, g_ref[...]).astype(BF16)
    q = _dot(hn, wq_ref[...]).astype(BF16)
    kv = kv_ref[0]
    HD = CROSS_HEADS * CROSS_HD
    outs = []
    for hd in range(CROSS_HEADS):
        lo = hd * CROSS_HD
        s = lax.dot_general(q[:, lo:lo + CROSS_HD], kv[:, lo:lo + CROSS_HD], NT_DIMS,
                            preferred_element_type=F32)
        m = jnp.max(s, axis=-1, keepdims=True)
        p = jnp.exp(s - m)
        l = jnp.sum(p, axis=-1, keepdims=True)
        o = _dot(p.astype(BF16), kv[:, HD + lo:HD + lo + CROSS_HD]) / l
        outs.append(o.astype(BF16))
    o_all = jnp.concatenate(outs, axis=-1)
    o_ref[0] = h + _dot(o_all, wo_ref[...])


def cross_attention(h, g, wq, kv, wo, *, kv_blk, tm):
    B, S, D = h.shape
    M = kv.shape[1]
    HD = CROSS_HEADS * CROSS_HD
    tm = min(tm, S)
    const = lambda shape: pl.BlockSpec(shape, lambda b, s: (0, 0))
    return pl.pallas_call(
        _cross_kernel,
        grid=(B, S // tm),
        in_specs=[
            pl.BlockSpec((1, tm, D), lambda b, s: (b, s, 0)),
            const((1, D)), const((D, HD)),
            pl.BlockSpec((1, M, 2 * HD), lambda b, s: (b, 0, kv_blk)),
            const((HD, D)),
        ],
        out_specs=pl.BlockSpec((1, tm, D), lambda b, s: (b, s, 0)),
        out_shape=jax.ShapeDtypeStruct((B, S, D), F32),
        compiler_params=_cparams("parallel", "arbitrary"),
        name="cross_attention",
    )(h, g.reshape(1, D), wq, kv, wo)


def _swiglu_step(x, wg_refs, wu_refs, wd_refs, o_ref, rows=slice(None)):
    gt = _dot_ksplit(x, wg_refs)
    up = _dot_ksplit(x, wu_refs)
    hm = (gt * jax.nn.sigmoid(gt) * up).astype(BF16)
    nc = wd_refs[0].shape[-1]
    for c, wd_ref in enumerate(wd_refs):
        o_ref[rows, c * nc:(c + 1) * nc] += _dot(hm, _load_bf16(wd_ref))


def _ffn_kernel(h_ref, g_ref, *refs):
    *w_refs, o_ref, xn_ref = refs
    s = len(w_refs) // 3

    @pl.when(pl.program_id(1) == 0)
    def _():
        xn_ref[...] = _rms(h_ref[...], g_ref[...]).astype(BF16)
        o_ref[...] = h_ref[...]

    _swiglu_step(xn_ref[...], w_refs[:s], w_refs[s:2 * s], w_refs[2 * s:], o_ref)


def dense_ffn(h, g, wg, wu, wd, *, tm, tf):
    T, D = h.shape
    F = wg.shape[1]
    tm = min(tm, T)
    up_specs = _split_specs((D, tf), lambda i, f: (0, f), axis=0)
    down_specs = _split_specs((tf, D), lambda i, f: (f, 0), axis=1)
    s = len(up_specs)
    return pl.pallas_call(
        _ffn_kernel,
        grid=(T // tm, F // tf),
        in_specs=[
            pl.BlockSpec((tm, D), lambda i, f: (i, 0), pipeline_mode=pl.Buffered(1)),
            pl.BlockSpec((1, D), lambda i, f: (0, 0)),
            *up_specs, *up_specs, *down_specs,
        ],
        out_specs=pl.BlockSpec((tm, D), lambda i, f: (i, 0)),
        out_shape=jax.ShapeDtypeStruct((T, D), F32),
        scratch_shapes=[pltpu.VMEM((tm, D), BF16)],
        compiler_params=_cparams("parallel", "arbitrary"),
        name="dense_ffn",
    )(h, g.reshape(1, D), *([wg] * s), *([wu] * s), *([wd] * s))


def _router_kernel(h_ref, g_ref, wr_ref, xn_ref, route_ref):
    xn = _rms(h_ref[...], g_ref[...])
    xn_ref[...] = xn
    logits = jnp.dot(xn, wr_ref[...], preferred_element_type=F32, precision=lax.Precision.HIGHEST)
    lane = lax.broadcasted_iota(jnp.int32, logits.shape, 1)
    logits = jnp.where(lane < N_EXPERTS, logits, -jnp.inf)
    v1 = jnp.max(logits, axis=-1, keepdims=True)
    i1 = jnp.min(jnp.where(logits == v1, lane, LANE), axis=-1, keepdims=True)
    rest = jnp.where(lane == i1, -jnp.inf, logits)
    v2 = jnp.max(rest, axis=-1, keepdims=True)
    i2 = jnp.min(jnp.where(rest == v2, lane, LANE), axis=-1, keepdims=True)
    e2 = jnp.exp(v2 - v1)
    g1 = 1.0 / (1.0 + e2)
    g2 = e2 / (1.0 + e2)
    route = jnp.where(lane == 0, i1.astype(F32), 0.0)
    route = jnp.where(lane == 1, i2.astype(F32), route)
    route = jnp.where(lane == 2, g1, route)
    route = jnp.where(lane == 3, g2, route)
    route_ref[...] = route


def moe_router(h, g, w_router, *, tm):
    T, D = h.shape
    tm = min(tm, T)
    wr = jnp.zeros((D, LANE), F32).at[:, :N_EXPERTS].set(w_router)
    return pl.pallas_call(
        _router_kernel,
        grid=(T // tm,),
        in_specs=[
            pl.BlockSpec((tm, D), lambda i: (i, 0)),
            pl.BlockSpec((1, D), lambda i: (0, 0)),
            pl.BlockSpec((D, LANE), lambda i: (0, 0)),
        ],
        out_specs=[pl.BlockSpec((tm, D), lambda i: (i, 0)), pl.BlockSpec((tm, LANE), lambda i: (i, 0))],
        out_shape=[jax.ShapeDtypeStruct((T, D), F32), jax.ShapeDtypeStruct((T, LANE), F32)],
        compiler_params=_cparams("parallel"),
        name="moe_router",
    )(h, g.reshape(1, D), wr)


def _moe_kernel(tok_ref, be_ref, ns_ref, nu_ref, x_hbm, *refs, tm, sub):
    *w_refs, o_ref, xbuf, xb, sem = refs
    s = len(w_refs) // 3
    i = pl.program_id(0)
    f = pl.program_id(1)
    nused = nu_ref[0]

    def row_copy(blk, r):
        tok = tok_ref[blk * tm + r]
        return pltpu.make_async_copy(x_hbm.at[pl.ds(tok, 1)], xbuf.at[pl.ds(r, 1)], sem.at[0])

    def issue(blk):
        def body(r, c):
            row_copy(blk, r).start()
            return c
        lax.fori_loop(0, tm, body, 0, unroll=8)

    def wait(blk):
        def body(r, c):
            row_copy(blk, r).wait()
            return c
        lax.fori_loop(0, tm, body, 0, unroll=8)

    @pl.when(f == 0)
    def _():
        o_ref[...] = jnp.zeros(o_ref.shape, F32)

    @pl.when(i < nused)
    def _():
        @pl.when(f == 0)
        def _():
            @pl.when(i == 0)
            def _():
                issue(0)

            wait(i)
            xb[...] = xbuf[...].astype(BF16)

            @pl.when(i + 1 < nused)
            def _():
                issue(i + 1)

        def sub_block(r):
            rows = pl.ds(r * sub, sub)
            _swiglu_step(xb[rows, :], w_refs[:s], w_refs[s:2 * s], w_refs[2 * s:], o_ref, rows)

        sub_block(0)
        for r in range(1, tm // sub):
            pl.when(r < ns_ref[i])(functools.partial(sub_block, r))


def moe_experts(xn, slot_tok, block_e, block_nsub, nused, wg, wu, wd, *, tm, sub, tf):
    T, D = xn.shape
    P = slot_tok.shape[0]
    F = wg.shape[2]
    nf = F // tf

    def w_cols(i, f, tok, be, ns, nu):
        return be[jnp.minimum(i, nu[0] - 1)], 0, jnp.where(i < nu[0], f, nf - 1)

    def w_rows(i, f, tok, be, ns, nu):
        return be[jnp.minimum(i, nu[0] - 1)], jnp.where(i < nu[0], f, nf - 1), 0

    up_specs = _split_specs((1, D, tf), w_cols, axis=1)
    down_specs = _split_specs((1, tf, D), w_rows, axis=2)
    s = len(up_specs)
    return pl.pallas_call(
        functools.partial(_moe_kernel, tm=tm, sub=sub),
        grid_spec=pltpu.PrefetchScalarGridSpec(
            num_scalar_prefetch=4,
            grid=(P // tm, nf),
            in_specs=[pl.BlockSpec(memory_space=pl.ANY), *up_specs, *up_specs, *down_specs],
            out_specs=pl.BlockSpec((tm, D), lambda i, f, tok, be, ns, nu: (i, 0)),
            scratch_shapes=[
                pltpu.VMEM((tm, D), F32),
                pltpu.VMEM((tm, D), BF16),
                pltpu.SemaphoreType.DMA((1,)),
            ],
        ),
        out_shape=jax.ShapeDtypeStruct((P, D), F32),
        compiler_params=_cparams("arbitrary", "arbitrary"),
        name="moe_experts",
    )(slot_tok, block_e, block_nsub, nused, xn, *([wg] * s), *([wu] * s), *([wd] * s))


def _combine_kernel(pos_ref, h_ref, route_ref, g_ref, ys_hbm, o_ref, buf, sem, *, tm):
    i = pl.program_id(0)

    def row_copy(r, k):
        p = pos_ref[2 * (i * tm + r) + k]
        return pltpu.make_async_copy(ys_hbm.at[pl.ds(p, 1)], buf.at[k, pl.ds(r, 1)], sem.at[k])

    def issue(r, c):
        row_copy(r, 0).start()
        row_copy(r, 1).start()
        return c

    def wait(r, c):
        row_copy(r, 0).wait()
        row_copy(r, 1).wait()
        return c

    lax.fori_loop(0, tm, issue, 0, unroll=8)
    lax.fori_loop(0, tm, wait, 0, unroll=8)
    route = route_ref[...]
    y = route[:, 2:3] * buf[0] + route[:, 3:4] * buf[1]
    o_ref[...] = _rms(h_ref[...] + y, g_ref[...])


def moe_combine_norm(h, route, ys, pos, g, *, tm):
    T, D = h.shape
    tm = min(tm, T)
    return pl.pallas_call(
        functools.partial(_combine_kernel, tm=tm),
        grid_spec=pltpu.PrefetchScalarGridSpec(
            num_scalar_prefetch=1,
            grid=(T // tm,),
            in_specs=[
                pl.BlockSpec((tm, D), lambda i, pos: (i, 0)),
                pl.BlockSpec((tm, LANE), lambda i, pos: (i, 0)),
                pl.BlockSpec((1, D), lambda i, pos: (0, 0)),
                pl.BlockSpec(memory_space=pl.ANY),
            ],
            out_specs=pl.BlockSpec((tm, D), lambda i, pos: (i, 0)),
            scratch_shapes=[pltpu.VMEM((2, tm, D), ys.dtype), pltpu.SemaphoreType.DMA((2,))],
        ),
        out_shape=jax.ShapeDtypeStruct((T, D), F32),
        compiler_params=_cparams("arbitrary"),
        name="moe_combine_norm",
    )(pos, h, route, g.reshape(1, D), ys)


def _dispatch(route, tm, sub):
    T = route.shape[0]
    A = 2 * T
    P = A + N_EXPERTS * tm
    nblk = P // tm
    flat_e = route[:, :2].astype(jnp.int32).reshape(A)
    onehot = (flat_e[:, None] == jnp.arange(N_EXPERTS, dtype=jnp.int32)[None, :]).astype(jnp.int32)
    csum = jnp.cumsum(onehot, axis=0)
    counts = csum[-1]
    rank = jnp.sum(csum * onehot, axis=1) - 1
    padded = ((counts + tm - 1) // tm) * tm
    pend = jnp.cumsum(padded)
    pstart = pend - padded
    dest = (jnp.sum(onehot * pstart[None, :], axis=1) + rank).astype(jnp.int32)
    slot_tok = jnp.zeros((P,), jnp.int32).at[dest].set(jnp.arange(A, dtype=jnp.int32) // 2)
    blk_row0 = jnp.arange(nblk, dtype=jnp.int32) * tm
    block_e = jnp.minimum(jnp.searchsorted(pend, blk_row0, side="right"), N_EXPERTS - 1).astype(jnp.int32)
    live_end = pstart + ((counts + sub - 1) // sub) * sub
    block_nsub = (jnp.clip(live_end[block_e] - blk_row0, 0, tm) // sub).astype(jnp.int32)
    nused = (pend[-1] // tm).astype(jnp.int32).reshape(1)
    return slot_tok, dest, block_e, block_nsub, nused


def _rope_slabs(seq):
    pos = jnp.arange(seq, dtype=F32)
    inv = jnp.power(ROPE_THETA, -jnp.arange(0, MLA_ROPE, 2, dtype=F32) / MLA_ROPE)
    ang = pos[:, None] * inv[None, :]
    z = jnp.zeros((seq, LANE - MLA_ROPE), F32)
    ct = jnp.concatenate([jnp.cos(ang), jnp.cos(ang), z], axis=1)
    st = jnp.concatenate([jnp.sin(ang), jnp.sin(ang), z], axis=1)
    return ct, st


def _t5_bucket(rel):
    half = REL_BUCKETS // 2
    max_exact = half // 2
    ret = (rel > 0).astype(jnp.int32) * half
    n = jnp.abs(rel)
    nf = jnp.maximum(n, 1).astype(F32)
    large = max_exact + (jnp.log(nf / max_exact) / math.log(REL_MAX_DIST / max_exact)
                         * (half - max_exact)).astype(jnp.int32)
    large = jnp.minimum(large, half - 1)
    return ret + jnp.where(n < max_exact, n, large)


def _bias_tiles(rel_bias, tq):
    assert tq >= REL_MAX_DIST
    qi = jnp.arange(tq, dtype=jnp.int32)[:, None]
    ki = jnp.arange(tq, dtype=jnp.int32)[None, :]

    def lookup(rel):
        bucket = _t5_bucket(rel)[None]
        out = jnp.zeros((rel_bias.shape[1],) + rel.shape, F32)
        for b in range(REL_BUCKETS):
            out = jnp.where(bucket == b, rel_bias[b][:, None, None], out)
        return out

    prev = lookup(ki - qi - tq)
    diag = jnp.where(((ki // CHUNK) <= (qi // CHUNK))[None], lookup(ki - qi), NEG)
    far = lookup(jnp.full((1, 1), -2 * tq, jnp.int32))[:, 0, 0]
    return jnp.stack([prev, diag], axis=1), far


def kernel(x, mem, rel_bias, mem_norm_g, norm_mix_g, norm_cross_g, norm_ffn_g, cross_wq, cross_wkv, cross_wo, ev_w_in, ev_conv_w, ev_conv_b, ev_ln_g, ev_ln_b, ev_q_norm_g, ev_w_uq, ev_kv_norm_g, ev_w_ukv, ev_w_out, ev_ffn_wg, ev_ffn_wu, ev_ffn_wd, od_w_in, od_lambda_q1, od_lambda_k1, od_lambda_q2, od_lambda_k2, od_subln_g, od_w_out, od_router, od_moe_wg, od_moe_wu, od_moe_wd, final_norm_g):
    B, S, D = x.shape
    T = B * S
    M = mem.shape[1]
    AW = ev_conv_w.shape[2]
    H = MLA_HEADS
    R = MLA_RANK
    h = x.reshape(T, D)

    w_in0 = ev_w_in[0]
    kr0 = 2 * AW + 2 * R
    half = MLA_ROPE // 2
    w_in0 = jnp.concatenate(
        [w_in0, -w_in0[:, kr0 + half:kr0 + MLA_ROPE], w_in0[:, kr0:kr0 + half]], axis=1).astype(BF16)

    q_scale = (MLA_NOPE + MLA_ROPE) ** -0.5
    wuq = (ev_w_uq[0] * q_scale).reshape(R, H, MLA_NOPE + MLA_ROPE)
    w_nope, w_r1, w_r2 = wuq[..., :MLA_NOPE], wuq[..., MLA_NOPE:MLA_NOPE + half], wuq[..., MLA_NOPE + half:]
    zq = jnp.zeros((R, H, MLA_QK_PAD - MLA_NOPE - MLA_ROPE), F32)
    wqa = jnp.concatenate([w_nope, w_r1, w_r2, zq], axis=-1).reshape(R, H * MLA_QK_PAD).astype(BF16)
    wqb = jnp.concatenate([-w_r2, w_r1, zq], axis=-1).reshape(R, H * LANE).astype(BF16)
    wukv = ev_w_ukv[0].reshape(R, H, MLA_NOPE + MLA_V)
    wk = wukv[..., :MLA_NOPE].reshape(R, H * MLA_NOPE).astype(BF16)
    wv = wukv[..., MLA_NOPE:].reshape(R, H * MLA_V).astype(BF16)
    ct, st = _rope_slabs(S)

    c_scale = CROSS_HD ** -0.5
    wq_c = (cross_wq * c_scale).astype(BF16)
    wkv_c = jnp.concatenate([cross_wkv[0], cross_wkv[1]], axis=1).astype(BF16)
    wo_c = cross_wo.astype(BF16)

    d_scale = DIFF_HD ** -0.5
    layer = 1
    lambda_init = 0.8 - 0.6 * math.exp(-0.3 * layer)
    lam = (jnp.exp(jnp.sum(od_lambda_q1[0] * od_lambda_k1[0]))
           - jnp.exp(jnp.sum(od_lambda_q2[0] * od_lambda_k2[0])) + lambda_init).reshape(1).astype(F32)
    tq = min(256, S)
    bias_tiles, far_bias = _bias_tiles(rel_bias, tq)

    kv_mem = norm_matmul(mem.reshape(B * M, D), mem_norm_g, wkv_c, tm=512, tn=512)
    kv_mem = kv_mem.reshape(B, M, -1)

    z = norm_matmul(h, norm_mix_g[0], w_in0, tm=1024, tn=640)
    a_out = conformer_conv(z.reshape(B, S, -1), ev_conv_w[0], ev_conv_b[0], ev_ln_g[0], ev_ln_b[0], ts=256)
    q, k, v = mla_proj(z, ev_q_norm_g[0], ev_kv_norm_g[0], wqa, wqb, wk, wv, ct, st, seq=S, tm=512)
    b_out = mla_attention(q.reshape(B, S, -1), k.reshape(B, S, -1), v.reshape(B, S, -1), tq=tq)
    h = matmul_res(h, [a_out.reshape(T, AW), b_out.reshape(T, H * MLA_V)], ev_w_out[0], tm=1024, tn=1024)
    h = cross_attention(h.reshape(B, S, D), norm_cross_g[0], wq_c[0], kv_mem, wo_c[0], kv_blk=0, tm=512)
    h = dense_ffn(h.reshape(T, D), norm_ffn_g[0], ev_ffn_wg[0], ev_ffn_wu[0], ev_ffn_wd[0], tm=1024, tf=256)

    qkv = norm_matmul(h, norm_mix_g[1], od_w_in[0], tm=1024, tn=1024, scaled_cols=D, scale=d_scale)
    o = diff_attention(qkv.reshape(B, S, -1), lam, far_bias, bias_tiles, od_subln_g[0], tq=tq,
                       out_scale=1.0 - lambda_init)
    h = matmul_res(h, [o.reshape(T, D)], od_w_out[0], tm=1024, tn=1024)
    h = cross_attention(h.reshape(B, S, D), norm_cross_g[1], wq_c[1], kv_mem, wo_c[1], kv_blk=1, tm=512)
    h = h.reshape(T, D)

    tm_moe, sub_moe = 1024, 512
    xn, route = moe_router(h, norm_ffn_g[1], od_router[0], tm=512)
    slot_tok, dest, block_e, block_nsub, nused = _dispatch(route, tm_moe, sub_moe)
    ys = moe_experts(xn, slot_tok, block_e, block_nsub, nused, od_moe_wg[0], od_moe_wu[0], od_moe_wd[0],
                     tm=tm_moe, sub=sub_moe, tf=256)
    out = moe_combine_norm(h, route, ys, dest, final_norm_g, tm=256)
    return out.reshape(B, S, D)
```

```python
import functools
import math

import jax
import jax.numpy as jnp
from jax import lax
from jax.experimental import pallas as pl
from jax.experimental.pallas import tpu as pltpu

F32 = jnp.float32
BF16 = jnp.bfloat16

EPS = 1e-6
NEG = -1e30
CHUNK = 64

CONV_WIDTH = 31
CONV_HALO = 32
MLA_HEADS = 8
MLA_NOPE = 128
MLA_ROPE = 64
MLA_V = 128
MLA_RANK = 512
MLA_QK_PAD = 256
ROPE_THETA = 10000.0
DIFF_HEADS = 8
DIFF_HD = 128
REL_BUCKETS = 32
REL_MAX_DIST = 128
CROSS_HEADS = 4
CROSS_HD = 128
N_EXPERTS = 8
LANE = 128
SUBLANE = 8

VMEM_LIMIT = 56 * 1024 * 1024

NT_DIMS = (((1,), (1,)), ((), ()))


def _cparams(*sem):
    return pltpu.CompilerParams(dimension_semantics=sem, vmem_limit_bytes=VMEM_LIMIT)


def _rms(x, g):
    return x * lax.rsqrt(jnp.mean(x * x, axis=-1, keepdims=True) + EPS) * g


def _dot(a, b):
    return jnp.dot(a, b, preferred_element_type=F32)


W_SPLIT = 1


def _load_bf16(w_ref):
    w = w_ref[0] if len(w_ref.shape) == 3 else w_ref[...]
    return w.astype(BF16)


def _dot_ksplit(x, w_refs):
    kc = w_refs[0].shape[-2]
    acc = None
    for c, w_ref in enumerate(w_refs):
        part = _dot(x[:, c * kc:(c + 1) * kc], _load_bf16(w_ref))
        acc = part if acc is None else acc + part
    return acc


def _split_specs(block, index_map, axis, n=None):
    n = n or W_SPLIT
    shape = list(block)
    assert shape[axis] % n == 0
    shape[axis] //= n

    def chunk_map(c):
        def im(*args):
            idx = list(index_map(*args))
            idx[axis] = idx[axis] * n + c
            return tuple(idx)
        return im

    return [pl.BlockSpec(tuple(shape), chunk_map(c)) for c in range(n)]


def _norm_matmul_kernel(x_ref, g_ref, *refs, scaled_tiles, scale):
    *w_refs, o_ref, xn_ref = refs
    j = pl.program_id(1)

    @pl.when(j == 0)
    def _():
        xn_ref[...] = _rms(x_ref[...], g_ref[...]).astype(BF16)

    acc = _dot_ksplit(xn_ref[...], w_refs)
    if scaled_tiles:
        acc = acc * jnp.where(j < scaled_tiles, scale, 1.0)
    o_ref[...] = acc.astype(o_ref.dtype)


def norm_matmul(x, g, w, *, tm, tn, scaled_cols=0, scale=1.0):
    M, K = x.shape
    N = w.shape[1]
    tm = min(tm, M)
    assert scaled_cols % tn == 0
    w_specs = _split_specs((K, tn), lambda i, j: (0, j), axis=0)
    return pl.pallas_call(
        functools.partial(_norm_matmul_kernel, scaled_tiles=scaled_cols // tn, scale=scale),
        grid=(M // tm, N // tn),
        in_specs=[
            pl.BlockSpec((tm, K), lambda i, j: (i, 0)),
            pl.BlockSpec((1, K), lambda i, j: (0, 0)),
            *w_specs,
        ],
        out_specs=pl.BlockSpec((tm, tn), lambda i, j: (i, j)),
        out_shape=jax.ShapeDtypeStruct((M, N), BF16),
        scratch_shapes=[pltpu.VMEM((tm, K), BF16)],
        compiler_params=_cparams("parallel", "arbitrary"),
        name="norm_matmul",
    )(x, g.reshape(1, K), *([w] * len(w_specs)))


def _matmul_res_kernel(res_ref, *refs, n):
    a_refs, w_refs, o_ref = refs[:n], refs[n:-1], refs[-1]
    per_a = len(w_refs) // n
    acc = res_ref[...]
    for k in range(n):
        acc = acc + _dot_ksplit(a_refs[k][...], w_refs[k * per_a:(k + 1) * per_a])
    o_ref[...] = acc


def matmul_res(res, a_list, w, *, tm, tn):
    M, N = res.shape
    K = w.shape[0]
    tm = min(tm, M)
    n = len(a_list)
    ka = a_list[0].shape[1]
    assert all(a.shape[1] == ka for a in a_list) and K == n * ka
    w_specs = _split_specs((K, tn), lambda i, j: (0, j), axis=0, n=n * W_SPLIT)
    in_specs = [pl.BlockSpec((tm, tn), lambda i, j: (i, j))]
    in_specs += [pl.BlockSpec((tm, ka), lambda i, j: (i, 0)) for _ in a_list]
    return pl.pallas_call(
        functools.partial(_matmul_res_kernel, n=n),
        grid=(M // tm, N // tn),
        in_specs=in_specs + w_specs,
        out_specs=pl.BlockSpec((tm, tn), lambda i, j: (i, j)),
        out_shape=jax.ShapeDtypeStruct((M, N), F32),
        compiler_params=_cparams("parallel", "arbitrary"),
        name="matmul_res",
    )(res, *a_list, *([w] * len(w_specs)))


def _conv_kernel(val_ref, gate_ref, w_ref, b_ref, lg_ref, lb_ref, o_ref, ubuf, shifted, *, ts):
    s = pl.program_id(1)

    @pl.when(s == 0)
    def _():
        ubuf[0:CONV_HALO, :] = jnp.zeros((CONV_HALO, ubuf.shape[1]), F32)

    @pl.when(s > 0)
    def _():
        ubuf[0:CONV_HALO, :] = ubuf[ts:ts + CONV_HALO, :]

    val = val_ref[0].astype(F32)
    gate = gate_ref[0].astype(F32)
    ubuf[CONV_HALO:CONV_HALO + ts, :] = val * jax.nn.sigmoid(gate)

    span = ts + CONV_HALO - SUBLANE
    for k in range(1, SUBLANE):
        shifted[k - 1] = ubuf[k:k + span, :]

    base = CONV_HALO - (CONV_WIDTH - 1)
    acc = jnp.zeros((ts, ubuf.shape[1]), F32) + b_ref[...]
    for j in range(CONV_WIDTH):
        k = (base + j) % SUBLANE
        a = base + j - k
        window = ubuf[a:a + ts, :] if k == 0 else shifted[k - 1, a:a + ts, :]
        acc = acc + w_ref[j:j + 1, :] * window

    mu = jnp.mean(acc, axis=-1, keepdims=True)
    xc = acc - mu
    y = xc * lax.rsqrt(jnp.mean(xc * xc, axis=-1, keepdims=True) + EPS)
    y = y * lg_ref[...] + lb_ref[...]
    o_ref[0] = (y * jax.nn.sigmoid(y)).astype(o_ref.dtype)


def conformer_conv(z, conv_w, conv_b, ln_g, ln_b, *, ts):
    B, S, _ = z.shape
    C = conv_w.shape[1]
    ts = min(ts, S)
    wpad = jnp.zeros((CONV_HALO, C), F32).at[:CONV_WIDTH].set(conv_w)
    row = lambda v: v.reshape(1, C)
    const = lambda shape: pl.BlockSpec(shape, lambda b, s: (0, 0))
    return pl.pallas_call(
        functools.partial(_conv_kernel, ts=ts),
        grid=(B, S // ts),
        in_specs=[
            pl.BlockSpec((1, ts, C), lambda b, s: (b, s, 0)),
            pl.BlockSpec((1, ts, C), lambda b, s: (b, s, 1)),
            const((CONV_HALO, C)), const((1, C)), const((1, C)), const((1, C)),
        ],
        out_specs=pl.BlockSpec((1, ts, C), lambda b, s: (b, s, 0)),
        out_shape=jax.ShapeDtypeStruct((B, S, C), BF16),
        scratch_shapes=[pltpu.VMEM((ts + CONV_HALO, C), F32),
                        pltpu.VMEM((SUBLANE - 1, ts + CONV_HALO - SUBLANE, C), F32)],
        compiler_params=_cparams("parallel", "arbitrary"),
        name="conformer_conv",
    )(z, z, wpad, row(conv_b), row(ln_g), row(ln_b))


def _mla_proj_kernel(cq_ref, ckv_ref, kr_ref, qg_ref, kvg_ref, wqa_ref, wqb_ref, wk_ref, wv_ref,
                     ct_ref, st_ref, q_ref, k_ref, v_ref):
    cqn = _rms(cq_ref[...].astype(F32), qg_ref[...]).astype(BF16)
    ckvn = _rms(ckv_ref[...].astype(F32), kvg_ref[...]).astype(BF16)
    ct = ct_ref[...]
    st = st_ref[...]

    qa = _dot(cqn, wqa_ref[...])
    qb = _dot(cqn, wqb_ref[...])
    kn = _dot(ckvn, wk_ref[...])
    v_ref[...] = _dot(ckvn, wv_ref[...]).astype(BF16)

    kr = kr_ref[...].astype(F32)
    k_rope = (kr * ct + pltpu.roll(kr, LANE // 2, axis=1) * st).astype(BF16)

    for h in range(MLA_HEADS):
        o = h * MLA_QK_PAD
        q_ref[:, o:o + LANE] = qa[:, o:o + LANE].astype(BF16)
        q_ref[:, o + LANE:o + 2 * LANE] = (
            qa[:, o + LANE:o + 2 * LANE] * ct + qb[:, h * LANE:(h + 1) * LANE] * st).astype(BF16)
        k_ref[:, o:o + LANE] = kn[:, h * LANE:(h + 1) * LANE].astype(BF16)
        k_ref[:, o + LANE:o + 2 * LANE] = k_rope


def mla_proj(z, q_norm_g, kv_norm_g, wqa, wqb, wk, wv, ct, st, *, seq, tm):
    T = z.shape[0]
    tm = min(tm, seq)
    n_s = seq // tm
    R = MLA_RANK
    cq_blk = (2 * 1024) // R
    kr_blk = (2 * 1024 + 2 * R) // LANE
    const = lambda shape: pl.BlockSpec(shape, lambda i: (0, 0))
    HQ = MLA_HEADS * MLA_QK_PAD
    HV = MLA_HEADS * MLA_V
    return pl.pallas_call(
        _mla_proj_kernel,
        grid=(T // tm,),
        in_specs=[
            pl.BlockSpec((tm, R), lambda i: (i, cq_blk)),
            pl.BlockSpec((tm, R), lambda i: (i, cq_blk + 1)),
            pl.BlockSpec((tm, LANE), lambda i: (i, kr_blk)),
            const((1, R)), const((1, R)),
            const((R, HQ)), const((R, MLA_HEADS * LANE)), const((R, HV)), const((R, HV)),
            pl.BlockSpec((tm, LANE), lambda i: (i % n_s, 0)),
            pl.BlockSpec((tm, LANE), lambda i: (i % n_s, 0)),
        ],
        out_specs=[
            pl.BlockSpec((tm, HQ), lambda i: (i, 0)),
            pl.BlockSpec((tm, HQ), lambda i: (i, 0)),
            pl.BlockSpec((tm, HV), lambda i: (i, 0)),
        ],
        out_shape=[
            jax.ShapeDtypeStruct((T, HQ), BF16),
            jax.ShapeDtypeStruct((T, HQ), BF16),
            jax.ShapeDtypeStruct((T, HV), BF16),
        ],
        compiler_params=_cparams("parallel"),
        name="mla_proj",
    )(z, z, z, q_norm_g.reshape(1, R), kv_norm_g.reshape(1, R), wqa, wqb, wk, wv, ct, st)


def _softmax_pv(s, v):
    m = jnp.max(s, axis=-1, keepdims=True)
    p = jnp.exp(s - m)
    l = jnp.sum(p, axis=-1, keepdims=True)
    return _dot(p.astype(BF16), v) / l


def _mla_attn_kernel(q_ref, k_ref, v_ref, o_ref, *, tq, n_tiles):
    qc = lax.broadcasted_iota(jnp.int32, (tq, tq), 0) // CHUNK
    kc = lax.broadcasted_iota(jnp.int32, (tq, tq), 1) // CHUNK
    visible = kc <= qc

    for c in range(n_tiles):
        n = (c + 1) * tq
        rows = slice(c * tq, n)
        s = lax.dot_general(q_ref[0, rows, :], k_ref[0, :n, :], NT_DIMS, preferred_element_type=F32)
        parts = [s[:, :c * tq]] if c else []
        parts.append(jnp.where(visible, s[:, c * tq:], NEG))
        s = jnp.concatenate(parts, axis=1) if c else parts[0]
        o_ref[0, rows, :] = _softmax_pv(s, v_ref[0, :n, :]).astype(o_ref.dtype)


def mla_attention(q, k, v, *, tq):
    B, S, _ = q.shape
    tq = min(tq, S)
    head = lambda b, h: (b, 0, h)
    return pl.pallas_call(
        functools.partial(_mla_attn_kernel, tq=tq, n_tiles=S // tq),
        grid=(B, MLA_HEADS),
        in_specs=[
            pl.BlockSpec((1, S, MLA_QK_PAD), head),
            pl.BlockSpec((1, S, MLA_QK_PAD), head),
            pl.BlockSpec((1, S, MLA_V), head),
        ],
        out_specs=pl.BlockSpec((1, S, MLA_V), head),
        out_shape=jax.ShapeDtypeStruct((B, S, MLA_HEADS * MLA_V), BF16),
        compiler_params=_cparams("parallel", "parallel"),
        name="mla_attention",
    )(q, k, v)


def _diff_attn_kernel(lam_ref, far_ref, q_ref, k_ref, v_ref, bias_ref, g_ref, o_ref, *, tq, n_tiles, out_scale):
    far = far_ref[pl.program_id(1)]
    lam = lam_ref[0]

    for c in range(n_tiles):
        n = (c + 1) * tq
        rows = slice(c * tq, n)
        v = v_ref[0, :n, :]

        def half(lo):
            s = lax.dot_general(q_ref[0, rows, lo:lo + DIFF_HD], k_ref[0, :n, lo:lo + DIFF_HD], NT_DIMS,
                                preferred_element_type=F32)
            parts = []
            if c >= 2:
                parts.append(s[:, :(c - 1) * tq] + far)
            if c >= 1:
                parts.append(s[:, (c - 1) * tq:c * tq] + bias_ref[0, 0])
            parts.append(s[:, c * tq:] + bias_ref[0, 1])
            s = jnp.concatenate(parts, axis=1) if c else parts[0]
            return _softmax_pv(s, v)

        o = half(0) - lam * half(DIFF_HD)
        o_ref[0, rows, :] = (_rms(o, g_ref[...]) * out_scale).astype(o_ref.dtype)


def diff_attention(qkv, lam, far_bias, bias_tiles, subln_g, *, tq, out_scale):
    B, S, _ = qkv.shape
    H = DIFF_HEADS
    dv = 2 * DIFF_HD
    smem = pl.BlockSpec(memory_space=pltpu.SMEM)
    return pl.pallas_call(
        functools.partial(_diff_attn_kernel, tq=tq, n_tiles=S // tq, out_scale=out_scale),
        grid=(B, H),
        in_specs=[
            smem, smem,
            pl.BlockSpec((1, S, dv), lambda b, h: (b, 0, h)),
            pl.BlockSpec((1, S, dv), lambda b, h: (b, 0, H + h)),
            pl.BlockSpec((1, S, dv), lambda b, h: (b, 0, 2 * H + h)),
            pl.BlockSpec((1, 2, tq, tq), lambda b, h: (h, 0, 0, 0)),
            pl.BlockSpec((1, dv), lambda b, h: (0, 0)),
        ],
        out_specs=pl.BlockSpec((1, S, dv), lambda b, h: (b, 0, h)),
        out_shape=jax.ShapeDtypeStruct((B, S, H * dv), BF16),
        compiler_params=_cparams("parallel", "parallel"),
        name="diff_attention",
    )(lam, far_bias, qkv, qkv, qkv, bias_tiles, subln_g.reshape(1, dv))


def _cross_kernel(h_ref, g_ref, wq_ref, kv_ref, wo_ref, o_ref):
    h = h_ref[0]
    hn = _rms(h, g_ref[...]).astype(BF16)
    q = _dot(hn, wq_ref[...]).astype(BF16)
    kv = kv_ref[0]
    HD = CROSS_HEADS * CROSS_HD
    outs = []
    for hd in range(CROSS_HEADS):
        lo = hd * CROSS_HD
        s = lax.dot_general(q[:, lo:lo + CROSS_HD], kv[:, lo:lo + CROSS_HD], NT_DIMS,
                            preferred_element_type=F32)
        m = jnp.max(s, axis=-1, keepdims=True)
        p = jnp.exp(s - m)
        l = jnp.sum(p, axis=-1, keepdims=True)
        o = _dot(p.astype(BF16), kv[:, HD + lo:HD + lo + CROSS_HD]) / l
        outs.append(o.astype(BF16))
    o_all = jnp.concatenate(outs, axis=-1)
    o_ref[0] = h + _dot(o_all, wo_ref[...])


def cross_attention(h, g, wq, kv, wo, *, kv_blk, tm):
    B, S, D = h.shape
    M = kv.shape[1]
    HD = CROSS_HEADS * CROSS_HD
    tm = min(tm, S)
    const = lambda shape: pl.BlockSpec(shape, lambda b, s: (0, 0))
    return pl.pallas_call(
        _cross_kernel,
        grid=(B, S // tm),
        in_specs=[
            pl.BlockSpec((1, tm, D), lambda b, s: (b, s, 0)),
            const((1, D)), const((D, HD)),
            pl.BlockSpec((1, M, 2 * HD), lambda b, s: (b, 0, kv_blk)),
            const((HD, D)),
        ],
        out_specs=pl.BlockSpec((1, tm, D), lambda b, s: (b, s, 0)),
        out_shape=jax.ShapeDtypeStruct((B, S, D), F32),
        compiler_params=_cparams("parallel", "arbitrary"),
        name="cross_attention",
    )(h, g.reshape(1, D), wq, kv, wo)


def _swiglu_step(x, wg_refs, wu_refs, wd_refs, o_ref, rows=slice(None)):
    gt = _dot_ksplit(x, wg_refs)
    up = _dot_ksplit(x, wu_refs)
    hm = (gt * jax.nn.sigmoid(gt) * up).astype(BF16)
    nc = wd_refs[0].shape[-1]
    for c, wd_ref in enumerate(wd_refs):
        o_ref[rows, c * nc:(c + 1) * nc] += _dot(hm, _load_bf16(wd_ref))


def _ffn_kernel(h_ref, g_ref, *refs):
    *w_refs, o_ref, xn_ref = refs
    s = len(w_refs) // 3

    @pl.when(pl.program_id(1) == 0)
    def _():
        xn_ref[...] = _rms(h_ref[...], g_ref[...]).astype(BF16)
        o_ref[...] = h_ref[...]

    _swiglu_step(xn_ref[...], w_refs[:s], w_refs[s:2 * s], w_refs[2 * s:], o_ref)


def dense_ffn(h, g, wg, wu, wd, *, tm, tf):
    T, D = h.shape
    F = wg.shape[1]
    tm = min(tm, T)
    up_specs = _split_specs((D, tf), lambda i, f: (0, f), axis=0)
    down_specs = _split_specs((tf, D), lambda i, f: (f, 0), axis=1)
    s = len(up_specs)
    return pl.pallas_call(
        _ffn_kernel,
        grid=(T // tm, F // tf),
        in_specs=[
            pl.BlockSpec((tm, D), lambda i, f: (i, 0), pipeline_mode=pl.Buffered(1)),
            pl.BlockSpec((1, D), lambda i, f: (0, 0)),
            *up_specs, *up_specs, *down_specs,
        ],
        out_specs=pl.BlockSpec((tm, D), lambda i, f: (i, 0)),
        out_shape=jax.ShapeDtypeStruct((T, D), F32),
        scratch_shapes=[pltpu.VMEM((tm, D), BF16)],
        compiler_params=_cparams("parallel", "arbitrary"),
        name="dense_ffn",
    )(h, g.reshape(1, D), *([wg] * s), *([wu] * s), *([wd] * s))


def _router_kernel(h_ref, g_ref, wr_ref, xn_ref, route_ref):
    xn = _rms(h_ref[...], g_ref[...])
    xn_ref[...] = xn
    logits = jnp.dot(xn, wr_ref[...], preferred_element_type=F32, precision=lax.Precision.HIGHEST)
    lane = lax.broadcasted_iota(jnp.int32, logits.shape, 1)
    logits = jnp.where(lane < N_EXPERTS, logits, -jnp.inf)
    v1 = jnp.max(logits, axis=-1, keepdims=True)
    i1 = jnp.min(jnp.where(logits == v1, lane, LANE), axis=-1, keepdims=True)
    rest = jnp.where(lane == i1, -jnp.inf, logits)
    v2 = jnp.max(rest, axis=-1, keepdims=True)
    i2 = jnp.min(jnp.where(rest == v2, lane, LANE), axis=-1, keepdims=True)
    e2 = jnp.exp(v2 - v1)
    g1 = 1.0 / (1.0 + e2)
    g2 = e2 / (1.0 + e2)
    route = jnp.where(lane == 0, i1.astype(F32), 0.0)
    route = jnp.where(lane == 1, i2.astype(F32), route)
    route = jnp.where(lane == 2, g1, route)
    route = jnp.where(lane == 3, g2, route)
    route_ref[...] = route


def moe_router(h, g, w_router, *, tm):
    T, D = h.shape
    tm = min(tm, T)
    wr = jnp.zeros((D, LANE), F32).at[:, :N_EXPERTS].set(w_router)
    return pl.pallas_call(
        _router_kernel,
        grid=(T // tm,),
        in_specs=[
            pl.BlockSpec((tm, D), lambda i: (i, 0)),
            pl.BlockSpec((1, D), lambda i: (0, 0)),
            pl.BlockSpec((D, LANE), lambda i: (0, 0)),
        ],
        out_specs=[pl.BlockSpec((tm, D), lambda i: (i, 0)), pl.BlockSpec((tm, LANE), lambda i: (i, 0))],
        out_shape=[jax.ShapeDtypeStruct((T, D), F32), jax.ShapeDtypeStruct((T, LANE), F32)],
        compiler_params=_cparams("parallel"),
        name="moe_router",
    )(h, g.reshape(1, D), wr)


def _moe_kernel(tok_ref, be_ref, ns_ref, nu_ref, x_hbm, *refs, tm, sub):
    *w_refs, o_ref, xbuf, xb, sem = refs
    s = len(w_refs) // 3
    i = pl.program_id(0)
    f = pl.program_id(1)
    nused = nu_ref[0]

    def row_copy(blk, r):
        tok = tok_ref[blk * tm + r]
        return pltpu.make_async_copy(x_hbm.at[pl.ds(tok, 1)], xbuf.at[pl.ds(r, 1)], sem.at[0])

    def issue(blk):
        def body(r, c):
            row_copy(blk, r).start()
            return c
        lax.fori_loop(0, tm, body, 0, unroll=8)

    def wait(blk):
        def body(r, c):
            row_copy(blk, r).wait()
            return c
        lax.fori_loop(0, tm, body, 0, unroll=8)

    @pl.when(f == 0)
    def _():
        o_ref[...] = jnp.zeros(o_ref.shape, F32)

    @pl.when(i < nused)
    def _():
        @pl.when(f == 0)
        def _():
            @pl.when(i == 0)
            def _():
                issue(0)

            wait(i)
            xb[...] = xbuf[...].astype(BF16)

            @pl.when(i + 1 < nused)
            def _():
                issue(i + 1)

        def sub_block(r):
            rows = pl.ds(r * sub, sub)
            _swiglu_step(xb[rows, :], w_refs[:s], w_refs[s:2 * s], w_refs[2 * s:], o_ref, rows)

        sub_block(0)
        for r in range(1, tm // sub):
            pl.when(r < ns_ref[i])(functools.partial(sub_block, r))


def moe_experts(xn, slot_tok, block_e, block_nsub, nused, wg, wu, wd, *, tm, sub, tf):
    T, D = xn.shape
    P = slot_tok.shape[0]
    F = wg.shape[2]
    nf = F // tf

    def w_cols(i, f, tok, be, ns, nu):
        return be[jnp.minimum(i, nu[0] - 1)], 0, jnp.where(i < nu[0], f, nf - 1)

    def w_rows(i, f, tok, be, ns, nu):
        return be[jnp.minimum(i, nu[0] - 1)], jnp.where(i < nu[0], f, nf - 1), 0

    up_specs = _split_specs((1, D, tf), w_cols, axis=1)
    down_specs = _split_specs((1, tf, D), w_rows, axis=2)
    s = len(up_specs)
    return pl.pallas_call(
        functools.partial(_moe_kernel, tm=tm, sub=sub),
        grid_spec=pltpu.PrefetchScalarGridSpec(
            num_scalar_prefetch=4,
            grid=(P // tm, nf),
            in_specs=[pl.BlockSpec(memory_space=pl.ANY), *up_specs, *up_specs, *down_specs],
            out_specs=pl.BlockSpec((tm, D), lambda i, f, tok, be, ns, nu: (i, 0)),
            scratch_shapes=[
                pltpu.VMEM((tm, D), F32),
                pltpu.VMEM((tm, D), BF16),
                pltpu.SemaphoreType.DMA((1,)),
            ],
        ),
        out_shape=jax.ShapeDtypeStruct((P, D), F32),
        compiler_params=_cparams("arbitrary", "arbitrary"),
        name="moe_experts",
    )(slot_tok, block_e, block_nsub, nused, xn, *([wg] * s), *([wu] * s), *([wd] * s))


def _combine_kernel(pos_ref, h_ref, route_ref, g_ref, ys_hbm, o_ref, buf, sem, *, tm):
    i = pl.program_id(0)

    def row_copy(r, k):
        p = pos_ref[2 * (i * tm + r) + k]
        return pltpu.make_async_copy(ys_hbm.at[pl.ds(p, 1)], buf.at[k, pl.ds(r, 1)], sem.at[k])

    def issue(r, c):
        row_copy(r, 0).start()
        row_copy(r, 1).start()
        return c

    def wait(r, c):
        row_copy(r, 0).wait()
        row_copy(r, 1).wait()
        return c

    lax.fori_loop(0, tm, issue, 0, unroll=8)
    lax.fori_loop(0, tm, wait, 0, unroll=8)
    route = route_ref[...]
    y = route[:, 2:3] * buf[0] + route[:, 3:4] * buf[1]
    o_ref[...] = _rms(h_ref[...] + y, g_ref[...])


def moe_combine_norm(h, route, ys, pos, g, *, tm):
    T, D = h.shape
    tm = min(tm, T)
    return pl.pallas_call(
        functools.partial(_combine_kernel, tm=tm),
        grid_spec=pltpu.PrefetchScalarGridSpec(
            num_scalar_prefetch=1,
            grid=(T // tm,),
            in_specs=[
                pl.BlockSpec((tm, D), lambda i, pos: (i, 0)),
                pl.BlockSpec((tm, LANE), lambda i, pos: (i, 0)),
                pl.BlockSpec((1, D), lambda i, pos: (0, 0)),
                pl.BlockSpec(memory_space=pl.ANY),
            ],
            out_specs=pl.BlockSpec((tm, D), lambda i, pos: (i, 0)),
            scratch_shapes=[pltpu.VMEM((2, tm, D), ys.dtype), pltpu.SemaphoreType.DMA((2,))],
        ),
        out_shape=jax.ShapeDtypeStruct((T, D), F32),
        compiler_params=_cparams("arbitrary"),
        name="moe_combine_norm",
    )(pos, h, route, g.reshape(1, D), ys)


def _dispatch(route, tm, sub):
    T = route.shape[0]
    A = 2 * T
    P = A + N_EXPERTS * tm
    nblk = P // tm
    flat_e = route[:, :2].astype(jnp.int32).reshape(A)
    onehot = (flat_e[:, None] == jnp.arange(N_EXPERTS, dtype=jnp.int32)[None, :]).astype(jnp.int32)
    csum = jnp.cumsum(onehot, axis=0)
    counts = csum[-1]
    rank = jnp.sum(csum * onehot, axis=1) - 1
    padded = ((counts + tm - 1) // tm) * tm
    pend = jnp.cumsum(padded)
    pstart = pend - padded
    dest = (jnp.sum(onehot * pstart[None, :], axis=1) + rank).astype(jnp.int32)
    slot_tok = jnp.zeros((P,), jnp.int32).at[dest].set(jnp.arange(A, dtype=jnp.int32) // 2)
    blk_row0 = jnp.arange(nblk, dtype=jnp.int32) * tm
    block_e = jnp.minimum(jnp.searchsorted(pend, blk_row0, side="right"), N_EXPERTS - 1).astype(jnp.int32)
    live_end = pstart + ((counts + sub - 1) // sub) * sub
    block_nsub = (jnp.clip(live_end[block_e] - blk_row0, 0, tm) // sub).astype(jnp.int32)
    nused = (pend[-1] // tm).astype(jnp.int32).reshape(1)
    return slot_tok, dest, block_e, block_nsub, nused


def _rope_slabs(seq):
    pos = jnp.arange(seq, dtype=F32)
    inv = jnp.power(ROPE_THETA, -jnp.arange(0, MLA_ROPE, 2, dtype=F32) / MLA_ROPE)
    ang = pos[:, None] * inv[None, :]
    z = jnp.zeros((seq, LANE - MLA_ROPE), F32)
    ct = jnp.concatenate([jnp.cos(ang), jnp.cos(ang), z], axis=1)
    st = jnp.concatenate([jnp.sin(ang), jnp.sin(ang), z], axis=1)
    return ct, st


def _t5_bucket(rel):
    half = REL_BUCKETS // 2
    max_exact = half // 2
    ret = (rel > 0).astype(jnp.int32) * half
    n = jnp.abs(rel)
    nf = jnp.maximum(n, 1).astype(F32)
    large = max_exact + (jnp.log(nf / max_exact) / math.log(REL_MAX_DIST / max_exact)
                         * (half - max_exact)).astype(jnp.int32)
    large = jnp.minimum(large, half - 1)
    return ret + jnp.where(n < max_exact, n, large)


def _bias_tiles(rel_bias, tq):
    assert tq >= REL_MAX_DIST
    qi = jnp.arange(tq, dtype=jnp.int32)[:, None]
    ki = jnp.arange(tq, dtype=jnp.int32)[None, :]

    def lookup(rel):
        bucket = _t5_bucket(rel)[None]
        out = jnp.zeros((rel_bias.shape[1],) + rel.shape, F32)
        for b in range(REL_BUCKETS):
            out = jnp.where(bucket == b, rel_bias[b][:, None, None], out)
        return out

    prev = lookup(ki - qi - tq)
    diag = jnp.where(((ki // CHUNK) <= (qi // CHUNK))[None], lookup(ki - qi), NEG)
    far = lookup(jnp.full((1, 1), -2 * tq, jnp.int32))[:, 0, 0]
    return jnp.stack([prev, diag], axis=1), far


def kernel(x, mem, rel_bias, mem_norm_g, norm_mix_g, norm_cross_g, norm_ffn_g, cross_wq, cross_wkv, cross_wo, ev_w_in, ev_conv_w, ev_conv_b, ev_ln_g, ev_ln_b, ev_q_norm_g, ev_w_uq, ev_kv_norm_g, ev_w_ukv, ev_w_out, ev_ffn_wg, ev_ffn_wu, ev_ffn_wd, od_w_in, od_lambda_q1, od_lambda_k1, od_lambda_q2, od_lambda_k2, od_subln_g, od_w_out, od_router, od_moe_wg, od_moe_wu, od_moe_wd, final_norm_g):
    B, S, D = x.shape
    T = B * S
    M = mem.shape[1]
    AW = ev_conv_w.shape[2]
    H = MLA_HEADS
    R = MLA_RANK
    h = x.reshape(T, D)

    w_in0 = ev_w_in[0]
    kr0 = 2 * AW + 2 * R
    half = MLA_ROPE // 2
    w_in0 = jnp.concatenate(
        [w_in0, -w_in0[:, kr0 + half:kr0 + MLA_ROPE], w_in0[:, kr0:kr0 + half]], axis=1).astype(BF16)

    q_scale = (MLA_NOPE + MLA_ROPE) ** -0.5
    wuq = (ev_w_uq[0] * q_scale).reshape(R, H, MLA_NOPE + MLA_ROPE)
    w_nope, w_r1, w_r2 = wuq[..., :MLA_NOPE], wuq[..., MLA_NOPE:MLA_NOPE + half], wuq[..., MLA_NOPE + half:]
    zq = jnp.zeros((R, H, MLA_QK_PAD - MLA_NOPE - MLA_ROPE), F32)
    wqa = jnp.concatenate([w_nope, w_r1, w_r2, zq], axis=-1).reshape(R, H * MLA_QK_PAD).astype(BF16)
    wqb = jnp.concatenate([-w_r2, w_r1, zq], axis=-1).reshape(R, H * LANE).astype(BF16)
    wukv = ev_w_ukv[0].reshape(R, H, MLA_NOPE + MLA_V)
    wk = wukv[..., :MLA_NOPE].reshape(R, H * MLA_NOPE).astype(BF16)
    wv = wukv[..., MLA_NOPE:].reshape(R, H * MLA_V).astype(BF16)
    ct, st = _rope_slabs(S)

    c_scale = CROSS_HD ** -0.5
    wq_c = (cross_wq * c_scale).astype(BF16)
    wkv_c = jnp.concatenate([cross_wkv[0], cross_wkv[1]], axis=1).astype(BF16)
    wo_c = cross_wo.astype(BF16)

    d_scale = DIFF_HD ** -0.5
    layer = 1
    lambda_init = 0.8 - 0.6 * math.exp(-0.3 * layer)
    lam = (jnp.exp(jnp.sum(od_lambda_q1[0] * od_lambda_k1[0]))
           - jnp.exp(jnp.sum(od_lambda_q2[0] * od_lambda_k2[0])) + lambda_init).reshape(1).astype(F32)
    tq = min(256, S)
    bias_tiles, far_bias = _bias_tiles(rel_bias, tq)

    kv_mem = norm_matmul(mem.reshape(B * M, D), mem_norm_g, wkv_c, tm=512, tn=512)
    kv_mem = kv_mem.reshape(B, M, -1)

    z = norm_matmul(h, norm_mix_g[0], w_in0, tm=1024, tn=640)
    a_out = conformer_conv(z.reshape(B, S, -1), ev_conv_w[0], ev_conv_b[0], ev_ln_g[0], ev_ln_b[0], ts=256)
    q, k, v = mla_proj(z, ev_q_norm_g[0], ev_kv_norm_g[0], wqa, wqb, wk, wv, ct, st, seq=S, tm=512)
    b_out = mla_attention(q.reshape(B, S, -1), k.reshape(B, S, -1), v.reshape(B, S, -1), tq=tq)
    h = matmul_res(h, [a_out.reshape(T, AW), b_out.reshape(T, H * MLA_V)], ev_w_out[0], tm=1024, tn=1024)
    h = cross_attention(h.reshape(B, S, D), norm_cross_g[0], wq_c[0], kv_mem, wo_c[0], kv_blk=0, tm=512)
    h = dense_ffn(h.reshape(T, D), norm_ffn_g[0], ev_ffn_wg[0], ev_ffn_wu[0], ev_ffn_wd[0], tm=1024, tf=256)

    qkv = norm_matmul(h, norm_mix_g[1], od_w_in[0], tm=1024, tn=1024, scaled_cols=D, scale=d_scale)
    o = diff_attention(qkv.reshape(B, S, -1), lam, far_bias, bias_tiles, od_subln_g[0], tq=tq,
                       out_scale=1.0 - lambda_init)
    h = matmul_res(h, [o.reshape(T, D)], od_w_out[0], tm=1024, tn=1024)
    h = cross_attention(h.reshape(B, S, D), norm_cross_g[1], wq_c[1], kv_mem, wo_c[1], kv_blk=1, tm=512)
    h = h.reshape(T, D)

    tm_moe, sub_moe = 1024, 512
    xn, route = moe_router(h, norm_ffn_g[1], od_router[0], tm=512)
    slot_tok, dest, block_e, block_nsub, nused = _dispatch(route, tm_moe, sub_moe)
    ys = moe_experts(xn, slot_tok, block_e, block_nsub, nused, od_moe_wg[0], od_moe_wu[0], od_moe_wd[0],
                     tm=tm_moe, sub=sub_moe, tf=256)
    out = moe_combine_norm(h, route, ys, dest, final_norm_g, tm=256)
    return out.reshape(B, S, D)
```

```python
import functools
import math

import jax
import jax.numpy as jnp
from jax import lax
from jax.experimental import pallas as pl
from jax.experimental.pallas import tpu as pltpu

F32 = jnp.float32
BF16 = jnp.bfloat16

EPS = 1e-6
NEG = -1e30
CHUNK = 64

CONV_WIDTH = 31
CONV_HALO = 32
MLA_HEADS = 8
MLA_NOPE = 128
MLA_ROPE = 64
MLA_V = 128
MLA_RANK = 512
MLA_QK_PAD = 256
ROPE_THETA = 10000.0
DIFF_HEADS = 8
DIFF_HD = 128
REL_BUCKETS = 32
REL_MAX_DIST = 128
CROSS_HEADS = 4
CROSS_HD = 128
N_EXPERTS = 8
LANE = 128
SUBLANE = 8

VMEM_LIMIT = 60 * 1024 * 1024

NT_DIMS = (((1,), (1,)), ((), ()))


def _cparams(*sem):
    return pltpu.CompilerParams(dimension_semantics=sem, vmem_limit_bytes=VMEM_LIMIT)


def _rms(x, g):
    return x * lax.rsqrt(jnp.mean(x * x, axis=-1, keepdims=True) + EPS) * g


def _dot(a, b):
    return jnp.dot(a, b, preferred_element_type=F32)


W_SPLIT = 1


def _load_bf16(w_ref):
    w = w_ref[0] if len(w_ref.shape) == 3 else w_ref[...]
    return w.astype(BF16)


def _dot_ksplit(x, w_refs):
    kc = w_refs[0].shape[-2]
    acc = None
    for c, w_ref in enumerate(w_refs):
        part = _dot(x[:, c * kc:(c + 1) * kc], _load_bf16(w_ref))
        acc = part if acc is None else acc + part
    return acc


def _split_specs(block, index_map, axis, n=None):
    n = n or W_SPLIT
    shape = list(block)
    assert shape[axis] % n == 0
    shape[axis] //= n

    def chunk_map(c):
        def im(*args):
            idx = list(index_map(*args))
            idx[axis] = idx[axis] * n + c
            return tuple(idx)
        return im

    return [pl.BlockSpec(tuple(shape), chunk_map(c)) for c in range(n)]


def _norm_matmul_kernel(x_ref, g_ref, w_ref, *refs, scaled_tiles, scale, has_side):
    if has_side:
        ws_ref, o_ref, os_ref, xn_ref = refs
    else:
        o_ref, xn_ref = refs
    j = pl.program_id(1)

    @pl.when(j == 0)
    def _():
        xn_ref[...] = _rms(x_ref[...], g_ref[...]).astype(BF16)
        if has_side:
            os_ref[...] = _dot(xn_ref[...], ws_ref[...]).astype(os_ref.dtype)

    acc = _dot(xn_ref[...], w_ref[...].astype(BF16))
    if scaled_tiles:
        acc = acc * jnp.where(j < scaled_tiles, scale, 1.0)
    o_ref[...] = acc.astype(o_ref.dtype)


def norm_matmul(x, g, w, *, tm, tn, n_cols=None, scaled_cols=0, scale=1.0, side_w=None):
    M, K = x.shape
    N = n_cols or w.shape[1]
    tm = min(tm, M)
    assert scaled_cols % tn == 0 and N % tn == 0
    in_specs = [
        pl.BlockSpec((tm, K), lambda i, j: (i, 0)),
        pl.BlockSpec((1, K), lambda i, j: (0, 0)),
        pl.BlockSpec((K, tn), lambda i, j: (0, j)),
    ]
    out_specs = [pl.BlockSpec((tm, tn), lambda i, j: (i, j))]
    out_shape = [jax.ShapeDtypeStruct((M, N), BF16)]
    args = [x, g.reshape(1, K), w]
    if side_w is not None:
        ns = side_w.shape[1]
        in_specs.append(pl.BlockSpec((K, ns), lambda i, j: (0, 0)))
        out_specs.append(pl.BlockSpec((tm, ns), lambda i, j: (i, 0)))
        out_shape.append(jax.ShapeDtypeStruct((M, ns), BF16))
        args.append(side_w)
    outs = pl.pallas_call(
        functools.partial(_norm_matmul_kernel, scaled_tiles=scaled_cols // tn, scale=scale,
                          has_side=side_w is not None),
        grid=(M // tm, N // tn),
        in_specs=in_specs,
        out_specs=out_specs,
        out_shape=out_shape,
        scratch_shapes=[pltpu.VMEM((tm, K), BF16)],
        compiler_params=_cparams("parallel", "arbitrary"),
        name="norm_matmul",
    )(*args)
    return outs if side_w is not None else outs[0]


def _matmul_res_kernel(res_ref, *refs, n):
    a_refs, w_ref, o_ref, wb_ref = refs[:n], refs[n], refs[n + 1], refs[n + 2]

    @pl.when(pl.program_id(1) == 0)
    def _():
        wb_ref[...] = w_ref[...].astype(BF16)

    ka = a_refs[0].shape[1]
    acc = res_ref[...]
    for k in range(n):
        acc = acc + _dot(a_refs[k][...], wb_ref[k * ka:(k + 1) * ka, :])
    o_ref[...] = acc


def matmul_res(res, a_list, w, *, tm, tn):
    M, N = res.shape
    K = w.shape[0]
    tm = min(tm, M)
    n = len(a_list)
    ka = a_list[0].shape[1]
    assert all(a.shape[1] == ka for a in a_list) and K == n * ka
    in_specs = [pl.BlockSpec((tm, tn), lambda j, i: (i, j))]
    in_specs += [pl.BlockSpec((tm, ka), lambda j, i: (i, 0)) for _ in a_list]
    in_specs += [pl.BlockSpec((K, tn), lambda j, i: (0, j))]
    return pl.pallas_call(
        functools.partial(_matmul_res_kernel, n=n),
        grid=(N // tn, M // tm),
        in_specs=in_specs,
        out_specs=pl.BlockSpec((tm, tn), lambda j, i: (i, j)),
        out_shape=jax.ShapeDtypeStruct((M, N), F32),
        scratch_shapes=[pltpu.VMEM((K, tn), BF16)],
        compiler_params=_cparams("parallel", "arbitrary"),
        name="matmul_res",
    )(res, *a_list, w)


def _conv_kernel(val_ref, gate_ref, w_ref, b_ref, lg_ref, lb_ref, o_ref, ubuf, shifted, *, ts):
    s = pl.program_id(1)

    @pl.when(s == 0)
    def _():
        ubuf[0:CONV_HALO, :] = jnp.zeros((CONV_HALO, ubuf.shape[1]), F32)

    @pl.when(s > 0)
    def _():
        ubuf[0:CONV_HALO, :] = ubuf[ts:ts + CONV_HALO, :]

    val = val_ref[0].astype(F32)
    gate = gate_ref[0].astype(F32)
    ubuf[CONV_HALO:CONV_HALO + ts, :] = val * jax.nn.sigmoid(gate)

    span = ts + CONV_HALO - SUBLANE
    for k in range(1, SUBLANE):
        shifted[k - 1] = ubuf[k:k + span, :]

    base = CONV_HALO - (CONV_WIDTH - 1)
    acc = jnp.zeros((ts, ubuf.shape[1]), F32) + b_ref[...]
    for j in range(CONV_WIDTH):
        k = (base + j) % SUBLANE
        a = base + j - k
        window = ubuf[a:a + ts, :] if k == 0 else shifted[k - 1, a:a + ts, :]
        acc = acc + w_ref[j:j + 1, :] * window

    mu = jnp.mean(acc, axis=-1, keepdims=True)
    xc = acc - mu
    y = xc * lax.rsqrt(jnp.mean(xc * xc, axis=-1, keepdims=True) + EPS)
    y = y * lg_ref[...] + lb_ref[...]
    o_ref[0] = (y * jax.nn.sigmoid(y)).astype(o_ref.dtype)


def conformer_conv(z, conv_w, conv_b, ln_g, ln_b, *, ts):
    B, S, _ = z.shape
    C = conv_w.shape[1]
    ts = min(ts, S)
    wpad = jnp.zeros((CONV_HALO, C), F32).at[:CONV_WIDTH].set(conv_w)
    row = lambda v: v.reshape(1, C)
    const = lambda shape: pl.BlockSpec(shape, lambda b, s: (0, 0))
    return pl.pallas_call(
        functools.partial(_conv_kernel, ts=ts),
        grid=(B, S // ts),
        in_specs=[
            pl.BlockSpec((1, ts, C), lambda b, s: (b, s, 0)),
            pl.BlockSpec((1, ts, C), lambda b, s: (b, s, 1)),
            const((CONV_HALO, C)), const((1, C)), const((1, C)), const((1, C)),
        ],
        out_specs=pl.BlockSpec((1, ts, C), lambda b, s: (b, s, 0)),
        out_shape=jax.ShapeDtypeStruct((B, S, C), BF16),
        scratch_shapes=[pltpu.VMEM((ts + CONV_HALO, C), F32),
                        pltpu.VMEM((SUBLANE - 1, ts + CONV_HALO - SUBLANE, C), F32)],
        compiler_params=_cparams("parallel", "arbitrary"),
        name="conformer_conv",
    )(z, z, wpad, row(conv_b), row(ln_g), row(ln_b))


def _mla_proj_kernel(cq_ref, ckv_ref, kr_ref, qg_ref, kvg_ref, wqa_ref, wqb_ref, wk_ref, wv_ref,
                     ct_ref, st_ref, q_ref, k_ref, v_ref):
    cqn = _rms(cq_ref[...].astype(F32), qg_ref[...]).astype(BF16)
    ckvn = _rms(ckv_ref[...].astype(F32), kvg_ref[...]).astype(BF16)
    ct = ct_ref[...]
    st = st_ref[...]

    qa = _dot(cqn, wqa_ref[...])
    qb = _dot(cqn, wqb_ref[...])
    kn = _dot(ckvn, wk_ref[...])
    v_ref[...] = _dot(ckvn, wv_ref[...]).astype(BF16)

    kr = kr_ref[...].astype(F32)
    k_rope = (kr * ct + pltpu.roll(kr, LANE // 2, axis=1) * st).astype(BF16)

    for h in range(MLA_HEADS):
        o = h * MLA_QK_PAD
        q_ref[:, o:o + LANE] = qa[:, o:o + LANE].astype(BF16)
        q_ref[:, o + LANE:o + 2 * LANE] = (
            qa[:, o + LANE:o + 2 * LANE] * ct + qb[:, h * LANE:(h + 1) * LANE] * st).astype(BF16)
        k_ref[:, o:o + LANE] = kn[:, h * LANE:(h + 1) * LANE].astype(BF16)
        k_ref[:, o + LANE:o + 2 * LANE] = k_rope


def mla_proj(z, z_kr, q_norm_g, kv_norm_g, wqa, wqb, wk, wv, ct, st, *, seq, tm):
    T = z.shape[0]
    tm = min(tm, seq)
    n_s = seq // tm
    R = MLA_RANK
    cq_blk = (2 * 1024) // R
    const = lambda shape: pl.BlockSpec(shape, lambda i: (0, 0))
    HQ = MLA_HEADS * MLA_QK_PAD
    HV = MLA_HEADS * MLA_V
    return pl.pallas_call(
        _mla_proj_kernel,
        grid=(T // tm,),
        in_specs=[
            pl.BlockSpec((tm, R), lambda i: (i, cq_blk)),
            pl.BlockSpec((tm, R), lambda i: (i, cq_blk + 1)),
            pl.BlockSpec((tm, LANE), lambda i: (i, 0)),
            const((1, R)), const((1, R)),
            const((R, HQ)), const((R, MLA_HEADS * LANE)), const((R, HV)), const((R, HV)),
            pl.BlockSpec((tm, LANE), lambda i: (i % n_s, 0)),
            pl.BlockSpec((tm, LANE), lambda i: (i % n_s, 0)),
        ],
        out_specs=[
            pl.BlockSpec((tm, HQ), lambda i: (i, 0)),
            pl.BlockSpec((tm, HQ), lambda i: (i, 0)),
            pl.BlockSpec((tm, HV), lambda i: (i, 0)),
        ],
        out_shape=[
            jax.ShapeDtypeStruct((T, HQ), BF16),
            jax.ShapeDtypeStruct((T, HQ), BF16),
            jax.ShapeDtypeStruct((T, HV), BF16),
        ],
        compiler_params=_cparams("parallel"),
        name="mla_proj",
    )(z, z, z_kr, q_norm_g.reshape(1, R), kv_norm_g.reshape(1, R), wqa, wqb, wk, wv, ct, st)


def _softmax_pv(s, v):
    m = jnp.max(s, axis=-1, keepdims=True)
    p = jnp.exp(s - m)
    l = jnp.sum(p, axis=-1, keepdims=True)
    return _dot(p.astype(BF16), v) / l


def _mla_attn_kernel(q_ref, k_ref, v_ref, o_ref, *, tq, n_tiles):
    qc = lax.broadcasted_iota(jnp.int32, (tq, tq), 0) // CHUNK
    kc = lax.broadcasted_iota(jnp.int32, (tq, tq), 1) // CHUNK
    visible = kc <= qc

    for c in range(n_tiles):
        n = (c + 1) * tq
        rows = slice(c * tq, n)
        s = lax.dot_general(q_ref[0, rows, :], k_ref[0, :n, :], NT_DIMS, preferred_element_type=F32)
        parts = [s[:, :c * tq]] if c else []
        parts.append(jnp.where(visible, s[:, c * tq:], NEG))
        s = jnp.concatenate(parts, axis=1) if c else parts[0]
        o_ref[0, rows, :] = _softmax_pv(s, v_ref[0, :n, :]).astype(o_ref.dtype)


def mla_attention(q, k, v, *, tq):
    B, S, _ = q.shape
    tq = min(tq, S)
    head = lambda b, h: (b, 0, h)
    return pl.pallas_call(
        functools.partial(_mla_attn_kernel, tq=tq, n_tiles=S // tq),
        grid=(B, MLA_HEADS),
        in_specs=[
            pl.BlockSpec((1, S, MLA_QK_PAD), head),
            pl.BlockSpec((1, S, MLA_QK_PAD), head),
            pl.BlockSpec((1, S, MLA_V), head),
        ],
        out_specs=pl.BlockSpec((1, S, MLA_V), head),
        out_shape=jax.ShapeDtypeStruct((B, S, MLA_HEADS * MLA_V), BF16),
        compiler_params=_cparams("parallel", "parallel"),
        name="mla_attention",
    )(q, k, v)


def _diff_attn_kernel(lam_ref, far_ref, q_ref, k_ref, v_ref, bias_ref, g_ref, o_ref, *, tq, n_tiles, out_scale):
    far = far_ref[pl.program_id(1)]
    lam = lam_ref[0]

    for c in range(n_tiles):
        n = (c + 1) * tq
        rows = slice(c * tq, n)
        v = v_ref[0, :n, :]

        def half(lo):
            s = lax.dot_general(q_ref[0, rows, lo:lo + DIFF_HD], k_ref[0, :n, lo:lo + DIFF_HD], NT_DIMS,
                                preferred_element_type=F32)
            parts = []
            if c >= 2:
                parts.append(s[:, :(c - 1) * tq] + far)
            if c >= 1:
                parts.append(s[:, (c - 1) * tq:c * tq] + bias_ref[0, 0])
            parts.append(s[:, c * tq:] + bias_ref[0, 1])
            s = jnp.concatenate(parts, axis=1) if c else parts[0]
            return _softmax_pv(s, v)

        o = half(0) - lam * half(DIFF_HD)
        o_ref[0, rows, :] = (_rms(o, g_ref[...]) * out_scale).astype(o_ref.dtype)


def diff_attention(qkv, lam, far_bias, bias_tiles, subln_g, *, tq, out_scale):
    B, S, _ = qkv.shape
    H = DIFF_HEADS
    dv = 2 * DIFF_HD
    smem = pl.BlockSpec(memory_space=pltpu.SMEM)
    return pl.pallas_call(
        functools.partial(_diff_attn_kernel, tq=tq, n_tiles=S // tq, out_scale=out_scale),
        grid=(B, H),
        in_specs=[
            smem, smem,
            pl.BlockSpec((1, S, dv), lambda b, h: (b, 0, h)),
            pl.BlockSpec((1, S, dv), lambda b, h: (b, 0, H + h)),
            pl.BlockSpec((1, S, dv), lambda b, h: (b, 0, 2 * H + h)),
            pl.BlockSpec((1, 2, tq, tq), lambda b, h: (h, 0, 0, 0)),
            pl.BlockSpec((1, dv), lambda b, h: (0, 0)),
        ],
        out_specs=pl.BlockSpec((1, S, dv), lambda b, h: (b, 0, h)),
        out_shape=jax.ShapeDtypeStruct((B, S, H * dv), BF16),
        compiler_params=_cparams("parallel", "parallel"),
        name="diff_attention",
    )(lam, far_bias, qkv, qkv, qkv, bias_tiles, subln_g.reshape(1, dv))


def _cross_kernel(h_ref, g_ref, wq_ref, kv_ref, wo_ref, o_ref):
    h = h_ref[0]
    hn = _rms(h, g_ref[...]).astype(BF16)
    q = _dot(hn, wq_ref[...]).astype(BF16)
    kv = kv_ref[0]
    HD = CROSS_HEADS * CROSS_HD
    outs = []
    for hd in range(CROSS_HEADS):
        lo = hd * CROSS_HD
        s = lax.dot_general(q[:, lo:lo + CROSS_HD], kv[:, lo:lo + CROSS_HD], NT_DIMS,
                            preferred_element_type=F32)
        m = jnp.max(s, axis=-1, keepdims=True)
        p = jnp.exp(s - m)
        l = jnp.sum(p, axis=-1, keepdims=True)
        o = _dot(p.astype(BF16), kv[:, HD + lo:HD + lo + CROSS_HD]) / l
        outs.append(o.astype(BF16))
    o_all = jnp.concatenate(outs, axis=-1)
    o_ref[0] = h + _dot(o_all, wo_ref[...])


def cross_attention(h, g, wq, kv, wo, *, kv_blk, tm):
    B, S, D = h.shape
    M = kv.shape[1]
    HD = CROSS_HEADS * CROSS_HD
    tm = min(tm, S)
    const = lambda shape: pl.BlockSpec(shape, lambda b, s: (0, 0))
    return pl.pallas_call(
        _cross_kernel,
        grid=(B, S // tm),
        in_specs=[
            pl.BlockSpec((1, tm, D), lambda b, s: (b, s, 0)),
            const((1, D)), const((D, HD)),
            pl.BlockSpec((1, M, 2 * HD), lambda b, s: (b, 0, kv_blk)),
            const((HD, D)),
        ],
        out_specs=pl.BlockSpec((1, tm, D), lambda b, s: (b, s, 0)),
        out_shape=jax.ShapeDtypeStruct((B, S, D), F32),
        compiler_params=_cparams("parallel", "arbitrary"),
        name="cross_attention",
    )(h, g.reshape(1, D), wq, kv, wo)


def _swiglu_step(x, wg_refs, wu_refs, wd_refs, o_ref, rows=slice(None)):
    gt = _dot_ksplit(x, wg_refs)
    up = _dot_ksplit(x, wu_refs)
    hm = (gt * jax.nn.sigmoid(gt) * up).astype(BF16)
    nc = wd_refs[0].shape[-1]
    for c, wd_ref in enumerate(wd_refs):
        o_ref[rows, c * nc:(c + 1) * nc] += _dot(hm, _load_bf16(wd_ref))


def _ffn_kernel(h_ref, g_ref, *refs):
    *w_refs, o_ref, xn_ref = refs
    s = len(w_refs) // 3

    @pl.when(pl.program_id(1) == 0)
    def _():
        xn_ref[...] = _rms(h_ref[...], g_ref[...]).astype(BF16)
        o_ref[...] = h_ref[...]

    _swiglu_step(xn_ref[...], w_refs[:s], w_refs[s:2 * s], w_refs[2 * s:], o_ref)


def dense_ffn(h, g, wg, wu, wd, *, tm, tf):
    T, D = h.shape
    F = wg.shape[1]
    tm = min(tm, T)
    up_specs = _split_specs((D, tf), lambda i, f: (0, f), axis=0)
    down_specs = _split_specs((tf, D), lambda i, f: (f, 0), axis=1)
    s = len(up_specs)
    return pl.pallas_call(
        _ffn_kernel,
        grid=(T // tm, F // tf),
        in_specs=[
            pl.BlockSpec((tm, D), lambda i, f: (i, 0)),
            pl.BlockSpec((1, D), lambda i, f: (0, 0)),
            *up_specs, *up_specs, *down_specs,
        ],
        out_specs=pl.BlockSpec((tm, D), lambda i, f: (i, 0)),
        out_shape=jax.ShapeDtypeStruct((T, D), F32),
        scratch_shapes=[pltpu.VMEM((tm, D), BF16)],
        compiler_params=_cparams("parallel", "arbitrary"),
        name="dense_ffn",
    )(h, g.reshape(1, D), *([wg] * s), *([wu] * s), *([wd] * s))


def _router_kernel(h_ref, g_ref, wr_ref, xn_ref, route_ref):
    xn = _rms(h_ref[...], g_ref[...])
    xn_ref[...] = xn
    logits = jnp.dot(xn, wr_ref[...], preferred_element_type=F32, precision=lax.Precision.HIGHEST)
    lane = lax.broadcasted_iota(jnp.int32, logits.shape, 1)
    logits = jnp.where(lane < N_EXPERTS, logits, -jnp.inf)
    v1 = jnp.max(logits, axis=-1, keepdims=True)
    i1 = jnp.min(jnp.where(logits == v1, lane, LANE), axis=-1, keepdims=True)
    rest = jnp.where(lane == i1, -jnp.inf, logits)
    v2 = jnp.max(rest, axis=-1, keepdims=True)
    i2 = jnp.min(jnp.where(rest == v2, lane, LANE), axis=-1, keepdims=True)
    e2 = jnp.exp(v2 - v1)
    g1 = 1.0 / (1.0 + e2)
    g2 = e2 / (1.0 + e2)
    route = jnp.where(lane == 0, i1.astype(F32), 0.0)
    route = jnp.where(lane == 1, i2.astype(F32), route)
    route = jnp.where(lane == 2, g1, route)
    route = jnp.where(lane == 3, g2, route)
    route_ref[...] = route


def moe_router(h, g, w_router, *, tm):
    T, D = h.shape
    tm = min(tm, T)
    wr = jnp.zeros((D, LANE), F32).at[:, :N_EXPERTS].set(w_router)
    return pl.pallas_call(
        _router_kernel,
        grid=(T // tm,),
        in_specs=[
            pl.BlockSpec((tm, D), lambda i: (i, 0)),
            pl.BlockSpec((1, D), lambda i: (0, 0)),
            pl.BlockSpec((D, LANE), lambda i: (0, 0)),
        ],
        out_specs=[pl.BlockSpec((tm, D), lambda i: (i, 0)), pl.BlockSpec((tm, LANE), lambda i: (i, 0))],
        out_shape=[jax.ShapeDtypeStruct((T, D), F32), jax.ShapeDtypeStruct((T, LANE), F32)],
        compiler_params=_cparams("parallel"),
        name="moe_router",
    )(h, g.reshape(1, D), wr)


def _moe_kernel(tok_ref, be_ref, ns_ref, nu_ref, x_hbm, *refs, tm, sub, rows_per_step, nf):
    *w_refs, o_ref, xbuf, xb, sem = refs
    s = len(w_refs) // 3
    i = pl.program_id(0)
    f = pl.program_id(1)
    nused = nu_ref[0]
    n_rows = rows_per_step * nf

    def row_copy(blk, r):
        tok = tok_ref[blk * tm + r]
        return pltpu.make_async_copy(x_hbm.at[pl.ds(tok, 1)], xbuf.at[pl.ds(r, 1)], sem.at[0])

    def for_rows(fn):
        def body(r, c):
            fn(r)
            return c
        lax.fori_loop(0, n_rows, body, 0, unroll=4)

    @pl.when(f == 0)
    def _():
        o_ref[...] = jnp.zeros(o_ref.shape, F32)

        @pl.when(i == 0)
        def _():
            for_rows(lambda r: row_copy(0, r).start())

        @pl.when(i <= nused)
        def _():
            for_rows(lambda r: row_copy(i, r).wait())

    @pl.when(i < nused)
    def _():
        @pl.when(f == 0)
        def _():
            xb[...] = xbuf[0:tm, :].astype(BF16)

        for k in range(rows_per_step):
            row_copy(i + 1, f * rows_per_step + k).start()

        def sub_block(r):
            rows = pl.ds(r * sub, sub)
            _swiglu_step(xb[rows, :], w_refs[:s], w_refs[s:2 * s], w_refs[2 * s:], o_ref, rows)

        sub_block(0)
        for r in range(1, tm // sub):
            pl.when(r < ns_ref[i])(functools.partial(sub_block, r))


def moe_experts(xn, slot_tok, block_e, block_nsub, nused, wg, wu, wd, *, tm, sub, tf):
    T, D = xn.shape
    P = slot_tok.shape[0]
    F = wg.shape[2]
    nf = F // tf
    rows_per_step = -(-tm // nf)
    spare = rows_per_step * nf - tm
    assert (tm + spare) % 4 == 0
    slot_tok = jnp.concatenate([slot_tok, jnp.zeros((spare,), jnp.int32)])
    xbuf_rows = -(-(tm + spare) // SUBLANE) * SUBLANE

    def w_cols(i, f, tok, be, ns, nu):
        return be[jnp.minimum(i, nu[0] - 1)], 0, jnp.where(i < nu[0], f, nf - 1)

    def w_rows(i, f, tok, be, ns, nu):
        return be[jnp.minimum(i, nu[0] - 1)], jnp.where(i < nu[0], f, nf - 1), 0

    up_specs = _split_specs((1, D, tf), w_cols, axis=1)
    down_specs = _split_specs((1, tf, D), w_rows, axis=2)
    s = len(up_specs)
    return pl.pallas_call(
        functools.partial(_moe_kernel, tm=tm, sub=sub, rows_per_step=rows_per_step, nf=nf),
        grid_spec=pltpu.PrefetchScalarGridSpec(
            num_scalar_prefetch=4,
            grid=(P // tm, nf),
            in_specs=[pl.BlockSpec(memory_space=pl.ANY), *up_specs, *up_specs, *down_specs],
            out_specs=pl.BlockSpec((tm, D), lambda i, f, tok, be, ns, nu: (i, 0)),
            scratch_shapes=[
                pltpu.VMEM((xbuf_rows, D), F32),
                pltpu.VMEM((tm, D), BF16),
                pltpu.SemaphoreType.DMA((1,)),
            ],
        ),
        out_shape=jax.ShapeDtypeStruct((P, D), F32),
        compiler_params=_cparams("arbitrary", "arbitrary"),
        name="moe_experts",
    )(slot_tok, block_e, block_nsub, nused, xn, *([wg] * s), *([wu] * s), *([wd] * s))


def _combine_kernel(pos_ref, h_ref, route_ref, g_ref, ys_hbm, o_ref, buf, sem, *, tm):
    i = pl.program_id(0)

    def row_copy(r, k):
        p = pos_ref[2 * (i * tm + r) + k]
        return pltpu.make_async_copy(ys_hbm.at[pl.ds(p, 1)], buf.at[k, pl.ds(r, 1)], sem.at[k])

    def issue(r, c):
        row_copy(r, 0).start()
        row_copy(r, 1).start()
        return c

    def wait(r, c):
        row_copy(r, 0).wait()
        row_copy(r, 1).wait()
        return c

    lax.fori_loop(0, tm, issue, 0, unroll=8)
    lax.fori_loop(0, tm, wait, 0, unroll=8)
    route = route_ref[...]
    y = route[:, 2:3] * buf[0] + route[:, 3:4] * buf[1]
    o_ref[...] = _rms(h_ref[...] + y, g_ref[...])


def moe_combine_norm(h, route, ys, pos, g, *, tm):
    T, D = h.shape
    tm = min(tm, T)
    return pl.pallas_call(
        functools.partial(_combine_kernel, tm=tm),
        grid_spec=pltpu.PrefetchScalarGridSpec(
            num_scalar_prefetch=1,
            grid=(T // tm,),
            in_specs=[
                pl.BlockSpec((tm, D), lambda i, pos: (i, 0)),
                pl.BlockSpec((tm, LANE), lambda i, pos: (i, 0)),
                pl.BlockSpec((1, D), lambda i, pos: (0, 0)),
                pl.BlockSpec(memory_space=pl.ANY),
            ],
            out_specs=pl.BlockSpec((tm, D), lambda i, pos: (i, 0)),
            scratch_shapes=[pltpu.VMEM((2, tm, D), ys.dtype), pltpu.SemaphoreType.DMA((2,))],
        ),
        out_shape=jax.ShapeDtypeStruct((T, D), F32),
        compiler_params=_cparams("arbitrary"),
        name="moe_combine_norm",
    )(pos, h, route, g.reshape(1, D), ys)


def _dispatch(route, tm, sub):
    T = route.shape[0]
    A = 2 * T
    P = A + N_EXPERTS * tm
    nblk = P // tm
    flat_e = route[:, :2].astype(jnp.int32).reshape(A)
    onehot = (flat_e[:, None] == jnp.arange(N_EXPERTS, dtype=jnp.int32)[None, :]).astype(jnp.int32)
    csum = jnp.cumsum(onehot, axis=0)
    counts = csum[-1]
    rank = jnp.sum(csum * onehot, axis=1) - 1
    padded = ((counts + tm - 1) // tm) * tm
    pend = jnp.cumsum(padded)
    pstart = pend - padded
    dest = (jnp.sum(onehot * pstart[None, :], axis=1) + rank).astype(jnp.int32)
    slot_tok = jnp.zeros((P,), jnp.int32).at[dest].set(jnp.arange(A, dtype=jnp.int32) // 2)
    blk_row0 = jnp.arange(nblk, dtype=jnp.int32) * tm
    block_e = jnp.minimum(jnp.searchsorted(pend, blk_row0, side="right"), N_EXPERTS - 1).astype(jnp.int32)
    live_end = pstart + ((counts + sub - 1) // sub) * sub
    block_nsub = (jnp.clip(live_end[block_e] - blk_row0, 0, tm) // sub).astype(jnp.int32)
    nused = (pend[-1] // tm).astype(jnp.int32).reshape(1)
    return slot_tok, dest, block_e, block_nsub, nused


def _rope_slabs(seq):
    pos = jnp.arange(seq, dtype=F32)
    inv = jnp.power(ROPE_THETA, -jnp.arange(0, MLA_ROPE, 2, dtype=F32) / MLA_ROPE)
    ang = pos[:, None] * inv[None, :]
    z = jnp.zeros((seq, LANE - MLA_ROPE), F32)
    ct = jnp.concatenate([jnp.cos(ang), jnp.cos(ang), z], axis=1)
    st = jnp.concatenate([jnp.sin(ang), jnp.sin(ang), z], axis=1)
    return ct, st


def _t5_bucket(rel):
    half = REL_BUCKETS // 2
    max_exact = half // 2
    ret = (rel > 0).astype(jnp.int32) * half
    n = jnp.abs(rel)
    nf = jnp.maximum(n, 1).astype(F32)
    large = max_exact + (jnp.log(nf / max_exact) / math.log(REL_MAX_DIST / max_exact)
                         * (half - max_exact)).astype(jnp.int32)
    large = jnp.minimum(large, half - 1)
    return ret + jnp.where(n < max_exact, n, large)


def _bias_tiles(rel_bias, tq):
    assert tq >= REL_MAX_DIST
    qi = jnp.arange(tq, dtype=jnp.int32)[:, None]
    ki = jnp.arange(tq, dtype=jnp.int32)[None, :]

    def lookup(rel):
        bucket = _t5_bucket(rel)[None]
        out = jnp.zeros((rel_bias.shape[1],) + rel.shape, F32)
        for b in range(REL_BUCKETS):
            out = jnp.where(bucket == b, rel_bias[b][:, None, None], out)
        return out

    prev = lookup(ki - qi - tq)
    diag = jnp.where(((ki // CHUNK) <= (qi // CHUNK))[None], lookup(ki - qi), NEG)
    far = lookup(jnp.full((1, 1), -2 * tq, jnp.int32))[:, 0, 0]
    return jnp.stack([prev, diag], axis=1), far


def kernel(x, mem, rel_bias, mem_norm_g, norm_mix_g, norm_cross_g, norm_ffn_g, cross_wq, cross_wkv, cross_wo, ev_w_in, ev_conv_w, ev_conv_b, ev_ln_g, ev_ln_b, ev_q_norm_g, ev_w_uq, ev_kv_norm_g, ev_w_ukv, ev_w_out, ev_ffn_wg, ev_ffn_wu, ev_ffn_wd, od_w_in, od_lambda_q1, od_lambda_k1, od_lambda_q2, od_lambda_k2, od_subln_g, od_w_out, od_router, od_moe_wg, od_moe_wu, od_moe_wd, final_norm_g):
    B, S, D = x.shape
    T = B * S
    M = mem.shape[1]
    AW = ev_conv_w.shape[2]
    H = MLA_HEADS
    R = MLA_RANK
    h = x.reshape(T, D)

    kr0 = 2 * AW + 2 * R
    half = MLA_ROPE // 2
    w_kr = ev_w_in[0][:, kr0:kr0 + MLA_ROPE]
    w_kr = jnp.concatenate([w_kr, -w_kr[:, half:], w_kr[:, :half]], axis=1).astype(BF16)

    q_scale = (MLA_NOPE + MLA_ROPE) ** -0.5
    wuq = (ev_w_uq[0] * q_scale).reshape(R, H, MLA_NOPE + MLA_ROPE)
    w_nope, w_r1, w_r2 = wuq[..., :MLA_NOPE], wuq[..., MLA_NOPE:MLA_NOPE + half], wuq[..., MLA_NOPE + half:]
    zq = jnp.zeros((R, H, MLA_QK_PAD - MLA_NOPE - MLA_ROPE), F32)
    wqa = jnp.concatenate([w_nope, w_r1, w_r2, zq], axis=-1).reshape(R, H * MLA_QK_PAD).astype(BF16)
    wqb = jnp.concatenate([-w_r2, w_r1, zq], axis=-1).reshape(R, H * LANE).astype(BF16)
    wukv = ev_w_ukv[0].reshape(R, H, MLA_NOPE + MLA_V)
    wk = wukv[..., :MLA_NOPE].reshape(R, H * MLA_NOPE).astype(BF16)
    wv = wukv[..., MLA_NOPE:].reshape(R, H * MLA_V).astype(BF16)
    ct, st = _rope_slabs(S)

    c_scale = CROSS_HD ** -0.5
    wq_c = (cross_wq * c_scale).astype(BF16)
    wkv_c = jnp.concatenate([cross_wkv[0], cross_wkv[1]], axis=1).astype(BF16)
    wo_c = cross_wo.astype(BF16)

    d_scale = DIFF_HD ** -0.5
    layer = 1
    lambda_init = 0.8 - 0.6 * math.exp(-0.3 * layer)
    lam = (jnp.exp(jnp.sum(od_lambda_q1[0] * od_lambda_k1[0]))
           - jnp.exp(jnp.sum(od_lambda_q2[0] * od_lambda_k2[0])) + lambda_init).reshape(1).astype(F32)
    tq = min(256, S)
    bias_tiles, far_bias = _bias_tiles(rel_bias, tq)

    kv_mem = norm_matmul(mem.reshape(B * M, D), mem_norm_g, wkv_c, tm=512, tn=512)
    kv_mem = kv_mem.reshape(B, M, -1)

    z, z_kr = norm_matmul(h, norm_mix_g[0], ev_w_in[0], tm=1024, tn=1024, n_cols=kr0, side_w=w_kr)
    a_out = conformer_conv(z.reshape(B, S, -1), ev_conv_w[0], ev_conv_b[0], ev_ln_g[0], ev_ln_b[0], ts=256)
    q, k, v = mla_proj(z, z_kr, ev_q_norm_g[0], ev_kv_norm_g[0], wqa, wqb, wk, wv, ct, st, seq=S, tm=512)
    b_out = mla_attention(q.reshape(B, S, -1), k.reshape(B, S, -1), v.reshape(B, S, -1), tq=tq)
    h = matmul_res(h, [a_out.reshape(T, AW), b_out.reshape(T, H * MLA_V)], ev_w_out[0], tm=1024, tn=1024)
    h = cross_attention(h.reshape(B, S, D), norm_cross_g[0], wq_c[0], kv_mem, wo_c[0], kv_blk=0, tm=512)
    h = dense_ffn(h.reshape(T, D), norm_ffn_g[0], ev_ffn_wg[0], ev_ffn_wu[0], ev_ffn_wd[0], tm=1024, tf=256)

    qkv = norm_matmul(h, norm_mix_g[1], od_w_in[0], tm=1024, tn=1024, scaled_cols=D, scale=d_scale)
    o = diff_attention(qkv.reshape(B, S, -1), lam, far_bias, bias_tiles, od_subln_g[0], tq=tq,
                       out_scale=1.0 - lambda_init)
    h = matmul_res(h, [o.reshape(T, D)], od_w_out[0], tm=1024, tn=1024)
    h = cross_attention(h.reshape(B, S, D), norm_cross_g[1], wq_c[1], kv_mem, wo_c[1], kv_blk=1, tm=512)
    h = h.reshape(T, D)

    tm_moe, sub_moe = 1024, 512
    xn, route = moe_router(h, norm_ffn_g[1], od_router[0], tm=512)
    slot_tok, dest, block_e, block_nsub, nused = _dispatch(route, tm_moe, sub_moe)
    ys = moe_experts(xn, slot_tok, block_e, block_nsub, nused, od_moe_wg[0], od_moe_wu[0], od_moe_wd[0],
                     tm=tm_moe, sub=sub_moe, tf=256)
    out = moe_combine_norm(h, route, ys, dest, final_norm_g, tm=256)
    return out.reshape(B, S, D)
```

```python
import functools
import math

import jax
import jax.numpy as jnp
from jax import lax
from jax.experimental import pallas as pl
from jax.experimental.pallas import tpu as pltpu

F32 = jnp.float32
BF16 = jnp.bfloat16

EPS = 1e-6
NEG = -1e30
LOG2E = math.log2(math.e)
CHUNK = 64

CONV_WIDTH = 31
CONV_HALO = 32
MLA_HEADS = 8
MLA_NOPE = 128
MLA_ROPE = 64
MLA_V = 128
MLA_RANK = 512
MLA_QK_PAD = 256
ROPE_THETA = 10000.0
DIFF_HEADS = 8
DIFF_HD = 128
REL_BUCKETS = 32
REL_MAX_DIST = 128
CROSS_HEADS = 4
CROSS_HD = 128
N_EXPERTS = 8
LANE = 128
SUBLANE = 8

VMEM_LIMIT = 60 * 1024 * 1024

NT_DIMS = (((1,), (1,)), ((), ()))


def _cparams(*sem):
    return pltpu.CompilerParams(dimension_semantics=sem, vmem_limit_bytes=VMEM_LIMIT)


def _rms(x, g):
    return x * lax.rsqrt(jnp.mean(x * x, axis=-1, keepdims=True) + EPS) * g


def _dot(a, b):
    return jnp.dot(a, b, preferred_element_type=F32)


def _load_bf16(w_ref):
    w = w_ref[0] if len(w_ref.shape) == 3 else w_ref[...]
    return w.astype(BF16)


def _norm_matmul_kernel(x_ref, g_ref, w_ref, *refs, scaled_tiles, scale, has_side):
    if has_side:
        ws_ref, o_ref, os_ref, xn_ref = refs
    else:
        o_ref, xn_ref = refs
    j = pl.program_id(1)

    @pl.when(j == 0)
    def _():
        xn_ref[...] = _rms(x_ref[...], g_ref[...]).astype(BF16)
        if has_side:
            os_ref[...] = _dot(xn_ref[...], ws_ref[...]).astype(os_ref.dtype)

    acc = _dot(xn_ref[...], w_ref[...].astype(BF16))
    if scaled_tiles:
        acc = acc * jnp.where(j < scaled_tiles, scale, 1.0)
    o_ref[...] = acc.astype(o_ref.dtype)


def norm_matmul(x, g, w, *, tm, tn, n_cols=None, scaled_cols=0, scale=1.0, side_w=None):
    M, K = x.shape
    N = n_cols or w.shape[1]
    tm = min(tm, M)
    assert scaled_cols % tn == 0 and N % tn == 0
    in_specs = [
        pl.BlockSpec((tm, K), lambda i, j: (i, 0)),
        pl.BlockSpec((1, K), lambda i, j: (0, 0)),
        pl.BlockSpec((K, tn), lambda i, j: (0, j)),
    ]
    out_specs = [pl.BlockSpec((tm, tn), lambda i, j: (i, j))]
    out_shape = [jax.ShapeDtypeStruct((M, N), BF16)]
    args = [x, g.reshape(1, K), w]
    if side_w is not None:
        ns = side_w.shape[1]
        in_specs.append(pl.BlockSpec((K, ns), lambda i, j: (0, 0)))
        out_specs.append(pl.BlockSpec((tm, ns), lambda i, j: (i, 0)))
        out_shape.append(jax.ShapeDtypeStruct((M, ns), BF16))
        args.append(side_w)
    outs = pl.pallas_call(
        functools.partial(_norm_matmul_kernel, scaled_tiles=scaled_cols // tn, scale=scale,
                          has_side=side_w is not None),
        grid=(M // tm, N // tn),
        in_specs=in_specs,
        out_specs=out_specs,
        out_shape=out_shape,
        scratch_shapes=[pltpu.VMEM((tm, K), BF16)],
        compiler_params=_cparams("parallel", "arbitrary"),
        name="norm_matmul",
    )(*args)
    return outs if side_w is not None else outs[0]


def _matmul_res_kernel(res_ref, *refs, n):
    a_refs, w_ref, o_ref, wb_ref = refs[:n], refs[n], refs[n + 1], refs[n + 2]

    @pl.when(pl.program_id(1) == 0)
    def _():
        wb_ref[...] = w_ref[...].astype(BF16)

    ka = a_refs[0].shape[1]
    acc = res_ref[...]
    for k in range(n):
        acc = acc + _dot(a_refs[k][...], wb_ref[k * ka:(k + 1) * ka, :])
    o_ref[...] = acc


def matmul_res(res, a_list, w, *, tm, tn):
    M, N = res.shape
    K = w.shape[0]
    tm = min(tm, M)
    n = len(a_list)
    ka = a_list[0].shape[1]
    assert all(a.shape[1] == ka for a in a_list) and K == n * ka
    in_specs = [pl.BlockSpec((tm, tn), lambda j, i: (i, j))]
    in_specs += [pl.BlockSpec((tm, ka), lambda j, i: (i, 0)) for _ in a_list]
    in_specs += [pl.BlockSpec((K, tn), lambda j, i: (0, j))]
    return pl.pallas_call(
        functools.partial(_matmul_res_kernel, n=n),
        grid=(N // tn, M // tm),
        in_specs=in_specs,
        out_specs=pl.BlockSpec((tm, tn), lambda j, i: (i, j)),
        out_shape=jax.ShapeDtypeStruct((M, N), F32),
        scratch_shapes=[pltpu.VMEM((K, tn), BF16)],
        compiler_params=_cparams("parallel", "arbitrary"),
        name="matmul_res",
    )(res, *a_list, w)


def _conv_kernel(val_ref, gate_ref, w_ref, b_ref, lg_ref, lb_ref, o_ref, ubuf, shifted, *, ts):
    s = pl.program_id(1)

    @pl.when(s == 0)
    def _():
        ubuf[0:CONV_HALO, :] = jnp.zeros((CONV_HALO, ubuf.shape[1]), F32)

    @pl.when(s > 0)
    def _():
        ubuf[0:CONV_HALO, :] = ubuf[ts:ts + CONV_HALO, :]

    val = val_ref[0].astype(F32)
    gate = gate_ref[0].astype(F32)
    ubuf[CONV_HALO:CONV_HALO + ts, :] = val * jax.nn.sigmoid(gate)

    span = ts + CONV_HALO - SUBLANE
    for k in range(1, SUBLANE):
        shifted[k - 1] = ubuf[k:k + span, :]

    base = CONV_HALO - (CONV_WIDTH - 1)
    acc = jnp.zeros((ts, ubuf.shape[1]), F32) + b_ref[...]
    for j in range(CONV_WIDTH):
        k = (base + j) % SUBLANE
        a = base + j - k
        window = ubuf[a:a + ts, :] if k == 0 else shifted[k - 1, a:a + ts, :]
        acc = acc + w_ref[j:j + 1, :] * window

    mu = jnp.mean(acc, axis=-1, keepdims=True)
    xc = acc - mu
    y = xc * lax.rsqrt(jnp.mean(xc * xc, axis=-1, keepdims=True) + EPS)
    y = y * lg_ref[...] + lb_ref[...]
    o_ref[0] = (y * jax.nn.sigmoid(y)).astype(o_ref.dtype)


def conformer_conv(z, conv_w, conv_b, ln_g, ln_b, *, ts):
    B, S, _ = z.shape
    C = conv_w.shape[1]
    ts = min(ts, S)
    wpad = jnp.zeros((CONV_HALO, C), F32).at[:CONV_WIDTH].set(conv_w)
    row = lambda v: v.reshape(1, C)
    const = lambda shape: pl.BlockSpec(shape, lambda b, s: (0, 0))
    return pl.pallas_call(
        functools.partial(_conv_kernel, ts=ts),
        grid=(B, S // ts),
        in_specs=[
            pl.BlockSpec((1, ts, C), lambda b, s: (b, s, 0)),
            pl.BlockSpec((1, ts, C), lambda b, s: (b, s, 1)),
            const((CONV_HALO, C)), const((1, C)), const((1, C)), const((1, C)),
        ],
        out_specs=pl.BlockSpec((1, ts, C), lambda b, s: (b, s, 0)),
        out_shape=jax.ShapeDtypeStruct((B, S, C), BF16),
        scratch_shapes=[pltpu.VMEM((ts + CONV_HALO, C), F32),
                        pltpu.VMEM((SUBLANE - 1, ts + CONV_HALO - SUBLANE, C), F32)],
        compiler_params=_cparams("parallel", "arbitrary"),
        name="conformer_conv",
    )(z, z, wpad, row(conv_b), row(ln_g), row(ln_b))


def _mla_proj_kernel(cq_ref, ckv_ref, kr_ref, qg_ref, kvg_ref, wqa_ref, wqb_ref, wk_ref, wv_ref,
                     ct_ref, st_ref, q_ref, k_ref, v_ref):
    cqn = _rms(cq_ref[...].astype(F32), qg_ref[...]).astype(BF16)
    ckvn = _rms(ckv_ref[...].astype(F32), kvg_ref[...]).astype(BF16)
    ct = ct_ref[...]
    st = st_ref[...]

    qa = _dot(cqn, wqa_ref[...])
    qb = _dot(cqn, wqb_ref[...])
    kn = _dot(ckvn, wk_ref[...])
    v_ref[...] = _dot(ckvn, wv_ref[...]).astype(BF16)

    kr = kr_ref[...].astype(F32)
    k_rope = (kr * ct + pltpu.roll(kr, LANE // 2, axis=1) * st).astype(BF16)

    for h in range(MLA_HEADS):
        o = h * MLA_QK_PAD
        q_ref[:, o:o + LANE] = qa[:, o:o + LANE].astype(BF16)
        q_ref[:, o + LANE:o + 2 * LANE] = (
            qa[:, o + LANE:o + 2 * LANE] * ct + qb[:, h * LANE:(h + 1) * LANE] * st).astype(BF16)
        k_ref[:, o:o + LANE] = kn[:, h * LANE:(h + 1) * LANE].astype(BF16)
        k_ref[:, o + LANE:o + 2 * LANE] = k_rope


def mla_proj(z, z_kr, q_norm_g, kv_norm_g, wqa, wqb, wk, wv, ct, st, *, seq, tm):
    T = z.shape[0]
    tm = min(tm, seq)
    n_s = seq // tm
    R = MLA_RANK
    cq_blk = (2 * 1024) // R
    const = lambda shape: pl.BlockSpec(shape, lambda i: (0, 0))
    HQ = MLA_HEADS * MLA_QK_PAD
    HV = MLA_HEADS * MLA_V
    return pl.pallas_call(
        _mla_proj_kernel,
        grid=(T // tm,),
        in_specs=[
            pl.BlockSpec((tm, R), lambda i: (i, cq_blk)),
            pl.BlockSpec((tm, R), lambda i: (i, cq_blk + 1)),
            pl.BlockSpec((tm, LANE), lambda i: (i, 0)),
            const((1, R)), const((1, R)),
            const((R, HQ)), const((R, MLA_HEADS * LANE)), const((R, HV)), const((R, HV)),
            pl.BlockSpec((tm, LANE), lambda i: (i % n_s, 0)),
            pl.BlockSpec((tm, LANE), lambda i: (i % n_s, 0)),
        ],
        out_specs=[
            pl.BlockSpec((tm, HQ), lambda i: (i, 0)),
            pl.BlockSpec((tm, HQ), lambda i: (i, 0)),
            pl.BlockSpec((tm, HV), lambda i: (i, 0)),
        ],
        out_shape=[
            jax.ShapeDtypeStruct((T, HQ), BF16),
            jax.ShapeDtypeStruct((T, HQ), BF16),
            jax.ShapeDtypeStruct((T, HV), BF16),
        ],
        compiler_params=_cparams("parallel"),
        name="mla_proj",
    )(z, z, z_kr, q_norm_g.reshape(1, R), kv_norm_g.reshape(1, R), wqa, wqb, wk, wv, ct, st)


def _softmax_pv(s, v):
    m = jnp.max(s, axis=-1, keepdims=True)
    p = jnp.exp2(s - m)
    l = jnp.sum(p, axis=-1, keepdims=True)
    return _dot(p.astype(BF16), v) / l


def _mla_attn_kernel(q_ref, k_ref, v_ref, o_ref, *, tq, n_tiles):
    qc = lax.broadcasted_iota(jnp.int32, (tq, tq), 0) // CHUNK
    kc = lax.broadcasted_iota(jnp.int32, (tq, tq), 1) // CHUNK
    visible = kc <= qc

    for c in range(n_tiles):
        n = (c + 1) * tq
        rows = slice(c * tq, n)
        s = lax.dot_general(q_ref[0, rows, :], k_ref[0, :n, :], NT_DIMS, preferred_element_type=F32)
        parts = [s[:, :c * tq]] if c else []
        parts.append(jnp.where(visible, s[:, c * tq:], NEG))
        s = jnp.concatenate(parts, axis=1) if c else parts[0]
        o_ref[0, rows, :] = _softmax_pv(s, v_ref[0, :n, :]).astype(o_ref.dtype)


def mla_attention(q, k, v, *, tq):
    B, S, _ = q.shape
    tq = min(tq, S)
    head = lambda b, h: (b, 0, h)
    return pl.pallas_call(
        functools.partial(_mla_attn_kernel, tq=tq, n_tiles=S // tq),
        grid=(B, MLA_HEADS),
        in_specs=[
            pl.BlockSpec((1, S, MLA_QK_PAD), head),
            pl.BlockSpec((1, S, MLA_QK_PAD), head),
            pl.BlockSpec((1, S, MLA_V), head),
        ],
        out_specs=pl.BlockSpec((1, S, MLA_V), head),
        out_shape=jax.ShapeDtypeStruct((B, S, MLA_HEADS * MLA_V), BF16),
        compiler_params=_cparams("parallel", "parallel"),
        name="mla_attention",
    )(q, k, v)


def _diff_attn_kernel(lam_ref, far_ref, q_ref, k_ref, v_ref, bias_ref, g_ref, o_ref, *, tq, n_tiles, out_scale):
    far = far_ref[pl.program_id(1)]
    lam = lam_ref[0]

    for c in range(n_tiles):
        n = (c + 1) * tq
        rows = slice(c * tq, n)
        v = v_ref[0, :n, :]

        def half(lo):
            s = lax.dot_general(q_ref[0, rows, lo:lo + DIFF_HD], k_ref[0, :n, lo:lo + DIFF_HD], NT_DIMS,
                                preferred_element_type=F32)
            parts = []
            if c >= 2:
                parts.append(s[:, :(c - 1) * tq] + far)
            if c >= 1:
                parts.append(s[:, (c - 1) * tq:c * tq] + bias_ref[0, 0])
            parts.append(s[:, c * tq:] + bias_ref[0, 1])
            s = jnp.concatenate(parts, axis=1) if c else parts[0]
            return _softmax_pv(s, v)

        o = half(0) - lam * half(DIFF_HD)
        o_ref[0, rows, :] = (_rms(o, g_ref[...]) * out_scale).astype(o_ref.dtype)


def diff_attention(qkv, lam, far_bias, bias_tiles, subln_g, *, tq, out_scale):
    B, S, _ = qkv.shape
    H = DIFF_HEADS
    dv = 2 * DIFF_HD
    smem = pl.BlockSpec(memory_space=pltpu.SMEM)
    return pl.pallas_call(
        functools.partial(_diff_attn_kernel, tq=tq, n_tiles=S // tq, out_scale=out_scale),
        grid=(B, H),
        in_specs=[
            smem, smem,
            pl.BlockSpec((1, S, dv), lambda b, h: (b, 0, h)),
            pl.BlockSpec((1, S, dv), lambda b, h: (b, 0, H + h)),
            pl.BlockSpec((1, S, dv), lambda b, h: (b, 0, 2 * H + h)),
            pl.BlockSpec((1, 2, tq, tq), lambda b, h: (h, 0, 0, 0)),
            pl.BlockSpec((1, dv), lambda b, h: (0, 0)),
        ],
        out_specs=pl.BlockSpec((1, S, dv), lambda b, h: (b, 0, h)),
        out_shape=jax.ShapeDtypeStruct((B, S, H * dv), BF16),
        compiler_params=_cparams("parallel", "parallel"),
        name="diff_attention",
    )(lam, far_bias, qkv, qkv, qkv, bias_tiles, subln_g.reshape(1, dv))


def _cross_kernel(h_ref, g_ref, wq_ref, kv_ref, wo_ref, *rest, with_router):
    if with_router:
        rg_ref, wr_ref, o_ref, xn_ref, route_ref = rest
    else:
        (o_ref,) = rest
    h = h_ref[0]
    hn = _rms(h, g_ref[...]).astype(BF16)
    q = _dot(hn, wq_ref[...]).astype(BF16)
    kv = kv_ref[0]
    HD = CROSS_HEADS * CROSS_HD
    outs = []
    for hd in range(CROSS_HEADS):
        lo = hd * CROSS_HD
        s = lax.dot_general(q[:, lo:lo + CROSS_HD], kv[:, lo:lo + CROSS_HD], NT_DIMS,
                            preferred_element_type=F32)
        m = jnp.max(s, axis=-1, keepdims=True)
        p = jnp.exp(s - m)
        l = jnp.sum(p, axis=-1, keepdims=True)
        o = _dot(p.astype(BF16), kv[:, HD + lo:HD + lo + CROSS_HD]) / l
        outs.append(o.astype(BF16))
    o_all = jnp.concatenate(outs, axis=-1)
    h_new = h + _dot(o_all, wo_ref[...])
    o_ref[0] = h_new
    if with_router:
        xn_ref[0], route_ref[0] = _route(h_new, rg_ref[...], wr_ref[...])


def cross_attention(h, g, wq, kv, wo, *, kv_blk, tm, router=None):
    B, S, D = h.shape
    M = kv.shape[1]
    HD = CROSS_HEADS * CROSS_HD
    tm = min(tm, S)
    const = lambda shape: pl.BlockSpec(shape, lambda b, s: (0, 0))
    tile = lambda width: pl.BlockSpec((1, tm, width), lambda b, s: (b, s, 0))
    in_specs = [tile(D), const((1, D)), const((D, HD)),
                pl.BlockSpec((1, M, 2 * HD), lambda b, s: (b, 0, kv_blk)), const((HD, D))]
    args = [h, g.reshape(1, D), wq, kv, wo]
    out_specs = [tile(D)]
    out_shape = [jax.ShapeDtypeStruct((B, S, D), F32)]
    if router is not None:
        rg, w_router = router
        in_specs += [const((1, D)), const((D, LANE))]
        args += [rg.reshape(1, D), jnp.zeros((D, LANE), BF16).at[:, :N_EXPERTS].set(w_router.astype(BF16))]
        out_specs += [tile(D), tile(LANE)]
        out_shape += [jax.ShapeDtypeStruct((B, S, D), F32), jax.ShapeDtypeStruct((B, S, LANE), F32)]
    outs = pl.pallas_call(
        functools.partial(_cross_kernel, with_router=router is not None),
        grid=(B, S // tm),
        in_specs=in_specs,
        out_specs=out_specs,
        out_shape=out_shape,
        compiler_params=_cparams("parallel", "arbitrary"),
        name="cross_attention",
    )(*args)
    return outs if router is not None else outs[0]


def _swiglu_step(x, wg_ref, wu_ref, wd_ref, o_ref, rows=slice(None)):
    gt = _dot(x, _load_bf16(wg_ref))
    up = _dot(x, _load_bf16(wu_ref))
    hm = (gt * jax.nn.sigmoid(gt) * up).astype(BF16)
    o_ref[rows, :] += _dot(hm, _load_bf16(wd_ref))


def _ffn_kernel(h_ref, g_ref, wg_ref, wu_ref, wd_ref, o_ref, xn_ref):
    @pl.when(pl.program_id(1) == 0)
    def _():
        xn_ref[...] = _rms(h_ref[...], g_ref[...]).astype(BF16)
        o_ref[...] = h_ref[...]

    _swiglu_step(xn_ref[...], wg_ref, wu_ref, wd_ref, o_ref)


def dense_ffn(h, g, wg, wu, wd, *, tm, tf):
    T, D = h.shape
    F = wg.shape[1]
    tm = min(tm, T)
    return pl.pallas_call(
        _ffn_kernel,
        grid=(T // tm, F // tf),
        in_specs=[
            pl.BlockSpec((tm, D), lambda i, f: (i, 0), pipeline_mode=pl.Buffered(1)),
            pl.BlockSpec((1, D), lambda i, f: (0, 0)),
            pl.BlockSpec((D, tf), lambda i, f: (0, f)),
            pl.BlockSpec((D, tf), lambda i, f: (0, f)),
            pl.BlockSpec((tf, D), lambda i, f: (f, 0)),
        ],
        out_specs=pl.BlockSpec((tm, D), lambda i, f: (i, 0)),
        out_shape=jax.ShapeDtypeStruct((T, D), F32),
        scratch_shapes=[pltpu.VMEM((tm, D), BF16)],
        compiler_params=_cparams("parallel", "arbitrary"),
        name="dense_ffn",
    )(h, g.reshape(1, D), wg, wu, wd)


def _route(h, g, wr):
    xn = _rms(h, g)
    logits = _dot(xn.astype(BF16), wr)
    lane = lax.broadcasted_iota(jnp.int32, logits.shape, 1)
    logits = jnp.where(lane < N_EXPERTS, logits, -jnp.inf)
    v1 = jnp.max(logits, axis=-1, keepdims=True)
    i1 = jnp.min(jnp.where(logits == v1, lane, LANE), axis=-1, keepdims=True)
    rest = jnp.where(lane == i1, -jnp.inf, logits)
    v2 = jnp.max(rest, axis=-1, keepdims=True)
    i2 = jnp.min(jnp.where(rest == v2, lane, LANE), axis=-1, keepdims=True)
    e2 = jnp.exp(v2 - v1)
    g1 = 1.0 / (1.0 + e2)
    g2 = e2 / (1.0 + e2)
    route = jnp.where(lane == 0, i1.astype(F32), 0.0)
    route = jnp.where(lane == 1, i2.astype(F32), route)
    route = jnp.where(lane == 2, g1, route)
    route = jnp.where(lane == 3, g2, route)
    return xn, route


def _moe_kernel(tok_ref, be_ref, ns_ref, nu_ref, x_hbm, wg_ref, wu_ref, wd_ref, o_ref, xbuf, xb, sem,
                *, tm, sub, rows_per_step, n_steps):
    i = pl.program_id(0)
    step = pl.program_id(1)
    nused = nu_ref[0]
    n_rows = rows_per_step * n_steps

    def row_copy(blk, r):
        tok = tok_ref[blk * tm + r]
        return pltpu.make_async_copy(x_hbm.at[pl.ds(tok, 1)], xbuf.at[pl.ds(r, 1)], sem.at[0])

    def for_rows(fn):
        def body(r, c):
            fn(r)
            return c
        lax.fori_loop(0, n_rows, body, 0, unroll=4)

    @pl.when(step == 0)
    def _():
        o_ref[...] = jnp.zeros(o_ref.shape, F32)

        @pl.when(i == 0)
        def _():
            for_rows(lambda r: row_copy(0, r).start())

        @pl.when(i <= nused)
        def _():
            for_rows(lambda r: row_copy(i, r).wait())

    @pl.when(i < nused)
    def _():
        @pl.when(step == 0)
        def _():
            xb[...] = xbuf[0:tm, :].astype(BF16)

        for j in range(rows_per_step):
            row_copy(i + 1, step * rows_per_step + j).start()

        def sub_block(r):
            rows = pl.ds(r * sub, sub)
            _swiglu_step(xb[rows, :], wg_ref, wu_ref, wd_ref, o_ref, rows)

        sub_block(0)
        for r in range(1, tm // sub):
            pl.when(r < ns_ref[i])(functools.partial(sub_block, r))


def moe_experts(xn, slot_tok, block_e, block_nsub, nused, wg, wu, wd, *, tm, sub, tf):
    T, D = xn.shape
    P = slot_tok.shape[0]
    F = wg.shape[2]
    nf = n_steps = F // tf
    rows_per_step = -(-tm // n_steps)
    spare = rows_per_step * n_steps - tm
    assert (tm + spare) % 4 == 0
    slot_tok = jnp.concatenate([slot_tok, jnp.zeros((spare,), jnp.int32)])
    xbuf_rows = -(-(tm + spare) // SUBLANE) * SUBLANE

    def live(i, f, nu):
        return jnp.minimum(i, nu[0] - 1), jnp.where(i < nu[0], f, nf - 1)

    def up_map(i, f, tok, be, ns, nu):
        ii, ff = live(i, f, nu)
        return be[ii], 0, ff

    def down_map(i, f, tok, be, ns, nu):
        ii, ff = live(i, f, nu)
        return be[ii], ff, 0

    return pl.pallas_call(
        functools.partial(_moe_kernel, tm=tm, sub=sub, rows_per_step=rows_per_step, n_steps=n_steps),
        grid_spec=pltpu.PrefetchScalarGridSpec(
            num_scalar_prefetch=4,
            grid=(P // tm, nf),
            in_specs=[
                pl.BlockSpec(memory_space=pl.ANY),
                pl.BlockSpec((1, D, tf), up_map),
                pl.BlockSpec((1, D, tf), up_map),
                pl.BlockSpec((1, tf, D), down_map),
            ],
            out_specs=pl.BlockSpec((tm, D), lambda i, f, tok, be, ns, nu: (i, 0)),
            scratch_shapes=[
                pltpu.VMEM((xbuf_rows, D), F32),
                pltpu.VMEM((tm, D), BF16),
                pltpu.SemaphoreType.DMA((1,)),
            ],
        ),
        out_shape=jax.ShapeDtypeStruct((P, D), F32),
        compiler_params=_cparams("arbitrary", "arbitrary"),
        name="moe_experts",
    )(slot_tok, block_e, block_nsub, nused, xn, wg, wu, wd)


def _combine_kernel(pos_ref, h_ref, route_ref, g_ref, ys_hbm, o_ref, buf, sem, *, tm):
    i = pl.program_id(0)

    def row_copy(r, k):
        p = pos_ref[2 * (i * tm + r) + k]
        return pltpu.make_async_copy(ys_hbm.at[pl.ds(p, 1)], buf.at[k, pl.ds(r, 1)], sem.at[k])

    def issue(r, c):
        row_copy(r, 0).start()
        row_copy(r, 1).start()
        return c

    def wait(r, c):
        row_copy(r, 0).wait()
        row_copy(r, 1).wait()
        return c

    lax.fori_loop(0, tm, issue, 0, unroll=8)
    lax.fori_loop(0, tm, wait, 0, unroll=8)
    route = route_ref[...]
    y = route[:, 2:3] * buf[0] + route[:, 3:4] * buf[1]
    o_ref[...] = _rms(h_ref[...] + y, g_ref[...])


def moe_combine_norm(h, route, ys, pos, g, *, tm):
    T, D = h.shape
    tm = min(tm, T)
    return pl.pallas_call(
        functools.partial(_combine_kernel, tm=tm),
        grid_spec=pltpu.PrefetchScalarGridSpec(
            num_scalar_prefetch=1,
            grid=(T // tm,),
            in_specs=[
                pl.BlockSpec((tm, D), lambda i, pos: (i, 0)),
                pl.BlockSpec((tm, LANE), lambda i, pos: (i, 0)),
                pl.BlockSpec((1, D), lambda i, pos: (0, 0)),
                pl.BlockSpec(memory_space=pl.ANY),
            ],
            out_specs=pl.BlockSpec((tm, D), lambda i, pos: (i, 0)),
            scratch_shapes=[pltpu.VMEM((2, tm, D), ys.dtype), pltpu.SemaphoreType.DMA((2,))],
        ),
        out_shape=jax.ShapeDtypeStruct((T, D), F32),
        compiler_params=_cparams("arbitrary"),
        name="moe_combine_norm",
    )(pos, h, route, g.reshape(1, D), ys)


def _dispatch(route, tm, sub):
    T = route.shape[0]
    A = 2 * T
    P = A + N_EXPERTS * tm
    nblk = P // tm
    flat_e = route[:, :2].astype(jnp.int32).reshape(A)
    onehot = (flat_e[:, None] == jnp.arange(N_EXPERTS, dtype=jnp.int32)[None, :]).astype(jnp.int32)
    csum = jnp.cumsum(onehot, axis=0)
    counts = csum[-1]
    rank = jnp.sum(csum * onehot, axis=1) - 1
    padded = ((counts + tm - 1) // tm) * tm
    pend = jnp.cumsum(padded)
    pstart = pend - padded
    dest = (jnp.sum(onehot * pstart[None, :], axis=1) + rank).astype(jnp.int32)
    slot_tok = jnp.zeros((P,), jnp.int32).at[dest].set(jnp.arange(A, dtype=jnp.int32) // 2)
    blk_row0 = jnp.arange(nblk, dtype=jnp.int32) * tm
    block_e = jnp.minimum(jnp.searchsorted(pend, blk_row0, side="right"), N_EXPERTS - 1).astype(jnp.int32)
    live_end = pstart + ((counts + sub - 1) // sub) * sub
    block_nsub = (jnp.clip(live_end[block_e] - blk_row0, 0, tm) // sub).astype(jnp.int32)
    nused = (pend[-1] // tm).astype(jnp.int32).reshape(1)
    return slot_tok, dest, block_e, block_nsub, nused


def _rope_slabs(seq):
    pos = jnp.arange(seq, dtype=F32)
    inv = jnp.power(ROPE_THETA, -jnp.arange(0, MLA_ROPE, 2, dtype=F32) / MLA_ROPE)
    ang = pos[:, None] * inv[None, :]
    z = jnp.zeros((seq, LANE - MLA_ROPE), F32)
    ct = jnp.concatenate([jnp.cos(ang), jnp.cos(ang), z], axis=1)
    st = jnp.concatenate([jnp.sin(ang), jnp.sin(ang), z], axis=1)
    return ct, st


def _t5_bucket(rel):
    half = REL_BUCKETS // 2
    max_exact = half // 2
    ret = (rel > 0).astype(jnp.int32) * half
    n = jnp.abs(rel)
    nf = jnp.maximum(n, 1).astype(F32)
    large = max_exact + (jnp.log(nf / max_exact) / math.log(REL_MAX_DIST / max_exact)
                         * (half - max_exact)).astype(jnp.int32)
    large = jnp.minimum(large, half - 1)
    return ret + jnp.where(n < max_exact, n, large)


def _bias_tiles(rel_bias, tq):
    assert tq >= REL_MAX_DIST
    qi = jnp.arange(tq, dtype=jnp.int32)[:, None]
    ki = jnp.arange(tq, dtype=jnp.int32)[None, :]

    def lookup(rel):
        bucket = _t5_bucket(rel)[None]
        out = jnp.zeros((rel_bias.shape[1],) + rel.shape, F32)
        for b in range(REL_BUCKETS):
            out = jnp.where(bucket == b, rel_bias[b][:, None, None], out)
        return out

    prev = lookup(ki - qi - tq)
    diag = jnp.where(((ki // CHUNK) <= (qi // CHUNK))[None], lookup(ki - qi), NEG)
    far = lookup(jnp.full((1, 1), -2 * tq, jnp.int32))[:, 0, 0]
    return jnp.stack([prev, diag], axis=1), far


def kernel(x, mem, rel_bias, mem_norm_g, norm_mix_g, norm_cross_g, norm_ffn_g, cross_wq, cross_wkv, cross_wo, ev_w_in, ev_conv_w, ev_conv_b, ev_ln_g, ev_ln_b, ev_q_norm_g, ev_w_uq, ev_kv_norm_g, ev_w_ukv, ev_w_out, ev_ffn_wg, ev_ffn_wu, ev_ffn_wd, od_w_in, od_lambda_q1, od_lambda_k1, od_lambda_q2, od_lambda_k2, od_subln_g, od_w_out, od_router, od_moe_wg, od_moe_wu, od_moe_wd, final_norm_g):
    B, S, D = x.shape
    T = B * S
    M = mem.shape[1]
    AW = ev_conv_w.shape[2]
    H = MLA_HEADS
    R = MLA_RANK
    h = x.reshape(T, D)

    kr0 = 2 * AW + 2 * R
    half = MLA_ROPE // 2
    w_kr = ev_w_in[0][:, kr0:kr0 + MLA_ROPE]
    w_kr = jnp.concatenate([w_kr, -w_kr[:, half:], w_kr[:, :half]], axis=1).astype(BF16)

    q_scale = (MLA_NOPE + MLA_ROPE) ** -0.5 * LOG2E
    wuq = (ev_w_uq[0] * q_scale).reshape(R, H, MLA_NOPE + MLA_ROPE)
    w_nope, w_r1, w_r2 = wuq[..., :MLA_NOPE], wuq[..., MLA_NOPE:MLA_NOPE + half], wuq[..., MLA_NOPE + half:]
    zq = jnp.zeros((R, H, MLA_QK_PAD - MLA_NOPE - MLA_ROPE), F32)
    wqa = jnp.concatenate([w_nope, w_r1, w_r2, zq], axis=-1).reshape(R, H * MLA_QK_PAD).astype(BF16)
    wqb = jnp.concatenate([-w_r2, w_r1, zq], axis=-1).reshape(R, H * LANE).astype(BF16)
    wukv = ev_w_ukv[0].reshape(R, H, MLA_NOPE + MLA_V)
    wk = wukv[..., :MLA_NOPE].reshape(R, H * MLA_NOPE).astype(BF16)
    wv = wukv[..., MLA_NOPE:].reshape(R, H * MLA_V).astype(BF16)
    ct, st = _rope_slabs(S)

    c_scale = CROSS_HD ** -0.5
    wq_c = (cross_wq * c_scale).astype(BF16)
    wkv_c = jnp.concatenate([cross_wkv[0], cross_wkv[1]], axis=1).astype(BF16)
    wo_c = cross_wo.astype(BF16)

    d_scale = DIFF_HD ** -0.5 * LOG2E
    layer = 1
    lambda_init = 0.8 - 0.6 * math.exp(-0.3 * layer)
    lam = (jnp.exp(jnp.sum(od_lambda_q1[0] * od_lambda_k1[0]))
           - jnp.exp(jnp.sum(od_lambda_q2[0] * od_lambda_k2[0])) + lambda_init).reshape(1).astype(F32)
    tq = min(256, S)
    bias_tiles, far_bias = _bias_tiles(rel_bias * LOG2E, tq)

    kv_mem = norm_matmul(mem.reshape(B * M, D), mem_norm_g, wkv_c, tm=512, tn=512)
    kv_mem = kv_mem.reshape(B, M, -1)

    z, z_kr = norm_matmul(h, norm_mix_g[0], ev_w_in[0].astype(BF16), tm=1024, tn=1024, n_cols=kr0, side_w=w_kr)
    a_out = conformer_conv(z.reshape(B, S, -1), ev_conv_w[0], ev_conv_b[0], ev_ln_g[0], ev_ln_b[0], ts=256)
    q, k, v = mla_proj(z, z_kr, ev_q_norm_g[0], ev_kv_norm_g[0], wqa, wqb, wk, wv, ct, st, seq=S, tm=512)
    b_out = mla_attention(q.reshape(B, S, -1), k.reshape(B, S, -1), v.reshape(B, S, -1), tq=tq)
    h = matmul_res(h, [a_out.reshape(T, AW), b_out.reshape(T, H * MLA_V)], ev_w_out[0], tm=1024, tn=1024)
    h = cross_attention(h.reshape(B, S, D), norm_cross_g[0], wq_c[0], kv_mem, wo_c[0], kv_blk=0, tm=512)
    h = dense_ffn(h.reshape(T, D), norm_ffn_g[0], ev_ffn_wg[0].astype(BF16), ev_ffn_wu[0].astype(BF16),
                  ev_ffn_wd[0].astype(BF16), tm=1024, tf=512)

    qkv = norm_matmul(h, norm_mix_g[1], od_w_in[0].astype(BF16), tm=1024, tn=1024, scaled_cols=D, scale=d_scale)
    o = diff_attention(qkv.reshape(B, S, -1), lam, far_bias, bias_tiles, od_subln_g[0], tq=tq,
                       out_scale=1.0 - lambda_init)
    h = matmul_res(h, [o.reshape(T, D)], od_w_out[0], tm=1024, tn=1024)
    h, xn, route = cross_attention(h.reshape(B, S, D), norm_cross_g[1], wq_c[1], kv_mem, wo_c[1], kv_blk=1,
                                   tm=512, router=(norm_ffn_g[1], od_router[0]))
    h, xn, route = h.reshape(T, D), xn.reshape(T, D), route.reshape(T, LANE)

    tm_moe, sub_moe = 1024, 512
    slot_tok, dest, block_e, block_nsub, nused = _dispatch(route, tm_moe, sub_moe)
    ys = moe_experts(xn, slot_tok, block_e, block_nsub, nused, od_moe_wg[0], od_moe_wu[0], od_moe_wd[0],
                     tm=tm_moe, sub=sub_moe, tf=256)
    out = moe_combine_norm(h, route, ys, dest, final_norm_g, tm=256)
    return out.reshape(B, S, D)
```

```python
import functools
import math

import jax
import jax.numpy as jnp
from jax import lax
from jax.experimental import pallas as pl
from jax.experimental.pallas import tpu as pltpu

F32 = jnp.float32
BF16 = jnp.bfloat16

EPS = 1e-6
NEG = -1e30
LOG2E = math.log2(math.e)
CHUNK = 64

CONV_WIDTH = 31
CONV_HALO = 32
MLA_HEADS = 8
MLA_NOPE = 128
MLA_ROPE = 64
MLA_V = 128
MLA_RANK = 512
MLA_QK_PAD = 256
ROPE_THETA = 10000.0
DIFF_HEADS = 8
DIFF_HD = 128
REL_BUCKETS = 32
REL_MAX_DIST = 128
CROSS_HEADS = 4
CROSS_HD = 128
N_EXPERTS = 8
LANE = 128
SUBLANE = 8

VMEM_LIMIT = 60 * 1024 * 1024

NT_DIMS = (((1,), (1,)), ((), ()))


def _cparams(*sem):
    return pltpu.CompilerParams(dimension_semantics=sem, vmem_limit_bytes=VMEM_LIMIT)


def _rms(x, g):
    return x * lax.rsqrt(jnp.mean(x * x, axis=-1, keepdims=True) + EPS) * g


def _dot(a, b):
    return jnp.dot(a, b, preferred_element_type=F32)


def _load_bf16(w_ref):
    w = w_ref[0] if len(w_ref.shape) == 3 else w_ref[...]
    return w.astype(BF16)


def _norm_matmul_kernel(x_ref, g_ref, w_ref, *refs, scaled_tiles, scale, has_side):
    if has_side:
        ws_ref, o_ref, os_ref, xn_ref = refs
    else:
        o_ref, xn_ref = refs
    j = pl.program_id(1)

    @pl.when(j == 0)
    def _():
        xn_ref[...] = _rms(x_ref[...], g_ref[...]).astype(BF16)
        if has_side:
            os_ref[...] = _dot(xn_ref[...], ws_ref[...]).astype(os_ref.dtype)

    acc = _dot(xn_ref[...], w_ref[...].astype(BF16))
    if scaled_tiles:
        acc = acc * jnp.where(j < scaled_tiles, scale, 1.0)
    o_ref[...] = acc.astype(o_ref.dtype)


def norm_matmul(x, g, w, *, tm, tn, n_cols=None, scaled_cols=0, scale=1.0, side_w=None):
    M, K = x.shape
    N = n_cols or w.shape[1]
    tm = min(tm, M)
    assert scaled_cols % tn == 0 and N % tn == 0
    in_specs = [
        pl.BlockSpec((tm, K), lambda i, j: (i, 0)),
        pl.BlockSpec((1, K), lambda i, j: (0, 0)),
        pl.BlockSpec((K, tn), lambda i, j: (0, j)),
    ]
    out_specs = [pl.BlockSpec((tm, tn), lambda i, j: (i, j))]
    out_shape = [jax.ShapeDtypeStruct((M, N), BF16)]
    args = [x, g.reshape(1, K), w]
    if side_w is not None:
        ns = side_w.shape[1]
        in_specs.append(pl.BlockSpec((K, ns), lambda i, j: (0, 0)))
        out_specs.append(pl.BlockSpec((tm, ns), lambda i, j: (i, 0)))
        out_shape.append(jax.ShapeDtypeStruct((M, ns), BF16))
        args.append(side_w)
    outs = pl.pallas_call(
        functools.partial(_norm_matmul_kernel, scaled_tiles=scaled_cols // tn, scale=scale,
                          has_side=side_w is not None),
        grid=(M // tm, N // tn),
        in_specs=in_specs,
        out_specs=out_specs,
        out_shape=out_shape,
        scratch_shapes=[pltpu.VMEM((tm, K), BF16)],
        compiler_params=_cparams("parallel", "arbitrary"),
        name="norm_matmul",
    )(*args)
    return outs if side_w is not None else outs[0]


def _matmul_res_kernel(res_ref, *refs, n):
    a_refs, w_ref, o_ref, wb_ref = refs[:n], refs[n], refs[n + 1], refs[n + 2]

    @pl.when(pl.program_id(1) == 0)
    def _():
        wb_ref[...] = w_ref[...].astype(BF16)

    ka = a_refs[0].shape[1]
    acc = res_ref[...]
    for k in range(n):
        acc = acc + _dot(a_refs[k][...], wb_ref[k * ka:(k + 1) * ka, :])
    o_ref[...] = acc


def matmul_res(res, a_list, w, *, tm, tn):
    M, N = res.shape
    K = w.shape[0]
    tm = min(tm, M)
    n = len(a_list)
    ka = a_list[0].shape[1]
    assert all(a.shape[1] == ka for a in a_list) and K == n * ka
    in_specs = [pl.BlockSpec((tm, tn), lambda j, i: (i, j))]
    in_specs += [pl.BlockSpec((tm, ka), lambda j, i: (i, 0)) for _ in a_list]
    in_specs += [pl.BlockSpec((K, tn), lambda j, i: (0, j))]
    return pl.pallas_call(
        functools.partial(_matmul_res_kernel, n=n),
        grid=(N // tn, M // tm),
        in_specs=in_specs,
        out_specs=pl.BlockSpec((tm, tn), lambda j, i: (i, j)),
        out_shape=jax.ShapeDtypeStruct((M, N), F32),
        scratch_shapes=[pltpu.VMEM((K, tn), BF16)],
        compiler_params=_cparams("parallel", "arbitrary"),
        name="matmul_res",
    )(res, *a_list, w)


def _conv_kernel(val_ref, gate_ref, w_ref, b_ref, lg_ref, lb_ref, o_ref, ubuf, shifted, *, ts):
    s = pl.program_id(1)

    @pl.when(s == 0)
    def _():
        ubuf[0:CONV_HALO, :] = jnp.zeros((CONV_HALO, ubuf.shape[1]), F32)

    @pl.when(s > 0)
    def _():
        ubuf[0:CONV_HALO, :] = ubuf[ts:ts + CONV_HALO, :]

    val = val_ref[0].astype(F32)
    gate = gate_ref[0].astype(F32)
    ubuf[CONV_HALO:CONV_HALO + ts, :] = val * jax.nn.sigmoid(gate)

    span = ts + CONV_HALO - SUBLANE
    for k in range(1, SUBLANE):
        shifted[k - 1] = ubuf[k:k + span, :]

    base = CONV_HALO - (CONV_WIDTH - 1)
    acc = jnp.zeros((ts, ubuf.shape[1]), F32) + b_ref[...]
    for j in range(CONV_WIDTH):
        k = (base + j) % SUBLANE
        a = base + j - k
        window = ubuf[a:a + ts, :] if k == 0 else shifted[k - 1, a:a + ts, :]
        acc = acc + w_ref[j:j + 1, :] * window

    mu = jnp.mean(acc, axis=-1, keepdims=True)
    xc = acc - mu
    y = xc * lax.rsqrt(jnp.mean(xc * xc, axis=-1, keepdims=True) + EPS)
    y = y * lg_ref[...] + lb_ref[...]
    o_ref[0] = (y * jax.nn.sigmoid(y)).astype(o_ref.dtype)


def conformer_conv(z, conv_w, conv_b, ln_g, ln_b, *, ts):
    B, S, _ = z.shape
    C = conv_w.shape[1]
    ts = min(ts, S)
    wpad = jnp.zeros((CONV_HALO, C), F32).at[:CONV_WIDTH].set(conv_w)
    row = lambda v: v.reshape(1, C)
    const = lambda shape: pl.BlockSpec(shape, lambda b, s: (0, 0))
    return pl.pallas_call(
        functools.partial(_conv_kernel, ts=ts),
        grid=(B, S // ts),
        in_specs=[
            pl.BlockSpec((1, ts, C), lambda b, s: (b, s, 0)),
            pl.BlockSpec((1, ts, C), lambda b, s: (b, s, 1)),
            const((CONV_HALO, C)), const((1, C)), const((1, C)), const((1, C)),
        ],
        out_specs=pl.BlockSpec((1, ts, C), lambda b, s: (b, s, 0)),
        out_shape=jax.ShapeDtypeStruct((B, S, C), BF16),
        scratch_shapes=[pltpu.VMEM((ts + CONV_HALO, C), F32),
                        pltpu.VMEM((SUBLANE - 1, ts + CONV_HALO - SUBLANE, C), F32)],
        compiler_params=_cparams("parallel", "arbitrary"),
        name="conformer_conv",
    )(z, z, wpad, row(conv_b), row(ln_g), row(ln_b))


def _mla_proj_kernel(cq_ref, ckv_ref, kr_ref, qg_ref, kvg_ref, wqa_ref, wqb_ref, wk_ref, wv_ref,
                     ct_ref, st_ref, q_ref, k_ref, v_ref):
    cqn = _rms(cq_ref[...].astype(F32), qg_ref[...]).astype(BF16)
    ckvn = _rms(ckv_ref[...].astype(F32), kvg_ref[...]).astype(BF16)
    ct = ct_ref[...]
    st = st_ref[...]

    qa = _dot(cqn, wqa_ref[...])
    qb = _dot(cqn, wqb_ref[...])
    kn = _dot(ckvn, wk_ref[...])
    v_ref[...] = _dot(ckvn, wv_ref[...]).astype(BF16)

    kr = kr_ref[...].astype(F32)
    k_rope = (kr * ct + pltpu.roll(kr, LANE // 2, axis=1) * st).astype(BF16)

    for h in range(MLA_HEADS):
        o = h * MLA_QK_PAD
        q_ref[:, o:o + LANE] = qa[:, o:o + LANE].astype(BF16)
        q_ref[:, o + LANE:o + 2 * LANE] = (
            qa[:, o + LANE:o + 2 * LANE] * ct + qb[:, h * LANE:(h + 1) * LANE] * st).astype(BF16)
        k_ref[:, o:o + LANE] = kn[:, h * LANE:(h + 1) * LANE].astype(BF16)
        k_ref[:, o + LANE:o + 2 * LANE] = k_rope


def mla_proj(z, z_kr, q_norm_g, kv_norm_g, wqa, wqb, wk, wv, ct, st, *, seq, tm):
    T = z.shape[0]
    tm = min(tm, seq)
    n_s = seq // tm
    R = MLA_RANK
    cq_blk = (2 * 1024) // R
    const = lambda shape: pl.BlockSpec(shape, lambda i: (0, 0))
    HQ = MLA_HEADS * MLA_QK_PAD
    HV = MLA_HEADS * MLA_V
    return pl.pallas_call(
        _mla_proj_kernel,
        grid=(T // tm,),
        in_specs=[
            pl.BlockSpec((tm, R), lambda i: (i, cq_blk)),
            pl.BlockSpec((tm, R), lambda i: (i, cq_blk + 1)),
            pl.BlockSpec((tm, LANE), lambda i: (i, 0)),
            const((1, R)), const((1, R)),
            const((R, HQ)), const((R, MLA_HEADS * LANE)), const((R, HV)), const((R, HV)),
            pl.BlockSpec((tm, LANE), lambda i: (i % n_s, 0)),
            pl.BlockSpec((tm, LANE), lambda i: (i % n_s, 0)),
        ],
        out_specs=[
            pl.BlockSpec((tm, HQ), lambda i: (i, 0)),
            pl.BlockSpec((tm, HQ), lambda i: (i, 0)),
            pl.BlockSpec((tm, HV), lambda i: (i, 0)),
        ],
        out_shape=[
            jax.ShapeDtypeStruct((T, HQ), BF16),
            jax.ShapeDtypeStruct((T, HQ), BF16),
            jax.ShapeDtypeStruct((T, HV), BF16),
        ],
        compiler_params=_cparams("parallel"),
        name="mla_proj",
    )(z, z, z_kr, q_norm_g.reshape(1, R), kv_norm_g.reshape(1, R), wqa, wqb, wk, wv, ct, st)


def _softmax_pv(s, v):
    m = jnp.max(s, axis=-1, keepdims=True)
    p = jnp.exp2(s - m)
    l = jnp.sum(p, axis=-1, keepdims=True)
    return _dot(p.astype(BF16), v) / l


def _mla_attn_kernel(q_ref, k_ref, v_ref, o_ref, *, tq, n_tiles):
    qc = lax.broadcasted_iota(jnp.int32, (tq, tq), 0) // CHUNK
    kc = lax.broadcasted_iota(jnp.int32, (tq, tq), 1) // CHUNK
    visible = kc <= qc

    for c in range(n_tiles):
        n = (c + 1) * tq
        rows = slice(c * tq, n)
        s = lax.dot_general(q_ref[0, rows, :], k_ref[0, :n, :], NT_DIMS, preferred_element_type=F32)
        parts = [s[:, :c * tq]] if c else []
        parts.append(jnp.where(visible, s[:, c * tq:], NEG))
        s = jnp.concatenate(parts, axis=1) if c else parts[0]
        o_ref[0, rows, :] = _softmax_pv(s, v_ref[0, :n, :]).astype(o_ref.dtype)


def mla_attention(q, k, v, *, tq):
    B, S, _ = q.shape
    tq = min(tq, S)
    head = lambda b, h: (b, 0, h)
    return pl.pallas_call(
        functools.partial(_mla_attn_kernel, tq=tq, n_tiles=S // tq),
        grid=(B, MLA_HEADS),
        in_specs=[
            pl.BlockSpec((1, S, MLA_QK_PAD), head),
            pl.BlockSpec((1, S, MLA_QK_PAD), head),
            pl.BlockSpec((1, S, MLA_V), head),
        ],
        out_specs=pl.BlockSpec((1, S, MLA_V), head),
        out_shape=jax.ShapeDtypeStruct((B, S, MLA_HEADS * MLA_V), BF16),
        compiler_params=_cparams("parallel", "parallel"),
        name="mla_attention",
    )(q, k, v)


def _diff_attn_kernel(lam_ref, q_ref, k_ref, v_ref, bias_ref, g_ref, o_ref, *, tq, n_tiles, out_scale):
    lam = lam_ref[0]

    for c in range(n_tiles):
        n = (c + 1) * tq
        rows = slice(c * tq, n)
        v = v_ref[0, :n, :]

        def half(lo):
            s = lax.dot_general(q_ref[0, rows, lo:lo + DIFF_HD], k_ref[0, :n, lo:lo + DIFF_HD], NT_DIMS,
                                preferred_element_type=F32)
            parts = []
            if c >= 2:
                parts.append(s[:, :(c - 1) * tq])
            if c >= 1:
                parts.append(s[:, (c - 1) * tq:c * tq] + bias_ref[0, 0])
            parts.append(s[:, c * tq:] + bias_ref[0, 1])
            s = jnp.concatenate(parts, axis=1) if c else parts[0]
            return _softmax_pv(s, v)

        o = half(0) - lam * half(DIFF_HD)
        o_ref[0, rows, :] = (_rms(o, g_ref[...]) * out_scale).astype(o_ref.dtype)


def diff_attention(qkv, lam, bias_tiles, subln_g, *, tq, out_scale):
    B, S, _ = qkv.shape
    H = DIFF_HEADS
    dv = 2 * DIFF_HD
    smem = pl.BlockSpec(memory_space=pltpu.SMEM)
    return pl.pallas_call(
        functools.partial(_diff_attn_kernel, tq=tq, n_tiles=S // tq, out_scale=out_scale),
        grid=(B, H),
        in_specs=[
            smem,
            pl.BlockSpec((1, S, dv), lambda b, h: (b, 0, h)),
            pl.BlockSpec((1, S, dv), lambda b, h: (b, 0, H + h)),
            pl.BlockSpec((1, S, dv), lambda b, h: (b, 0, 2 * H + h)),
            pl.BlockSpec((1, 2, tq, tq), lambda b, h: (h, 0, 0, 0)),
            pl.BlockSpec((1, dv), lambda b, h: (0, 0)),
        ],
        out_specs=pl.BlockSpec((1, S, dv), lambda b, h: (b, 0, h)),
        out_shape=jax.ShapeDtypeStruct((B, S, H * dv), BF16),
        compiler_params=_cparams("parallel", "parallel"),
        name="diff_attention",
    )(lam, qkv, qkv, qkv, bias_tiles, subln_g.reshape(1, dv))


def _cross_kernel(h_ref, g_ref, wq_ref, kv_ref, wo_ref, *rest, with_router):
    if with_router:
        rg_ref, wr_ref, o_ref, xn_ref, route_ref = rest
    else:
        (o_ref,) = rest
    h = h_ref[0]
    hn = _rms(h, g_ref[...]).astype(BF16)
    q = _dot(hn, wq_ref[...]).astype(BF16)
    kv = kv_ref[0]
    HD = CROSS_HEADS * CROSS_HD
    outs = []
    for hd in range(CROSS_HEADS):
        lo = hd * CROSS_HD
        s = lax.dot_general(q[:, lo:lo + CROSS_HD], kv[:, lo:lo + CROSS_HD], NT_DIMS,
                            preferred_element_type=F32)
        m = jnp.max(s, axis=-1, keepdims=True)
        p = jnp.exp(s - m)
        l = jnp.sum(p, axis=-1, keepdims=True)
        o = _dot(p.astype(BF16), kv[:, HD + lo:HD + lo + CROSS_HD]) / l
        outs.append(o.astype(BF16))
    o_all = jnp.concatenate(outs, axis=-1)
    h_new = h + _dot(o_all, wo_ref[...])
    o_ref[0] = h_new
    if with_router:
        xn_ref[0], route_ref[0] = _route(h_new, rg_ref[...], wr_ref[...])


def cross_attention(h, g, wq, kv, wo, *, kv_blk, tm, router=None):
    B, S, D = h.shape
    M = kv.shape[1]
    HD = CROSS_HEADS * CROSS_HD
    tm = min(tm, S)
    const = lambda shape: pl.BlockSpec(shape, lambda b, s: (0, 0))
    tile = lambda width: pl.BlockSpec((1, tm, width), lambda b, s: (b, s, 0))
    in_specs = [tile(D), const((1, D)), const((D, HD)),
                pl.BlockSpec((1, M, 2 * HD), lambda b, s: (b, 0, kv_blk)), const((HD, D))]
    args = [h, g.reshape(1, D), wq, kv, wo]
    out_specs = [tile(D)]
    out_shape = [jax.ShapeDtypeStruct((B, S, D), F32)]
    if router is not None:
        rg, w_router = router
        in_specs += [const((1, D)), const((D, LANE))]
        args += [rg.reshape(1, D), jnp.zeros((D, LANE), BF16).at[:, :N_EXPERTS].set(w_router.astype(BF16))]
        out_specs += [tile(D), tile(LANE)]
        out_shape += [jax.ShapeDtypeStruct((B, S, D), F32), jax.ShapeDtypeStruct((B, S, LANE), F32)]
    outs = pl.pallas_call(
        functools.partial(_cross_kernel, with_router=router is not None),
        grid=(B, S // tm),
        in_specs=in_specs,
        out_specs=out_specs,
        out_shape=out_shape,
        compiler_params=_cparams("parallel", "arbitrary"),
        name="cross_attention",
    )(*args)
    return outs if router is not None else outs[0]


def _swiglu_step(x, wg_ref, wu_ref, wd_ref, o_ref, rows=slice(None)):
    gt = _dot(x, _load_bf16(wg_ref))
    up = _dot(x, _load_bf16(wu_ref))
    hm = (gt * jax.nn.sigmoid(gt) * up).astype(BF16)
    o_ref[rows, :] += _dot(hm, _load_bf16(wd_ref))


def _ffn_kernel(h_ref, g_ref, wg_ref, wu_ref, wd_ref, o_ref, xn_ref):
    @pl.when(pl.program_id(1) == 0)
    def _():
        xn_ref[...] = _rms(h_ref[...], g_ref[...]).astype(BF16)
        o_ref[...] = h_ref[...]

    _swiglu_step(xn_ref[...], wg_ref, wu_ref, wd_ref, o_ref)


def dense_ffn(h, g, wg, wu, wd, *, tm, tf):
    T, D = h.shape
    F = wg.shape[1]
    tm = min(tm, T)
    return pl.pallas_call(
        _ffn_kernel,
        grid=(T // tm, F // tf),
        in_specs=[
            pl.BlockSpec((tm, D), lambda i, f: (i, 0)),
            pl.BlockSpec((1, D), lambda i, f: (0, 0)),
            pl.BlockSpec((D, tf), lambda i, f: (0, f)),
            pl.BlockSpec((D, tf), lambda i, f: (0, f)),
            pl.BlockSpec((tf, D), lambda i, f: (f, 0)),
        ],
        out_specs=pl.BlockSpec((tm, D), lambda i, f: (i, 0)),
        out_shape=jax.ShapeDtypeStruct((T, D), F32),
        scratch_shapes=[pltpu.VMEM((tm, D), BF16)],
        compiler_params=_cparams("parallel", "arbitrary"),
        name="dense_ffn",
    )(h, g.reshape(1, D), wg, wu, wd)


def _route(h, g, wr):
    xn = _rms(h, g)
    logits = _dot(xn.astype(BF16), wr)
    lane = lax.broadcasted_iota(jnp.int32, logits.shape, 1)
    logits = jnp.where(lane < N_EXPERTS, logits, -jnp.inf)
    v1 = jnp.max(logits, axis=-1, keepdims=True)
    i1 = jnp.min(jnp.where(logits == v1, lane, LANE), axis=-1, keepdims=True)
    rest = jnp.where(lane == i1, -jnp.inf, logits)
    v2 = jnp.max(rest, axis=-1, keepdims=True)
    i2 = jnp.min(jnp.where(rest == v2, lane, LANE), axis=-1, keepdims=True)
    e2 = jnp.exp(v2 - v1)
    g1 = 1.0 / (1.0 + e2)
    g2 = e2 / (1.0 + e2)
    route = jnp.where(lane == 0, i1.astype(F32), 0.0)
    route = jnp.where(lane == 1, i2.astype(F32), route)
    route = jnp.where(lane == 2, g1, route)
    route = jnp.where(lane == 3, g2, route)
    return xn, route


def _moe_kernel(tok_ref, be_ref, ns_ref, nu_ref, x_hbm, wg_ref, wu_ref, wd_ref, o_ref, xbuf, xb, sem,
                *, tm, sub, rows_per_step, n_steps):
    i = pl.program_id(0)
    step = pl.program_id(1)
    nused = nu_ref[0]
    n_rows = rows_per_step * n_steps

    def row_copy(blk, r):
        tok = tok_ref[blk * tm + r]
        return pltpu.make_async_copy(x_hbm.at[pl.ds(tok, 1)], xbuf.at[pl.ds(r, 1)], sem.at[0])

    def for_rows(fn):
        def body(r, c):
            fn(r)
            return c
        lax.fori_loop(0, n_rows, body, 0, unroll=4)

    @pl.when(step == 0)
    def _():
        o_ref[...] = jnp.zeros(o_ref.shape, F32)

        @pl.when(i == 0)
        def _():
            for_rows(lambda r: row_copy(0, r).start())

        @pl.when(i <= nused)
        def _():
            for_rows(lambda r: row_copy(i, r).wait())

    @pl.when(i < nused)
    def _():
        @pl.when(step == 0)
        def _():
            xb[...] = xbuf[0:tm, :].astype(BF16)

        for j in range(rows_per_step):
            row_copy(i + 1, step * rows_per_step + j).start()

        def sub_block(r):
            rows = pl.ds(r * sub, sub)
            _swiglu_step(xb[rows, :], wg_ref, wu_ref, wd_ref, o_ref, rows)

        sub_block(0)
        for r in range(1, tm // sub):
            pl.when(r < ns_ref[i])(functools.partial(sub_block, r))


def moe_experts(xn, slot_tok, block_e, block_nsub, nused, wg, wu, wd, *, tm, sub, tf):
    T, D = xn.shape
    P = slot_tok.shape[0]
    F = wg.shape[2]
    nf = n_steps = F // tf
    rows_per_step = -(-tm // n_steps)
    spare = rows_per_step * n_steps - tm
    assert (tm + spare) % 4 == 0
    slot_tok = jnp.concatenate([slot_tok, jnp.zeros((spare,), jnp.int32)])
    xbuf_rows = -(-(tm + spare) // SUBLANE) * SUBLANE

    def live(i, f, nu):
        return jnp.minimum(i, nu[0] - 1), jnp.where(i < nu[0], f, nf - 1)

    def up_map(i, f, tok, be, ns, nu):
        ii, ff = live(i, f, nu)
        return be[ii], 0, ff

    def down_map(i, f, tok, be, ns, nu):
        ii, ff = live(i, f, nu)
        return be[ii], ff, 0

    return pl.pallas_call(
        functools.partial(_moe_kernel, tm=tm, sub=sub, rows_per_step=rows_per_step, n_steps=n_steps),
        grid_spec=pltpu.PrefetchScalarGridSpec(
            num_scalar_prefetch=4,
            grid=(P // tm, nf),
            in_specs=[
                pl.BlockSpec(memory_space=pl.ANY),
                pl.BlockSpec((1, D, tf), up_map),
                pl.BlockSpec((1, D, tf), up_map),
                pl.BlockSpec((1, tf, D), down_map),
            ],
            out_specs=pl.BlockSpec((tm, D), lambda i, f, tok, be, ns, nu: (i, 0)),
            scratch_shapes=[
                pltpu.VMEM((xbuf_rows, D), F32),
                pltpu.VMEM((tm, D), BF16),
                pltpu.SemaphoreType.DMA((1,)),
            ],
        ),
        out_shape=jax.ShapeDtypeStruct((P, D), F32),
        compiler_params=_cparams("arbitrary", "arbitrary"),
        name="moe_experts",
    )(slot_tok, block_e, block_nsub, nused, xn, wg, wu, wd)


def _combine_kernel(pos_ref, h_ref, route_ref, g_ref, ys_hbm, o_ref, buf_a, buf_b, sem, *, tm, n_blocks):
    i = pl.program_id(0)
    nxt = jnp.where(i + 1 < n_blocks, i + 1, 0)

    def row_copy(blk, buf, parity, r, k):
        p = pos_ref[2 * (blk * tm + r) + k]
        return pltpu.make_async_copy(ys_hbm.at[pl.ds(p, 1)], buf.at[k, pl.ds(r, 1)], sem.at[parity, k])

    def for_rows(fn):
        def body(r, c):
            fn(r, 0)
            fn(r, 1)
            return c
        lax.fori_loop(0, tm, body, 0, unroll=8)

    @pl.when(i == 0)
    def _():
        for_rows(lambda r, k: row_copy(0, buf_a, 0, r, k).start())

    def step(parity, cur, other):
        for_rows(lambda r, k: row_copy(i, cur, parity, r, k).wait())
        for r in range(tm):
            row_copy(nxt, other, 1 - parity, r, 0).start()
            row_copy(nxt, other, 1 - parity, r, 1).start()
        route = route_ref[...]
        y = route[:, 2:3] * cur[0] + route[:, 3:4] * cur[1]
        o_ref[...] = _rms(h_ref[...] + y, g_ref[...])

        @pl.when(i == n_blocks - 1)
        def _():
            for_rows(lambda r, k: row_copy(0, other, 1 - parity, r, k).wait())

    pl.when(i % 2 == 0)(functools.partial(step, 0, buf_a, buf_b))
    pl.when(i % 2 == 1)(functools.partial(step, 1, buf_b, buf_a))


def moe_combine_norm(h, route, ys, pos, g, *, tm):
    T, D = h.shape
    tm = min(tm, T)
    return pl.pallas_call(
        functools.partial(_combine_kernel, tm=tm, n_blocks=T // tm),
        grid_spec=pltpu.PrefetchScalarGridSpec(
            num_scalar_prefetch=1,
            grid=(T // tm,),
            in_specs=[
                pl.BlockSpec((tm, D), lambda i, pos: (i, 0)),
                pl.BlockSpec((tm, LANE), lambda i, pos: (i, 0)),
                pl.BlockSpec((1, D), lambda i, pos: (0, 0)),
                pl.BlockSpec(memory_space=pl.ANY),
            ],
            out_specs=pl.BlockSpec((tm, D), lambda i, pos: (i, 0)),
            scratch_shapes=[pltpu.VMEM((2, tm, D), F32), pltpu.VMEM((2, tm, D), F32),
                            pltpu.SemaphoreType.DMA((2, 2))],
        ),
        out_shape=jax.ShapeDtypeStruct((T, D), F32),
        compiler_params=_cparams("arbitrary"),
        name="moe_combine_norm",
    )(pos, h, route, g.reshape(1, D), ys)


def _dispatch(route, tm, sub):
    T = route.shape[0]
    A = 2 * T
    P = A + N_EXPERTS * tm
    nblk = P // tm
    flat_e = route[:, :2].astype(jnp.int32).reshape(A)
    onehot = (flat_e[:, None] == jnp.arange(N_EXPERTS, dtype=jnp.int32)[None, :]).astype(jnp.int32)
    csum = jnp.cumsum(onehot, axis=0)
    counts = csum[-1]
    rank = jnp.sum(csum * onehot, axis=1) - 1
    padded = ((counts + tm - 1) // tm) * tm
    pend = jnp.cumsum(padded)
    pstart = pend - padded
    dest = (jnp.sum(onehot * pstart[None, :], axis=1) + rank).astype(jnp.int32)
    slot_tok = jnp.zeros((P,), jnp.int32).at[dest].set(jnp.arange(A, dtype=jnp.int32) // 2)
    blk_row0 = jnp.arange(nblk, dtype=jnp.int32) * tm
    block_e = jnp.minimum(jnp.searchsorted(pend, blk_row0, side="right"), N_EXPERTS - 1).astype(jnp.int32)
    live_end = pstart + ((counts + sub - 1) // sub) * sub
    block_nsub = (jnp.clip(live_end[block_e] - blk_row0, 0, tm) // sub).astype(jnp.int32)
    nused = (pend[-1] // tm).astype(jnp.int32).reshape(1)
    return slot_tok, dest, block_e, block_nsub, nused


def _rope_slabs(seq):
    pos = jnp.arange(seq, dtype=F32)
    inv = jnp.power(ROPE_THETA, -jnp.arange(0, MLA_ROPE, 2, dtype=F32) / MLA_ROPE)
    ang = pos[:, None] * inv[None, :]
    z = jnp.zeros((seq, LANE - MLA_ROPE), F32)
    ct = jnp.concatenate([jnp.cos(ang), jnp.cos(ang), z], axis=1)
    st = jnp.concatenate([jnp.sin(ang), jnp.sin(ang), z], axis=1)
    return ct, st


def _t5_bucket(rel):
    half = REL_BUCKETS // 2
    max_exact = half // 2
    ret = (rel > 0).astype(jnp.int32) * half
    n = jnp.abs(rel)
    nf = jnp.maximum(n, 1).astype(F32)
    large = max_exact + (jnp.log(nf / max_exact) / math.log(REL_MAX_DIST / max_exact)
                         * (half - max_exact)).astype(jnp.int32)
    large = jnp.minimum(large, half - 1)
    return ret + jnp.where(n < max_exact, n, large)


def _bias_tiles(rel_bias, tq):
    assert tq >= REL_MAX_DIST
    qi = jnp.arange(tq, dtype=jnp.int32)[:, None]
    ki = jnp.arange(tq, dtype=jnp.int32)[None, :]

    def lookup(rel):
        bucket = _t5_bucket(rel)[None]
        out = jnp.zeros((rel_bias.shape[1],) + rel.shape, F32)
        for b in range(REL_BUCKETS):
            out = jnp.where(bucket == b, rel_bias[b][:, None, None], out)
        return out

    far = lookup(jnp.full((1, 1), -2 * tq, jnp.int32))
    prev = lookup(ki - qi - tq) - far
    diag = jnp.where(((ki // CHUNK) <= (qi // CHUNK))[None], lookup(ki - qi) - far, NEG)
    return jnp.stack([prev, diag], axis=1)


def kernel(x, mem, rel_bias, mem_norm_g, norm_mix_g, norm_cross_g, norm_ffn_g, cross_wq, cross_wkv, cross_wo, ev_w_in, ev_conv_w, ev_conv_b, ev_ln_g, ev_ln_b, ev_q_norm_g, ev_w_uq, ev_kv_norm_g, ev_w_ukv, ev_w_out, ev_ffn_wg, ev_ffn_wu, ev_ffn_wd, od_w_in, od_lambda_q1, od_lambda_k1, od_lambda_q2, od_lambda_k2, od_subln_g, od_w_out, od_router, od_moe_wg, od_moe_wu, od_moe_wd, final_norm_g):
    B, S, D = x.shape
    T = B * S
    M = mem.shape[1]
    AW = ev_conv_w.shape[2]
    H = MLA_HEADS
    R = MLA_RANK
    h = x.reshape(T, D)

    kr0 = 2 * AW + 2 * R
    half = MLA_ROPE // 2
    w_kr = ev_w_in[0][:, kr0:kr0 + MLA_ROPE]
    w_kr = jnp.concatenate([w_kr, -w_kr[:, half:], w_kr[:, :half]], axis=1).astype(BF16)

    q_scale = (MLA_NOPE + MLA_ROPE) ** -0.5 * LOG2E
    wuq = (ev_w_uq[0] * q_scale).reshape(R, H, MLA_NOPE + MLA_ROPE)
    w_nope, w_r1, w_r2 = wuq[..., :MLA_NOPE], wuq[..., MLA_NOPE:MLA_NOPE + half], wuq[..., MLA_NOPE + half:]
    zq = jnp.zeros((R, H, MLA_QK_PAD - MLA_NOPE - MLA_ROPE), F32)
    wqa = jnp.concatenate([w_nope, w_r1, w_r2, zq], axis=-1).reshape(R, H * MLA_QK_PAD).astype(BF16)
    wqb = jnp.concatenate([-w_r2, w_r1, zq], axis=-1).reshape(R, H * LANE).astype(BF16)
    wukv = ev_w_ukv[0].reshape(R, H, MLA_NOPE + MLA_V)
    wk = wukv[..., :MLA_NOPE].reshape(R, H * MLA_NOPE).astype(BF16)
    wv = wukv[..., MLA_NOPE:].reshape(R, H * MLA_V).astype(BF16)
    ct, st = _rope_slabs(S)

    c_scale = CROSS_HD ** -0.5
    wq_c = (cross_wq * c_scale).astype(BF16)
    wkv_c = jnp.concatenate([cross_wkv[0], cross_wkv[1]], axis=1).astype(BF16)
    wo_c = cross_wo.astype(BF16)

    d_scale = DIFF_HD ** -0.5 * LOG2E
    layer = 1
    lambda_init = 0.8 - 0.6 * math.exp(-0.3 * layer)
    lam = (jnp.exp(jnp.sum(od_lambda_q1[0] * od_lambda_k1[0]))
           - jnp.exp(jnp.sum(od_lambda_q2[0] * od_lambda_k2[0])) + lambda_init).reshape(1).astype(F32)
    tq = min(256, S)
    bias_tiles = _bias_tiles(rel_bias * LOG2E, tq)

    kv_mem = norm_matmul(mem.reshape(B * M, D), mem_norm_g, wkv_c, tm=512, tn=512)
    kv_mem = kv_mem.reshape(B, M, -1)

    z, z_kr = norm_matmul(h, norm_mix_g[0], ev_w_in[0].astype(BF16), tm=1024, tn=1024, n_cols=kr0, side_w=w_kr)
    a_out = conformer_conv(z.reshape(B, S, -1), ev_conv_w[0], ev_conv_b[0], ev_ln_g[0], ev_ln_b[0], ts=256)
    q, k, v = mla_proj(z, z_kr, ev_q_norm_g[0], ev_kv_norm_g[0], wqa, wqb, wk, wv, ct, st, seq=S, tm=512)
    b_out = mla_attention(q.reshape(B, S, -1), k.reshape(B, S, -1), v.reshape(B, S, -1), tq=tq)
    h = matmul_res(h, [a_out.reshape(T, AW), b_out.reshape(T, H * MLA_V)], ev_w_out[0], tm=1024, tn=1024)
    h = cross_attention(h.reshape(B, S, D), norm_cross_g[0], wq_c[0], kv_mem, wo_c[0], kv_blk=0, tm=512)
    h = dense_ffn(h.reshape(T, D), norm_ffn_g[0], ev_ffn_wg[0], ev_ffn_wu[0], ev_ffn_wd[0], tm=1024, tf=256)

    qkv = norm_matmul(h, norm_mix_g[1], od_w_in[0].astype(BF16), tm=1024, tn=1024, scaled_cols=D, scale=d_scale)
    o = diff_attention(qkv.reshape(B, S, -1), lam, bias_tiles, od_subln_g[0], tq=tq,
                       out_scale=1.0 - lambda_init)
    h = matmul_res(h, [o.reshape(T, D)], od_w_out[0], tm=1024, tn=1024)
    h, xn, route = cross_attention(h.reshape(B, S, D), norm_cross_g[1], wq_c[1], kv_mem, wo_c[1], kv_blk=1,
                                   tm=512, router=(norm_ffn_g[1], od_router[0]))
    h, xn, route = h.reshape(T, D), xn.reshape(T, D), route.reshape(T, LANE)

    tm_moe, sub_moe = 1024, 512
    slot_tok, dest, block_e, block_nsub, nused = _dispatch(route, tm_moe, sub_moe)
    ys = moe_experts(xn, slot_tok, block_e, block_nsub, nused, od_moe_wg[0], od_moe_wu[0], od_moe_wd[0],
                     tm=tm_moe, sub=sub_moe, tf=256)
    out = moe_combine_norm(h, route, ys, dest, final_norm_g, tm=256)
    return out.reshape(B, S, D)
```

```python
import functools
import math

import jax
import jax.numpy as jnp
from jax import lax
from jax.experimental import pallas as pl
from jax.experimental.pallas import tpu as pltpu

F32 = jnp.float32
BF16 = jnp.bfloat16

EPS = 1e-6
NEG = -1e30
LOG2E = math.log2(math.e)
CHUNK = 64

CONV_WIDTH = 31
CONV_HALO = 32
MLA_HEADS = 8
MLA_NOPE = 128
MLA_ROPE = 64
MLA_V = 128
MLA_RANK = 512
MLA_QK_PAD = 256
ROPE_THETA = 10000.0
DIFF_HEADS = 8
DIFF_HD = 128
REL_BUCKETS = 32
REL_MAX_DIST = 128
CROSS_HEADS = 4
CROSS_HD = 128
N_EXPERTS = 8
LANE = 128
SUBLANE = 8

VMEM_LIMIT = 60 * 1024 * 1024

NT_DIMS = (((1,), (1,)), ((), ()))


def _cparams(*sem):
    return pltpu.CompilerParams(dimension_semantics=sem, vmem_limit_bytes=VMEM_LIMIT)


def _rms(x, g):
    return x * lax.rsqrt(jnp.mean(x * x, axis=-1, keepdims=True) + EPS) * g


def _dot(a, b):
    return jnp.dot(a, b, preferred_element_type=F32)


def _pack_bf16_pairs(x):
    n = x.shape[1] // 2
    bits = lax.bitcast_convert_type(x.astype(BF16).astype(F32), jnp.uint32)
    return (bits[:, :n] >> 16) | bits[:, n:]


def _unpack_bf16_pairs(w):
    lo = lax.bitcast_convert_type(w << 16, F32)
    hi = lax.bitcast_convert_type(w & jnp.uint32(0xFFFF0000), F32)
    return lo.astype(BF16), hi.astype(BF16)


def _load_bf16(w_ref):
    w = w_ref[0] if len(w_ref.shape) == 3 else w_ref[...]
    return w.astype(BF16)


def _norm_matmul_kernel(x_ref, g_ref, w_ref, *refs, scaled_tiles, scale, has_side):
    if has_side:
        ws_ref, o_ref, os_ref, xn_ref = refs
    else:
        o_ref, xn_ref = refs
    j = pl.program_id(1)

    @pl.when(j == 0)
    def _():
        xn_ref[...] = _rms(x_ref[...], g_ref[...]).astype(BF16)
        if has_side:
            os_ref[...] = _dot(xn_ref[...], ws_ref[...]).astype(os_ref.dtype)

    acc = _dot(xn_ref[...], w_ref[...].astype(BF16))
    if scaled_tiles:
        acc = acc * jnp.where(j < scaled_tiles, scale, 1.0)
    o_ref[...] = acc.astype(o_ref.dtype)


def norm_matmul(x, g, w, *, tm, tn, n_cols=None, scaled_cols=0, scale=1.0, side_w=None):
    M, K = x.shape
    N = n_cols or w.shape[1]
    tm = min(tm, M)
    assert scaled_cols % tn == 0 and N % tn == 0
    in_specs = [
        pl.BlockSpec((tm, K), lambda i, j: (i, 0)),
        pl.BlockSpec((1, K), lambda i, j: (0, 0)),
        pl.BlockSpec((K, tn), lambda i, j: (0, j)),
    ]
    out_specs = [pl.BlockSpec((tm, tn), lambda i, j: (i, j))]
    out_shape = [jax.ShapeDtypeStruct((M, N), BF16)]
    args = [x, g.reshape(1, K), w]
    if side_w is not None:
        ns = side_w.shape[1]
        in_specs.append(pl.BlockSpec((K, ns), lambda i, j: (0, 0)))
        out_specs.append(pl.BlockSpec((tm, ns), lambda i, j: (i, 0)))
        out_shape.append(jax.ShapeDtypeStruct((M, ns), BF16))
        args.append(side_w)
    outs = pl.pallas_call(
        functools.partial(_norm_matmul_kernel, scaled_tiles=scaled_cols // tn, scale=scale,
                          has_side=side_w is not None),
        grid=(M // tm, N // tn),
        in_specs=in_specs,
        out_specs=out_specs,
        out_shape=out_shape,
        scratch_shapes=[pltpu.VMEM((tm, K), BF16)],
        compiler_params=_cparams("parallel", "arbitrary"),
        name="norm_matmul",
    )(*args)
    return outs if side_w is not None else outs[0]


def _matmul_res_kernel(res_ref, *refs, n):
    a_refs, w_ref, o_ref, wb_ref = refs[:n], refs[n], refs[n + 1], refs[n + 2]

    @pl.when(pl.program_id(1) == 0)
    def _():
        wb_ref[...] = w_ref[...].astype(BF16)

    ka = a_refs[0].shape[1]
    acc = res_ref[...]
    for k in range(n):
        acc = acc + _dot(a_refs[k][...], wb_ref[k * ka:(k + 1) * ka, :])
    o_ref[...] = acc


def matmul_res(res, a_list, w, *, tm, tn):
    M, N = res.shape
    K = w.shape[0]
    tm = min(tm, M)
    n = len(a_list)
    ka = a_list[0].shape[1]
    assert all(a.shape[1] == ka for a in a_list) and K == n * ka
    in_specs = [pl.BlockSpec((tm, tn), lambda j, i: (i, j))]
    in_specs += [pl.BlockSpec((tm, ka), lambda j, i: (i, 0)) for _ in a_list]
    in_specs += [pl.BlockSpec((K, tn), lambda j, i: (0, j))]
    return pl.pallas_call(
        functools.partial(_matmul_res_kernel, n=n),
        grid=(N // tn, M // tm),
        in_specs=in_specs,
        out_specs=pl.BlockSpec((tm, tn), lambda j, i: (i, j)),
        out_shape=jax.ShapeDtypeStruct((M, N), F32),
        scratch_shapes=[pltpu.VMEM((K, tn), BF16)],
        compiler_params=_cparams("parallel", "arbitrary"),
        name="matmul_res",
    )(res, *a_list, w)


def _conv_kernel(val_ref, gate_ref, w_ref, b_ref, lg_ref, lb_ref, o_ref, ubuf, shifted, *, ts):
    s = pl.program_id(1)

    @pl.when(s == 0)
    def _():
        ubuf[0:CONV_HALO, :] = jnp.zeros((CONV_HALO, ubuf.shape[1]), F32)

    @pl.when(s > 0)
    def _():
        ubuf[0:CONV_HALO, :] = ubuf[ts:ts + CONV_HALO, :]

    val = val_ref[0].astype(F32)
    gate = gate_ref[0].astype(F32)
    ubuf[CONV_HALO:CONV_HALO + ts, :] = val * jax.nn.sigmoid(gate)

    span = ts + CONV_HALO - SUBLANE
    for k in range(1, SUBLANE):
        shifted[k - 1] = ubuf[k:k + span, :]

    base = CONV_HALO - (CONV_WIDTH - 1)
    acc = jnp.zeros((ts, ubuf.shape[1]), F32) + b_ref[...]
    for j in range(CONV_WIDTH):
        k = (base + j) % SUBLANE
        a = base + j - k
        window = ubuf[a:a + ts, :] if k == 0 else shifted[k - 1, a:a + ts, :]
        acc = acc + w_ref[j:j + 1, :] * window

    mu = jnp.mean(acc, axis=-1, keepdims=True)
    xc = acc - mu
    y = xc * lax.rsqrt(jnp.mean(xc * xc, axis=-1, keepdims=True) + EPS)
    y = y * lg_ref[...] + lb_ref[...]
    o_ref[0] = (y * jax.nn.sigmoid(y)).astype(o_ref.dtype)


def conformer_conv(z, conv_w, conv_b, ln_g, ln_b, *, ts):
    B, S, _ = z.shape
    C = conv_w.shape[1]
    ts = min(ts, S)
    wpad = jnp.zeros((CONV_HALO, C), F32).at[:CONV_WIDTH].set(conv_w)
    row = lambda v: v.reshape(1, C)
    const = lambda shape: pl.BlockSpec(shape, lambda b, s: (0, 0))
    return pl.pallas_call(
        functools.partial(_conv_kernel, ts=ts),
        grid=(B, S // ts),
        in_specs=[
            pl.BlockSpec((1, ts, C), lambda b, s: (b, s, 0)),
            pl.BlockSpec((1, ts, C), lambda b, s: (b, s, 1)),
            const((CONV_HALO, C)), const((1, C)), const((1, C)), const((1, C)),
        ],
        out_specs=pl.BlockSpec((1, ts, C), lambda b, s: (b, s, 0)),
        out_shape=jax.ShapeDtypeStruct((B, S, C), BF16),
        scratch_shapes=[pltpu.VMEM((ts + CONV_HALO, C), F32),
                        pltpu.VMEM((SUBLANE - 1, ts + CONV_HALO - SUBLANE, C), F32)],
        compiler_params=_cparams("parallel", "arbitrary"),
        name="conformer_conv",
    )(z, z, wpad, row(conv_b), row(ln_g), row(ln_b))


def _mla_proj_kernel(cq_ref, ckv_ref, kr_ref, qg_ref, kvg_ref, wqa_ref, wqb_ref, wk_ref, wv_ref,
                     ct_ref, st_ref, q_ref, k_ref, v_ref):
    cqn = _rms(cq_ref[...].astype(F32), qg_ref[...]).astype(BF16)
    ckvn = _rms(ckv_ref[...].astype(F32), kvg_ref[...]).astype(BF16)
    ct = ct_ref[...]
    st = st_ref[...]

    qa = _dot(cqn, wqa_ref[...])
    qb = _dot(cqn, wqb_ref[...])
    kn = _dot(ckvn, wk_ref[...])
    v_ref[...] = _dot(ckvn, wv_ref[...]).astype(BF16)

    kr = kr_ref[...].astype(F32)
    k_rope = (kr * ct + pltpu.roll(kr, LANE // 2, axis=1) * st).astype(BF16)

    for h in range(MLA_HEADS):
        o = h * MLA_QK_PAD
        q_ref[:, o:o + LANE] = qa[:, o:o + LANE].astype(BF16)
        q_ref[:, o + LANE:o + 2 * LANE] = (
            qa[:, o + LANE:o + 2 * LANE] * ct + qb[:, h * LANE:(h + 1) * LANE] * st).astype(BF16)
        k_ref[:, o:o + LANE] = kn[:, h * LANE:(h + 1) * LANE].astype(BF16)
        k_ref[:, o + LANE:o + 2 * LANE] = k_rope


def mla_proj(z, z_kr, q_norm_g, kv_norm_g, wqa, wqb, wk, wv, ct, st, *, seq, tm):
    T = z.shape[0]
    tm = min(tm, seq)
    n_s = seq // tm
    R = MLA_RANK
    cq_blk = (2 * 1024) // R
    const = lambda shape: pl.BlockSpec(shape, lambda i: (0, 0))
    HQ = MLA_HEADS * MLA_QK_PAD
    HV = MLA_HEADS * MLA_V
    return pl.pallas_call(
        _mla_proj_kernel,
        grid=(T // tm,),
        in_specs=[
            pl.BlockSpec((tm, R), lambda i: (i, cq_blk)),
            pl.BlockSpec((tm, R), lambda i: (i, cq_blk + 1)),
            pl.BlockSpec((tm, LANE), lambda i: (i, 0)),
            const((1, R)), const((1, R)),
            const((R, HQ)), const((R, MLA_HEADS * LANE)), const((R, HV)), const((R, HV)),
            pl.BlockSpec((tm, LANE), lambda i: (i % n_s, 0)),
            pl.BlockSpec((tm, LANE), lambda i: (i % n_s, 0)),
        ],
        out_specs=[
            pl.BlockSpec((tm, HQ), lambda i: (i, 0)),
            pl.BlockSpec((tm, HQ), lambda i: (i, 0)),
            pl.BlockSpec((tm, HV), lambda i: (i, 0)),
        ],
        out_shape=[
            jax.ShapeDtypeStruct((T, HQ), BF16),
            jax.ShapeDtypeStruct((T, HQ), BF16),
            jax.ShapeDtypeStruct((T, HV), BF16),
        ],
        compiler_params=_cparams("parallel"),
        name="mla_proj",
    )(z, z, z_kr, q_norm_g.reshape(1, R), kv_norm_g.reshape(1, R), wqa, wqb, wk, wv, ct, st)


def _softmax_pv(s, v):
    m = jnp.max(s, axis=-1, keepdims=True)
    p = jnp.exp2(s - m)
    l = jnp.sum(p, axis=-1, keepdims=True)
    return _dot(p.astype(BF16), v) / l


def _mla_attn_kernel(q_ref, k_ref, v_ref, o_ref, *, tq, n_tiles):
    qc = lax.broadcasted_iota(jnp.int32, (tq, tq), 0) // CHUNK
    kc = lax.broadcasted_iota(jnp.int32, (tq, tq), 1) // CHUNK
    visible = kc <= qc

    for c in range(n_tiles):
        n = (c + 1) * tq
        rows = slice(c * tq, n)
        s = lax.dot_general(q_ref[0, rows, :], k_ref[0, :n, :], NT_DIMS, preferred_element_type=F32)
        parts = [s[:, :c * tq]] if c else []
        parts.append(jnp.where(visible, s[:, c * tq:], NEG))
        s = jnp.concatenate(parts, axis=1) if c else parts[0]
        o_ref[0, rows, :] = _softmax_pv(s, v_ref[0, :n, :]).astype(o_ref.dtype)


def mla_attention(q, k, v, *, tq):
    B, S, _ = q.shape
    tq = min(tq, S)
    head = lambda b, h: (b, 0, h)
    return pl.pallas_call(
        functools.partial(_mla_attn_kernel, tq=tq, n_tiles=S // tq),
        grid=(B, MLA_HEADS),
        in_specs=[
            pl.BlockSpec((1, S, MLA_QK_PAD), head),
            pl.BlockSpec((1, S, MLA_QK_PAD), head),
            pl.BlockSpec((1, S, MLA_V), head),
        ],
        out_specs=pl.BlockSpec((1, S, MLA_V), head),
        out_shape=jax.ShapeDtypeStruct((B, S, MLA_HEADS * MLA_V), BF16),
        compiler_params=_cparams("parallel", "parallel"),
        name="mla_attention",
    )(q, k, v)


def _diff_attn_kernel(lam_ref, q_ref, k_ref, v_ref, bias_ref, g_ref, o_ref, *, tq, n_tiles, out_scale):
    lam = lam_ref[0]

    for c in range(n_tiles):
        n = (c + 1) * tq
        rows = slice(c * tq, n)
        v = v_ref[0, :n, :]

        def half(lo):
            s = lax.dot_general(q_ref[0, rows, lo:lo + DIFF_HD], k_ref[0, :n, lo:lo + DIFF_HD], NT_DIMS,
                                preferred_element_type=F32)
            parts = []
            if c >= 2:
                parts.append(s[:, :(c - 1) * tq])
            if c >= 1:
                parts.append(s[:, (c - 1) * tq:c * tq] + bias_ref[0, 0])
            parts.append(s[:, c * tq:] + bias_ref[0, 1])
            s = jnp.concatenate(parts, axis=1) if c else parts[0]
            return _softmax_pv(s, v)

        o = half(0) - lam * half(DIFF_HD)
        o_ref[0, rows, :] = (_rms(o, g_ref[...]) * out_scale).astype(o_ref.dtype)


def diff_attention(qkv, lam, bias_tiles, subln_g, *, tq, out_scale):
    B, S, _ = qkv.shape
    H = DIFF_HEADS
    dv = 2 * DIFF_HD
    smem = pl.BlockSpec(memory_space=pltpu.SMEM)
    return pl.pallas_call(
        functools.partial(_diff_attn_kernel, tq=tq, n_tiles=S // tq, out_scale=out_scale),
        grid=(B, H),
        in_specs=[
            smem,
            pl.BlockSpec((1, S, dv), lambda b, h: (b, 0, h)),
            pl.BlockSpec((1, S, dv), lambda b, h: (b, 0, H + h)),
            pl.BlockSpec((1, S, dv), lambda b, h: (b, 0, 2 * H + h)),
            pl.BlockSpec((1, 2, tq, tq), lambda b, h: (h, 0, 0, 0)),
            pl.BlockSpec((1, dv), lambda b, h: (0, 0)),
        ],
        out_specs=pl.BlockSpec((1, S, dv), lambda b, h: (b, 0, h)),
        out_shape=jax.ShapeDtypeStruct((B, S, H * dv), BF16),
        compiler_params=_cparams("parallel", "parallel"),
        name="diff_attention",
    )(lam, qkv, qkv, qkv, bias_tiles, subln_g.reshape(1, dv))


def _cross_kernel(h_ref, g_ref, wq_ref, kv_ref, wo_ref, *rest, with_router):
    if with_router:
        rg_ref, wr_ref, o_ref, xn_ref, route_ref = rest
    else:
        (o_ref,) = rest
    h = h_ref[0]
    hn = _rms(h, g_ref[...]).astype(BF16)
    q = _dot(hn, wq_ref[...]).astype(BF16)
    kv = kv_ref[0]
    HD = CROSS_HEADS * CROSS_HD
    outs = []
    for hd in range(CROSS_HEADS):
        lo = hd * CROSS_HD
        s = lax.dot_general(q[:, lo:lo + CROSS_HD], kv[:, lo:lo + CROSS_HD], NT_DIMS,
                            preferred_element_type=F32)
        m = jnp.max(s, axis=-1, keepdims=True)
        p = jnp.exp(s - m)
        l = jnp.sum(p, axis=-1, keepdims=True)
        o = _dot(p.astype(BF16), kv[:, HD + lo:HD + lo + CROSS_HD]) / l
        outs.append(o.astype(BF16))
    o_all = jnp.concatenate(outs, axis=-1)
    h_new = h + _dot(o_all, wo_ref[...])
    o_ref[0] = h_new
    if with_router:
        xn, route_ref[0] = _route(h_new, rg_ref[...], wr_ref[...])
        xn_ref[0] = _pack_bf16_pairs(xn)


def cross_attention(h, g, wq, kv, wo, *, kv_blk, tm, router=None):
    B, S, D = h.shape
    M = kv.shape[1]
    HD = CROSS_HEADS * CROSS_HD
    tm = min(tm, S)
    const = lambda shape: pl.BlockSpec(shape, lambda b, s: (0, 0))
    tile = lambda width: pl.BlockSpec((1, tm, width), lambda b, s: (b, s, 0))
    in_specs = [tile(D), const((1, D)), const((D, HD)),
                pl.BlockSpec((1, M, 2 * HD), lambda b, s: (b, 0, kv_blk)), const((HD, D))]
    args = [h, g.reshape(1, D), wq, kv, wo]
    out_specs = [tile(D)]
    out_shape = [jax.ShapeDtypeStruct((B, S, D), F32)]
    if router is not None:
        rg, w_router = router
        in_specs += [const((1, D)), const((D, LANE))]
        args += [rg.reshape(1, D), jnp.zeros((D, LANE), BF16).at[:, :N_EXPERTS].set(w_router.astype(BF16))]
        out_specs += [tile(D // 2), tile(LANE)]
        out_shape += [jax.ShapeDtypeStruct((B, S, D // 2), jnp.uint32),
                      jax.ShapeDtypeStruct((B, S, LANE), F32)]
    outs = pl.pallas_call(
        functools.partial(_cross_kernel, with_router=router is not None),
        grid=(B, S // tm),
        in_specs=in_specs,
        out_specs=out_specs,
        out_shape=out_shape,
        compiler_params=_cparams("parallel", "arbitrary"),
        name="cross_attention",
    )(*args)
    return outs if router is not None else outs[0]


def _swiglu_step(x, wg_ref, wu_ref, wd_ref, o_ref, rows=slice(None)):
    gt = _dot(x, _load_bf16(wg_ref))
    up = _dot(x, _load_bf16(wu_ref))
    hm = (gt * jax.nn.sigmoid(gt) * up).astype(BF16)
    o_ref[rows, :] += _dot(hm, _load_bf16(wd_ref))


def _ffn_kernel(h_ref, g_ref, wg_ref, wu_ref, wd_ref, o_ref, xn_ref):
    @pl.when(pl.program_id(1) == 0)
    def _():
        xn_ref[...] = _rms(h_ref[...], g_ref[...]).astype(BF16)
        o_ref[...] = h_ref[...]

    _swiglu_step(xn_ref[...], wg_ref, wu_ref, wd_ref, o_ref)


def dense_ffn(h, g, wg, wu, wd, *, tm, tf):
    T, D = h.shape
    F = wg.shape[1]
    tm = min(tm, T)
    return pl.pallas_call(
        _ffn_kernel,
        grid=(T // tm, F // tf),
        in_specs=[
            pl.BlockSpec((tm, D), lambda i, f: (i, 0)),
            pl.BlockSpec((1, D), lambda i, f: (0, 0)),
            pl.BlockSpec((D, tf), lambda i, f: (0, f)),
            pl.BlockSpec((D, tf), lambda i, f: (0, f)),
            pl.BlockSpec((tf, D), lambda i, f: (f, 0)),
        ],
        out_specs=pl.BlockSpec((tm, D), lambda i, f: (i, 0)),
        out_shape=jax.ShapeDtypeStruct((T, D), F32),
        scratch_shapes=[pltpu.VMEM((tm, D), BF16)],
        compiler_params=_cparams("parallel", "arbitrary"),
        name="dense_ffn",
    )(h, g.reshape(1, D), wg, wu, wd)


def _route(h, g, wr):
    xn = _rms(h, g)
    logits = _dot(xn.astype(BF16), wr)
    lane = lax.broadcasted_iota(jnp.int32, logits.shape, 1)
    logits = jnp.where(lane < N_EXPERTS, logits, -jnp.inf)
    v1 = jnp.max(logits, axis=-1, keepdims=True)
    i1 = jnp.min(jnp.where(logits == v1, lane, LANE), axis=-1, keepdims=True)
    rest = jnp.where(lane == i1, -jnp.inf, logits)
    v2 = jnp.max(rest, axis=-1, keepdims=True)
    i2 = jnp.min(jnp.where(rest == v2, lane, LANE), axis=-1, keepdims=True)
    e2 = jnp.exp(v2 - v1)
    g1 = 1.0 / (1.0 + e2)
    g2 = e2 / (1.0 + e2)
    route = jnp.where(lane == 0, i1.astype(F32), 0.0)
    route = jnp.where(lane == 1, i2.astype(F32), route)
    route = jnp.where(lane == 2, g1, route)
    route = jnp.where(lane == 3, g2, route)
    return xn, route


def _moe_kernel(tok_ref, be_ref, ns_ref, nu_ref, x_hbm, wg_ref, wu_ref, wd_ref, o_ref, xbuf, xb, sem,
                *, tm, sub, rows_per_step, n_steps):
    i = pl.program_id(0)
    step = pl.program_id(1)
    nused = nu_ref[0]
    n_rows = rows_per_step * n_steps

    def row_copy(blk, r):
        tok = tok_ref[blk * tm + r]
        return pltpu.make_async_copy(x_hbm.at[pl.ds(tok, 1)], xbuf.at[pl.ds(r, 1)], sem.at[0])

    def for_rows(fn):
        def body(r, c):
            fn(r)
            return c
        lax.fori_loop(0, n_rows, body, 0, unroll=4)

    @pl.when(step == 0)
    def _():
        o_ref[...] = jnp.zeros(o_ref.shape, F32)

        @pl.when(i == 0)
        def _():
            for_rows(lambda r: row_copy(0, r).start())

        @pl.when(i <= nused)
        def _():
            for_rows(lambda r: row_copy(i, r).wait())

    @pl.when(i < nused)
    def _():
        @pl.when(step == 0)
        def _():
            half = xbuf.shape[1]
            xb[:, :half], xb[:, half:] = _unpack_bf16_pairs(xbuf[0:tm, :])

        for j in range(rows_per_step):
            row_copy(i + 1, step * rows_per_step + j).start()

        def sub_block(r):
            rows = pl.ds(r * sub, sub)
            _swiglu_step(xb[rows, :], wg_ref, wu_ref, wd_ref, o_ref, rows)

        sub_block(0)
        for r in range(1, tm // sub):
            pl.when(r < ns_ref[i])(functools.partial(sub_block, r))


def moe_experts(xn, slot_tok, block_e, block_nsub, nused, wg, wu, wd, *, tm, sub, tf):
    D = wg.shape[1]
    assert xn.shape[1] * 2 == D and xn.dtype == jnp.uint32
    P = slot_tok.shape[0]
    F = wg.shape[2]
    nf = n_steps = F // tf
    rows_per_step = -(-tm // n_steps)
    spare = rows_per_step * n_steps - tm
    assert (tm + spare) % 4 == 0
    slot_tok = jnp.concatenate([slot_tok, jnp.zeros((spare,), jnp.int32)])
    xbuf_rows = -(-(tm + spare) // SUBLANE) * SUBLANE

    def live(i, f, nu):
        return jnp.minimum(i, nu[0] - 1), jnp.where(i < nu[0], f, nf - 1)

    def up_map(i, f, tok, be, ns, nu):
        ii, ff = live(i, f, nu)
        return be[ii], 0, ff

    def down_map(i, f, tok, be, ns, nu):
        ii, ff = live(i, f, nu)
        return be[ii], ff, 0

    return pl.pallas_call(
        functools.partial(_moe_kernel, tm=tm, sub=sub, rows_per_step=rows_per_step, n_steps=n_steps),
        grid_spec=pltpu.PrefetchScalarGridSpec(
            num_scalar_prefetch=4,
            grid=(P // tm, nf),
            in_specs=[
                pl.BlockSpec(memory_space=pl.ANY),
                pl.BlockSpec((1, D, tf), up_map),
                pl.BlockSpec((1, D, tf), up_map),
                pl.BlockSpec((1, tf, D), down_map),
            ],
            out_specs=pl.BlockSpec((tm, D), lambda i, f, tok, be, ns, nu: (i, 0)),
            scratch_shapes=[
                pltpu.VMEM((xbuf_rows, D // 2), jnp.uint32),
                pltpu.VMEM((tm, D), BF16),
                pltpu.SemaphoreType.DMA((1,)),
            ],
        ),
        out_shape=jax.ShapeDtypeStruct((P, D), F32),
        compiler_params=_cparams("arbitrary", "arbitrary"),
        name="moe_experts",
    )(slot_tok, block_e, block_nsub, nused, xn, wg, wu, wd)


def _combine_kernel(pos_ref, h_ref, route_ref, g_ref, ys_hbm, o_ref, buf_a, buf_b, sem, *, tm, n_blocks):
    i = pl.program_id(0)
    nxt = jnp.where(i + 1 < n_blocks, i + 1, 0)

    def row_copy(blk, buf, parity, r, k):
        p = pos_ref[2 * (blk * tm + r) + k]
        return pltpu.make_async_copy(ys_hbm.at[pl.ds(p, 1)], buf.at[k, pl.ds(r, 1)], sem.at[parity, k])

    def for_rows(fn):
        def body(r, c):
            fn(r, 0)
            fn(r, 1)
            return c
        lax.fori_loop(0, tm, body, 0, unroll=8)

    @pl.when(i == 0)
    def _():
        for_rows(lambda r, k: row_copy(0, buf_a, 0, r, k).start())

    def step(parity, cur, other):
        for_rows(lambda r, k: row_copy(i, cur, parity, r, k).wait())
        for r in range(tm):
            row_copy(nxt, other, 1 - parity, r, 0).start()
            row_copy(nxt, other, 1 - parity, r, 1).start()
        route = route_ref[...]
        y = route[:, 2:3] * cur[0] + route[:, 3:4] * cur[1]
        o_ref[...] = _rms(h_ref[...] + y, g_ref[...])

        @pl.when(i == n_blocks - 1)
        def _():
            for_rows(lambda r, k: row_copy(0, other, 1 - parity, r, k).wait())

    pl.when(i % 2 == 0)(functools.partial(step, 0, buf_a, buf_b))
    pl.when(i % 2 == 1)(functools.partial(step, 1, buf_b, buf_a))


def moe_combine_norm(h, route, ys, pos, g, *, tm):
    T, D = h.shape
    tm = min(tm, T)
    return pl.pallas_call(
        functools.partial(_combine_kernel, tm=tm, n_blocks=T // tm),
        grid_spec=pltpu.PrefetchScalarGridSpec(
            num_scalar_prefetch=1,
            grid=(T // tm,),
            in_specs=[
                pl.BlockSpec((tm, D), lambda i, pos: (i, 0)),
                pl.BlockSpec((tm, LANE), lambda i, pos: (i, 0)),
                pl.BlockSpec((1, D), lambda i, pos: (0, 0)),
                pl.BlockSpec(memory_space=pl.ANY),
            ],
            out_specs=pl.BlockSpec((tm, D), lambda i, pos: (i, 0)),
            scratch_shapes=[pltpu.VMEM((2, tm, D), F32), pltpu.VMEM((2, tm, D), F32),
                            pltpu.SemaphoreType.DMA((2, 2))],
        ),
        out_shape=jax.ShapeDtypeStruct((T, D), F32),
        compiler_params=_cparams("arbitrary"),
        name="moe_combine_norm",
    )(pos, h, route, g.reshape(1, D), ys)


def _dispatch(route, tm, sub):
    T = route.shape[0]
    A = 2 * T
    P = A + N_EXPERTS * tm
    nblk = P // tm
    flat_e = route[:, :2].astype(jnp.int32).reshape(A)
    onehot = (flat_e[:, None] == jnp.arange(N_EXPERTS, dtype=jnp.int32)[None, :]).astype(jnp.int32)
    csum = jnp.cumsum(onehot, axis=0)
    counts = csum[-1]
    rank = jnp.sum(csum * onehot, axis=1) - 1
    padded = ((counts + tm - 1) // tm) * tm
    pend = jnp.cumsum(padded)
    pstart = pend - padded
    dest = (jnp.sum(onehot * pstart[None, :], axis=1) + rank).astype(jnp.int32)
    slot_tok = jnp.zeros((P,), jnp.int32).at[dest].set(jnp.arange(A, dtype=jnp.int32) // 2)
    blk_row0 = jnp.arange(nblk, dtype=jnp.int32) * tm
    block_e = jnp.minimum(jnp.searchsorted(pend, blk_row0, side="right"), N_EXPERTS - 1).astype(jnp.int32)
    live_end = pstart + ((counts + sub - 1) // sub) * sub
    block_nsub = (jnp.clip(live_end[block_e] - blk_row0, 0, tm) // sub).astype(jnp.int32)
    nused = (pend[-1] // tm).astype(jnp.int32).reshape(1)
    return slot_tok, dest, block_e, block_nsub, nused


def _rope_slabs(seq):
    pos = jnp.arange(seq, dtype=F32)
    inv = jnp.power(ROPE_THETA, -jnp.arange(0, MLA_ROPE, 2, dtype=F32) / MLA_ROPE)
    ang = pos[:, None] * inv[None, :]
    z = jnp.zeros((seq, LANE - MLA_ROPE), F32)
    ct = jnp.concatenate([jnp.cos(ang), jnp.cos(ang), z], axis=1)
    st = jnp.concatenate([jnp.sin(ang), jnp.sin(ang), z], axis=1)
    return ct, st


def _t5_bucket(rel):
    half = REL_BUCKETS // 2
    max_exact = half // 2
    ret = (rel > 0).astype(jnp.int32) * half
    n = jnp.abs(rel)
    nf = jnp.maximum(n, 1).astype(F32)
    large = max_exact + (jnp.log(nf / max_exact) / math.log(REL_MAX_DIST / max_exact)
                         * (half - max_exact)).astype(jnp.int32)
    large = jnp.minimum(large, half - 1)
    return ret + jnp.where(n < max_exact, n, large)


def _bias_tiles(rel_bias, tq):
    assert tq >= REL_MAX_DIST
    qi = jnp.arange(tq, dtype=jnp.int32)[:, None]
    ki = jnp.arange(tq, dtype=jnp.int32)[None, :]

    def lookup(rel):
        bucket = _t5_bucket(rel)[None]
        out = jnp.zeros((rel_bias.shape[1],) + rel.shape, F32)
        for b in range(REL_BUCKETS):
            out = jnp.where(bucket == b, rel_bias[b][:, None, None], out)
        return out

    far = lookup(jnp.full((1, 1), -2 * tq, jnp.int32))
    prev = lookup(ki - qi - tq) - far
    diag = jnp.where(((ki // CHUNK) <= (qi // CHUNK))[None], lookup(ki - qi) - far, NEG)
    return jnp.stack([prev, diag], axis=1)


def kernel(x, mem, rel_bias, mem_norm_g, norm_mix_g, norm_cross_g, norm_ffn_g, cross_wq, cross_wkv, cross_wo, ev_w_in, ev_conv_w, ev_conv_b, ev_ln_g, ev_ln_b, ev_q_norm_g, ev_w_uq, ev_kv_norm_g, ev_w_ukv, ev_w_out, ev_ffn_wg, ev_ffn_wu, ev_ffn_wd, od_w_in, od_lambda_q1, od_lambda_k1, od_lambda_q2, od_lambda_k2, od_subln_g, od_w_out, od_router, od_moe_wg, od_moe_wu, od_moe_wd, final_norm_g):
    B, S, D = x.shape
    T = B * S
    M = mem.shape[1]
    AW = ev_conv_w.shape[2]
    H = MLA_HEADS
    R = MLA_RANK
    h = x.reshape(T, D)

    kr0 = 2 * AW + 2 * R
    half = MLA_ROPE // 2
    w_kr = ev_w_in[0][:, kr0:kr0 + MLA_ROPE]
    w_kr = jnp.concatenate([w_kr, -w_kr[:, half:], w_kr[:, :half]], axis=1).astype(BF16)

    q_scale = (MLA_NOPE + MLA_ROPE) ** -0.5 * LOG2E
    wuq = (ev_w_uq[0] * q_scale).reshape(R, H, MLA_NOPE + MLA_ROPE)
    w_nope, w_r1, w_r2 = wuq[..., :MLA_NOPE], wuq[..., MLA_NOPE:MLA_NOPE + half], wuq[..., MLA_NOPE + half:]
    zq = jnp.zeros((R, H, MLA_QK_PAD - MLA_NOPE - MLA_ROPE), F32)
    wqa = jnp.concatenate([w_nope, w_r1, w_r2, zq], axis=-1).reshape(R, H * MLA_QK_PAD).astype(BF16)
    wqb = jnp.concatenate([-w_r2, w_r1, zq], axis=-1).reshape(R, H * LANE).astype(BF16)
    wukv = ev_w_ukv[0].reshape(R, H, MLA_NOPE + MLA_V)
    wk = wukv[..., :MLA_NOPE].reshape(R, H * MLA_NOPE).astype(BF16)
    wv = wukv[..., MLA_NOPE:].reshape(R, H * MLA_V).astype(BF16)
    ct, st = _rope_slabs(S)

    c_scale = CROSS_HD ** -0.5
    wq_c = (cross_wq * c_scale).astype(BF16)
    wkv_c = jnp.concatenate([cross_wkv[0], cross_wkv[1]], axis=1).astype(BF16)
    wo_c = cross_wo.astype(BF16)

    d_scale = DIFF_HD ** -0.5 * LOG2E
    layer = 1
    lambda_init = 0.8 - 0.6 * math.exp(-0.3 * layer)
    lam = (jnp.exp(jnp.sum(od_lambda_q1[0] * od_lambda_k1[0]))
           - jnp.exp(jnp.sum(od_lambda_q2[0] * od_lambda_k2[0])) + lambda_init).reshape(1).astype(F32)
    tq = min(256, S)
    bias_tiles = _bias_tiles(rel_bias * LOG2E, tq)

    kv_mem = norm_matmul(mem.reshape(B * M, D), mem_norm_g, wkv_c, tm=512, tn=512)
    kv_mem = kv_mem.reshape(B, M, -1)

    z, z_kr = norm_matmul(h, norm_mix_g[0], ev_w_in[0].astype(BF16), tm=1024, tn=1024, n_cols=kr0, side_w=w_kr)
    a_out = conformer_conv(z.reshape(B, S, -1), ev_conv_w[0], ev_conv_b[0], ev_ln_g[0], ev_ln_b[0], ts=256)
    q, k, v = mla_proj(z, z_kr, ev_q_norm_g[0], ev_kv_norm_g[0], wqa, wqb, wk, wv, ct, st, seq=S, tm=512)
    b_out = mla_attention(q.reshape(B, S, -1), k.reshape(B, S, -1), v.reshape(B, S, -1), tq=tq)
    h = matmul_res(h, [a_out.reshape(T, AW), b_out.reshape(T, H * MLA_V)], ev_w_out[0], tm=1024, tn=1024)
    h = cross_attention(h.reshape(B, S, D), norm_cross_g[0], wq_c[0], kv_mem, wo_c[0], kv_blk=0, tm=512)
    h = dense_ffn(h.reshape(T, D), norm_ffn_g[0], ev_ffn_wg[0], ev_ffn_wu[0], ev_ffn_wd[0], tm=1024, tf=256)

    qkv = norm_matmul(h, norm_mix_g[1], od_w_in[0].astype(BF16), tm=1024, tn=1024, scaled_cols=D, scale=d_scale)
    o = diff_attention(qkv.reshape(B, S, -1), lam, bias_tiles, od_subln_g[0], tq=tq,
                       out_scale=1.0 - lambda_init)
    h = matmul_res(h, [o.reshape(T, D)], od_w_out[0], tm=1024, tn=1024)
    h, xn, route = cross_attention(h.reshape(B, S, D), norm_cross_g[1], wq_c[1], kv_mem, wo_c[1], kv_blk=1,
                                   tm=512, router=(norm_ffn_g[1], od_router[0]))
    h, xn, route = h.reshape(T, D), xn.reshape(T, D // 2), route.reshape(T, LANE)

    tm_moe, sub_moe = 1024, 512
    slot_tok, dest, block_e, block_nsub, nused = _dispatch(route, tm_moe, sub_moe)
    ys = moe_experts(xn, slot_tok, block_e, block_nsub, nused, od_moe_wg[0], od_moe_wu[0], od_moe_wd[0],
                     tm=tm_moe, sub=sub_moe, tf=512)
    out = moe_combine_norm(h, route, ys, dest, final_norm_g, tm=256)
    return out.reshape(B, S, D)
```

```python
import functools
import math

import jax
import jax.numpy as jnp
from jax import lax
from jax.experimental import pallas as pl
from jax.experimental.pallas import tpu as pltpu

F32 = jnp.float32
BF16 = jnp.bfloat16

EPS = 1e-6
NEG = -1e30
LOG2E = math.log2(math.e)
CHUNK = 64

CONV_WIDTH = 31
CONV_HALO = 32
MLA_HEADS = 8
MLA_NOPE = 128
MLA_ROPE = 64
MLA_V = 128
MLA_RANK = 512
MLA_QK_PAD = 256
ROPE_THETA = 10000.0
DIFF_HEADS = 8
DIFF_HD = 128
REL_BUCKETS = 32
REL_MAX_DIST = 128
CROSS_HEADS = 4
CROSS_HD = 128
N_EXPERTS = 8
LANE = 128
SUBLANE = 8

VMEM_LIMIT = 60 * 1024 * 1024

NT_DIMS = (((1,), (1,)), ((), ()))


def _cparams(*sem):
    return pltpu.CompilerParams(dimension_semantics=sem, vmem_limit_bytes=VMEM_LIMIT)


def _rms(x, g):
    return x * lax.rsqrt(jnp.mean(x * x, axis=-1, keepdims=True) + EPS) * g


def _dot(a, b):
    return jnp.dot(a, b, preferred_element_type=F32)


def _pack_bf16_pairs(x):
    n = x.shape[1] // 2
    bits = lax.bitcast_convert_type(x.astype(BF16).astype(F32), jnp.uint32)
    return (bits[:, :n] >> 16) | bits[:, n:]


def _unpack_bf16_pairs(w):
    lo = lax.bitcast_convert_type(w << 16, F32)
    hi = lax.bitcast_convert_type(w & jnp.uint32(0xFFFF0000), F32)
    return lo.astype(BF16), hi.astype(BF16)


def _load_bf16(w_ref):
    w = w_ref[0] if len(w_ref.shape) == 3 else w_ref[...]
    return w.astype(BF16)


def _norm_matmul_kernel(x_ref, g_ref, w_ref, *refs, scaled_tiles, scale, has_side):
    if has_side:
        ws_ref, o_ref, os_ref, xn_ref = refs
    else:
        o_ref, xn_ref = refs
    j = pl.program_id(1)

    @pl.when(j == 0)
    def _():
        xn_ref[...] = _rms(x_ref[...], g_ref[...]).astype(BF16)
        if has_side:
            os_ref[...] = _dot(xn_ref[...], ws_ref[...]).astype(os_ref.dtype)

    acc = _dot(xn_ref[...], w_ref[...].astype(BF16))
    if scaled_tiles:
        acc = acc * jnp.where(j < scaled_tiles, scale, 1.0)
    o_ref[...] = acc.astype(o_ref.dtype)


def norm_matmul(x, g, w, *, tm, tn, n_cols=None, scaled_cols=0, scale=1.0, side_w=None):
    M, K = x.shape
    N = n_cols or w.shape[1]
    tm = min(tm, M)
    assert scaled_cols % tn == 0 and N % tn == 0
    in_specs = [
        pl.BlockSpec((tm, K), lambda i, j: (i, 0)),
        pl.BlockSpec((1, K), lambda i, j: (0, 0)),
        pl.BlockSpec((K, tn), lambda i, j: (0, j)),
    ]
    out_specs = [pl.BlockSpec((tm, tn), lambda i, j: (i, j))]
    out_shape = [jax.ShapeDtypeStruct((M, N), BF16)]
    args = [x, g.reshape(1, K), w]
    if side_w is not None:
        ns = side_w.shape[1]
        in_specs.append(pl.BlockSpec((K, ns), lambda i, j: (0, 0)))
        out_specs.append(pl.BlockSpec((tm, ns), lambda i, j: (i, 0)))
        out_shape.append(jax.ShapeDtypeStruct((M, ns), BF16))
        args.append(side_w)
    outs = pl.pallas_call(
        functools.partial(_norm_matmul_kernel, scaled_tiles=scaled_cols // tn, scale=scale,
                          has_side=side_w is not None),
        grid=(M // tm, N // tn),
        in_specs=in_specs,
        out_specs=out_specs,
        out_shape=out_shape,
        scratch_shapes=[pltpu.VMEM((tm, K), BF16)],
        compiler_params=_cparams("parallel", "arbitrary"),
        name="norm_matmul",
    )(*args)
    return outs if side_w is not None else outs[0]


def _matmul_res_kernel(res_ref, *refs, n):
    a_refs, w_ref, o_ref, wb_ref = refs[:n], refs[n], refs[n + 1], refs[n + 2]

    @pl.when(pl.program_id(1) == 0)
    def _():
        wb_ref[...] = w_ref[...].astype(BF16)

    ka = a_refs[0].shape[1]
    acc = res_ref[...]
    for k in range(n):
        acc = acc + _dot(a_refs[k][...], wb_ref[k * ka:(k + 1) * ka, :])
    o_ref[...] = acc


def matmul_res(res, a_list, w, *, tm, tn):
    M, N = res.shape
    K = w.shape[0]
    tm = min(tm, M)
    n = len(a_list)
    ka = a_list[0].shape[1]
    assert all(a.shape[1] == ka for a in a_list) and K == n * ka
    in_specs = [pl.BlockSpec((tm, tn), lambda j, i: (i, j))]
    in_specs += [pl.BlockSpec((tm, ka), lambda j, i: (i, 0)) for _ in a_list]
    in_specs += [pl.BlockSpec((K, tn), lambda j, i: (0, j))]
    return pl.pallas_call(
        functools.partial(_matmul_res_kernel, n=n),
        grid=(N // tn, M // tm),
        in_specs=in_specs,
        out_specs=pl.BlockSpec((tm, tn), lambda j, i: (i, j)),
        out_shape=jax.ShapeDtypeStruct((M, N), F32),
        scratch_shapes=[pltpu.VMEM((K, tn), BF16)],
        compiler_params=_cparams("parallel", "arbitrary"),
        name="matmul_res",
    )(res, *a_list, w)


def _conv_kernel(val_ref, gate_ref, w_ref, b_ref, lg_ref, lb_ref, o_ref, ubuf, shifted, *, ts):
    s = pl.program_id(1)

    @pl.when(s == 0)
    def _():
        ubuf[0:CONV_HALO, :] = jnp.zeros((CONV_HALO, ubuf.shape[1]), F32)

    @pl.when(s > 0)
    def _():
        ubuf[0:CONV_HALO, :] = ubuf[ts:ts + CONV_HALO, :]

    val = val_ref[0].astype(F32)
    gate = gate_ref[0].astype(F32)
    ubuf[CONV_HALO:CONV_HALO + ts, :] = val * jax.nn.sigmoid(gate)

    span = ts + CONV_HALO - SUBLANE
    for k in range(1, SUBLANE):
        shifted[k - 1] = ubuf[k:k + span, :]

    base = CONV_HALO - (CONV_WIDTH - 1)
    acc = jnp.zeros((ts, ubuf.shape[1]), F32) + b_ref[...]
    for j in range(CONV_WIDTH):
        k = (base + j) % SUBLANE
        a = base + j - k
        window = ubuf[a:a + ts, :] if k == 0 else shifted[k - 1, a:a + ts, :]
        acc = acc + w_ref[j:j + 1, :] * window

    mu = jnp.mean(acc, axis=-1, keepdims=True)
    xc = acc - mu
    y = xc * lax.rsqrt(jnp.mean(xc * xc, axis=-1, keepdims=True) + EPS)
    y = y * lg_ref[...] + lb_ref[...]
    o_ref[0] = (y * jax.nn.sigmoid(y)).astype(o_ref.dtype)


def conformer_conv(z, conv_w, conv_b, ln_g, ln_b, *, ts):
    B, S, _ = z.shape
    C = conv_w.shape[1]
    ts = min(ts, S)
    wpad = jnp.zeros((CONV_HALO, C), F32).at[:CONV_WIDTH].set(conv_w)
    row = lambda v: v.reshape(1, C)
    const = lambda shape: pl.BlockSpec(shape, lambda b, s: (0, 0))
    return pl.pallas_call(
        functools.partial(_conv_kernel, ts=ts),
        grid=(B, S // ts),
        in_specs=[
            pl.BlockSpec((1, ts, C), lambda b, s: (b, s, 0)),
            pl.BlockSpec((1, ts, C), lambda b, s: (b, s, 1)),
            const((CONV_HALO, C)), const((1, C)), const((1, C)), const((1, C)),
        ],
        out_specs=pl.BlockSpec((1, ts, C), lambda b, s: (b, s, 0)),
        out_shape=jax.ShapeDtypeStruct((B, S, C), BF16),
        scratch_shapes=[pltpu.VMEM((ts + CONV_HALO, C), F32),
                        pltpu.VMEM((SUBLANE - 1, ts + CONV_HALO - SUBLANE, C), F32)],
        compiler_params=_cparams("parallel", "arbitrary"),
        name="conformer_conv",
    )(z, z, wpad, row(conv_b), row(ln_g), row(ln_b))


def _mla_proj_kernel(cq_ref, ckv_ref, kr_ref, qg_ref, kvg_ref, wqa_ref, wqb_ref, wk_ref, wv_ref,
                     ct_ref, st_ref, q_ref, k_ref, v_ref):
    cqn = _rms(cq_ref[...].astype(F32), qg_ref[...]).astype(BF16)
    ckvn = _rms(ckv_ref[...].astype(F32), kvg_ref[...]).astype(BF16)
    ct = ct_ref[...]
    st = st_ref[...]

    qa = _dot(cqn, wqa_ref[...])
    qb = _dot(cqn, wqb_ref[...])
    kn = _dot(ckvn, wk_ref[...])
    v_ref[...] = _dot(ckvn, wv_ref[...]).astype(BF16)

    kr = kr_ref[...].astype(F32)
    k_rope = (kr * ct + pltpu.roll(kr, LANE // 2, axis=1) * st).astype(BF16)

    for h in range(MLA_HEADS):
        o = h * MLA_QK_PAD
        q_ref[:, o:o + LANE] = qa[:, o:o + LANE].astype(BF16)
        q_ref[:, o + LANE:o + 2 * LANE] = (
            qa[:, o + LANE:o + 2 * LANE] * ct + qb[:, h * LANE:(h + 1) * LANE] * st).astype(BF16)
        k_ref[:, o:o + LANE] = kn[:, h * LANE:(h + 1) * LANE].astype(BF16)
        k_ref[:, o + LANE:o + 2 * LANE] = k_rope


def mla_proj(z, z_kr, q_norm_g, kv_norm_g, wqa, wqb, wk, wv, ct, st, *, seq, tm):
    T = z.shape[0]
    tm = min(tm, seq)
    n_s = seq // tm
    R = MLA_RANK
    cq_blk = (2 * 1024) // R
    const = lambda shape: pl.BlockSpec(shape, lambda i: (0, 0))
    HQ = MLA_HEADS * MLA_QK_PAD
    HV = MLA_HEADS * MLA_V
    return pl.pallas_call(
        _mla_proj_kernel,
        grid=(T // tm,),
        in_specs=[
            pl.BlockSpec((tm, R), lambda i: (i, cq_blk)),
            pl.BlockSpec((tm, R), lambda i: (i, cq_blk + 1)),
            pl.BlockSpec((tm, LANE), lambda i: (i, 0)),
            const((1, R)), const((1, R)),
            const((R, HQ)), const((R, MLA_HEADS * LANE)), const((R, HV)), const((R, HV)),
            pl.BlockSpec((tm, LANE), lambda i: (i % n_s, 0)),
            pl.BlockSpec((tm, LANE), lambda i: (i % n_s, 0)),
        ],
        out_specs=[
            pl.BlockSpec((tm, HQ), lambda i: (i, 0)),
            pl.BlockSpec((tm, HQ), lambda i: (i, 0)),
            pl.BlockSpec((tm, HV), lambda i: (i, 0)),
        ],
        out_shape=[
            jax.ShapeDtypeStruct((T, HQ), BF16),
            jax.ShapeDtypeStruct((T, HQ), BF16),
            jax.ShapeDtypeStruct((T, HV), BF16),
        ],
        compiler_params=_cparams("parallel"),
        name="mla_proj",
    )(z, z, z_kr, q_norm_g.reshape(1, R), kv_norm_g.reshape(1, R), wqa, wqb, wk, wv, ct, st)


def _softmax_pv(s, v):
    m = jnp.max(s, axis=-1, keepdims=True)
    p = jnp.exp2(s - m)
    l = jnp.sum(p, axis=-1, keepdims=True)
    return _dot(p.astype(BF16), v) / l


def _mla_attn_kernel(q_ref, k_ref, v_ref, o_ref, *, tq, n_tiles):
    qc = lax.broadcasted_iota(jnp.int32, (tq, tq), 0) // CHUNK
    kc = lax.broadcasted_iota(jnp.int32, (tq, tq), 1) // CHUNK
    visible = kc <= qc

    for c in range(n_tiles):
        n = (c + 1) * tq
        rows = slice(c * tq, n)
        s = lax.dot_general(q_ref[0, rows, :], k_ref[0, :n, :], NT_DIMS, preferred_element_type=F32)
        parts = [s[:, :c * tq]] if c else []
        parts.append(jnp.where(visible, s[:, c * tq:], NEG))
        s = jnp.concatenate(parts, axis=1) if c else parts[0]
        o_ref[0, rows, :] = _softmax_pv(s, v_ref[0, :n, :]).astype(o_ref.dtype)


def mla_attention(q, k, v, *, tq):
    B, S, _ = q.shape
    tq = min(tq, S)
    head = lambda b, h: (b, 0, h)
    return pl.pallas_call(
        functools.partial(_mla_attn_kernel, tq=tq, n_tiles=S // tq),
        grid=(B, MLA_HEADS),
        in_specs=[
            pl.BlockSpec((1, S, MLA_QK_PAD), head),
            pl.BlockSpec((1, S, MLA_QK_PAD), head),
            pl.BlockSpec((1, S, MLA_V), head),
        ],
        out_specs=pl.BlockSpec((1, S, MLA_V), head),
        out_shape=jax.ShapeDtypeStruct((B, S, MLA_HEADS * MLA_V), BF16),
        compiler_params=_cparams("parallel", "parallel"),
        name="mla_attention",
    )(q, k, v)


def _diff_attn_kernel(lam_ref, q_ref, k_ref, v_ref, bias_ref, g_ref, o_ref, *, tq, n_tiles, out_scale):
    lam = lam_ref[0]

    for c in range(n_tiles):
        n = (c + 1) * tq
        rows = slice(c * tq, n)
        v = v_ref[0, :n, :]

        def half(lo):
            s = lax.dot_general(q_ref[0, rows, lo:lo + DIFF_HD], k_ref[0, :n, lo:lo + DIFF_HD], NT_DIMS,
                                preferred_element_type=F32)
            parts = []
            if c >= 2:
                parts.append(s[:, :(c - 1) * tq])
            if c >= 1:
                parts.append(s[:, (c - 1) * tq:c * tq] + bias_ref[0, 0])
            parts.append(s[:, c * tq:] + bias_ref[0, 1])
            s = jnp.concatenate(parts, axis=1) if c else parts[0]
            return _softmax_pv(s, v)

        o = half(0) - lam * half(DIFF_HD)
        o_ref[0, rows, :] = (_rms(o, g_ref[...]) * out_scale).astype(o_ref.dtype)


def diff_attention(qkv, lam, bias_tiles, subln_g, *, tq, out_scale):
    B, S, _ = qkv.shape
    H = DIFF_HEADS
    dv = 2 * DIFF_HD
    smem = pl.BlockSpec(memory_space=pltpu.SMEM)
    return pl.pallas_call(
        functools.partial(_diff_attn_kernel, tq=tq, n_tiles=S // tq, out_scale=out_scale),
        grid=(B, H),
        in_specs=[
            smem,
            pl.BlockSpec((1, S, dv), lambda b, h: (b, 0, h)),
            pl.BlockSpec((1, S, dv), lambda b, h: (b, 0, H + h)),
            pl.BlockSpec((1, S, dv), lambda b, h: (b, 0, 2 * H + h)),
            pl.BlockSpec((1, 2, tq, tq), lambda b, h: (h, 0, 0, 0)),
            pl.BlockSpec((1, dv), lambda b, h: (0, 0)),
        ],
        out_specs=pl.BlockSpec((1, S, dv), lambda b, h: (b, 0, h)),
        out_shape=jax.ShapeDtypeStruct((B, S, H * dv), BF16),
        compiler_params=_cparams("parallel", "parallel"),
        name="diff_attention",
    )(lam, qkv, qkv, qkv, bias_tiles, subln_g.reshape(1, dv))


def _cross_kernel(h_ref, g_ref, wq_ref, kv_ref, wo_ref, *rest, with_router):
    if with_router:
        rg_ref, wr_ref, o_ref, xn_ref, route_ref = rest
    else:
        (o_ref,) = rest
    h = h_ref[0]
    hn = _rms(h, g_ref[...]).astype(BF16)
    q = _dot(hn, wq_ref[...]).astype(BF16)
    kv = kv_ref[0]
    HD = CROSS_HEADS * CROSS_HD
    outs = []
    for hd in range(CROSS_HEADS):
        lo = hd * CROSS_HD
        s = lax.dot_general(q[:, lo:lo + CROSS_HD], kv[:, lo:lo + CROSS_HD], NT_DIMS,
                            preferred_element_type=F32)
        m = jnp.max(s, axis=-1, keepdims=True)
        p = jnp.exp(s - m)
        l = jnp.sum(p, axis=-1, keepdims=True)
        o = _dot(p.astype(BF16), kv[:, HD + lo:HD + lo + CROSS_HD]) / l
        outs.append(o.astype(BF16))
    o_all = jnp.concatenate(outs, axis=-1)
    h_new = h + _dot(o_all, wo_ref[...])
    o_ref[0] = h_new
    if with_router:
        xn, route_ref[0] = _route(h_new, rg_ref[...], wr_ref[...])
        xn_ref[0] = _pack_bf16_pairs(xn)


def cross_attention(h, g, wq, kv, wo, *, kv_blk, tm, router=None):
    B, S, D = h.shape
    M = kv.shape[1]
    HD = CROSS_HEADS * CROSS_HD
    tm = min(tm, S)
    const = lambda shape: pl.BlockSpec(shape, lambda b, s: (0, 0))
    tile = lambda width: pl.BlockSpec((1, tm, width), lambda b, s: (b, s, 0))
    in_specs = [tile(D), const((1, D)), const((D, HD)),
                pl.BlockSpec((1, M, 2 * HD), lambda b, s: (b, 0, kv_blk)), const((HD, D))]
    args = [h, g.reshape(1, D), wq, kv, wo]
    out_specs = [tile(D)]
    out_shape = [jax.ShapeDtypeStruct((B, S, D), F32)]
    if router is not None:
        rg, w_router = router
        in_specs += [const((1, D)), const((D, LANE))]
        args += [rg.reshape(1, D), jnp.zeros((D, LANE), BF16).at[:, :N_EXPERTS].set(w_router.astype(BF16))]
        out_specs += [tile(D // 2), tile(LANE)]
        out_shape += [jax.ShapeDtypeStruct((B, S, D // 2), jnp.uint32),
                      jax.ShapeDtypeStruct((B, S, LANE), F32)]
    outs = pl.pallas_call(
        functools.partial(_cross_kernel, with_router=router is not None),
        grid=(B, S // tm),
        in_specs=in_specs,
        out_specs=out_specs,
        out_shape=out_shape,
        compiler_params=_cparams("parallel", "arbitrary"),
        name="cross_attention",
    )(*args)
    return outs if router is not None else outs[0]


def _swiglu_step(x, wg_ref, wu_ref, wd_ref, o_ref, rows=slice(None)):
    gt = _dot(x, _load_bf16(wg_ref))
    up = _dot(x, _load_bf16(wu_ref))
    hm = (gt * jax.nn.sigmoid(gt) * up).astype(BF16)
    o_ref[rows, :] += _dot(hm, _load_bf16(wd_ref))


def _ffn_kernel(h_ref, g_ref, wg_ref, wu_ref, wd_ref, o_ref, xn_ref):
    @pl.when(pl.program_id(1) == 0)
    def _():
        xn_ref[...] = _rms(h_ref[...], g_ref[...]).astype(BF16)
        o_ref[...] = h_ref[...]

    _swiglu_step(xn_ref[...], wg_ref, wu_ref, wd_ref, o_ref)


def dense_ffn(h, g, wg, wu, wd, *, tm, tf):
    T, D = h.shape
    F = wg.shape[1]
    tm = min(tm, T)
    return pl.pallas_call(
        _ffn_kernel,
        grid=(T // tm, F // tf),
        in_specs=[
            pl.BlockSpec((tm, D), lambda i, f: (i, 0)),
            pl.BlockSpec((1, D), lambda i, f: (0, 0)),
            pl.BlockSpec((D, tf), lambda i, f: (0, f)),
            pl.BlockSpec((D, tf), lambda i, f: (0, f)),
            pl.BlockSpec((tf, D), lambda i, f: (f, 0)),
        ],
        out_specs=pl.BlockSpec((tm, D), lambda i, f: (i, 0)),
        out_shape=jax.ShapeDtypeStruct((T, D), F32),
        scratch_shapes=[pltpu.VMEM((tm, D), BF16)],
        compiler_params=_cparams("parallel", "arbitrary"),
        name="dense_ffn",
    )(h, g.reshape(1, D), wg, wu, wd)


def _route(h, g, wr):
    xn = _rms(h, g)
    logits = _dot(xn.astype(BF16), wr)
    lane = lax.broadcasted_iota(jnp.int32, logits.shape, 1)
    logits = jnp.where(lane < N_EXPERTS, logits, -jnp.inf)
    v1 = jnp.max(logits, axis=-1, keepdims=True)
    i1 = jnp.min(jnp.where(logits == v1, lane, LANE), axis=-1, keepdims=True)
    rest = jnp.where(lane == i1, -jnp.inf, logits)
    v2 = jnp.max(rest, axis=-1, keepdims=True)
    i2 = jnp.min(jnp.where(rest == v2, lane, LANE), axis=-1, keepdims=True)
    e2 = jnp.exp(v2 - v1)
    g1 = 1.0 / (1.0 + e2)
    g2 = e2 / (1.0 + e2)
    route = jnp.where(lane == 0, i1.astype(F32), 0.0)
    route = jnp.where(lane == 1, i2.astype(F32), route)
    route = jnp.where(lane == 2, g1, route)
    route = jnp.where(lane == 3, g2, route)
    return xn, route


def _moe_kernel(tok_ref, be_ref, ns_ref, nu_ref, x_hbm, wg_ref, wu_ref, wd_ref, o_ref, xbuf, xb, sem,
                *, tm, sub, rows_per_step, n_steps):
    i = pl.program_id(0)
    step = pl.program_id(1)
    nused = nu_ref[0]
    n_rows = rows_per_step * n_steps

    def row_copy(blk, r):
        tok = tok_ref[blk * tm + r]
        return pltpu.make_async_copy(x_hbm.at[pl.ds(tok, 1)], xbuf.at[pl.ds(r, 1)], sem.at[0])

    def for_rows(fn):
        def body(r, c):
            fn(r)
            return c
        lax.fori_loop(0, n_rows, body, 0, unroll=4)

    @pl.when(step == 0)
    def _():
        o_ref[...] = jnp.zeros(o_ref.shape, F32)

        @pl.when(i == 0)
        def _():
            for_rows(lambda r: row_copy(0, r).start())

        @pl.when(i <= nused)
        def _():
            for_rows(lambda r: row_copy(i, r).wait())

    @pl.when(i < nused)
    def _():
        @pl.when(step == 0)
        def _():
            half = xbuf.shape[1]
            xb[:, :half], xb[:, half:] = _unpack_bf16_pairs(xbuf[0:tm, :])

        for j in range(rows_per_step):
            row_copy(i + 1, step * rows_per_step + j).start()

        def row_range(start, size):
            rows = pl.ds(start, size)
            _swiglu_step(xb[rows, :], wg_ref, wu_ref, wd_ref, o_ref, rows)

        units = tm // sub
        assert units == 4
        row_range(0, 2 * sub)
        pl.when(ns_ref[i] == units)(functools.partial(row_range, 2 * sub, 2 * sub))
        pl.when(ns_ref[i] == units - 1)(functools.partial(row_range, 2 * sub, sub))


def moe_experts(xn, slot_tok, block_e, block_nsub, nused, wg, wu, wd, *, tm, sub, tf):
    D = wg.shape[1]
    assert xn.shape[1] * 2 == D and xn.dtype == jnp.uint32
    P = slot_tok.shape[0]
    F = wg.shape[2]
    nf = n_steps = F // tf
    rows_per_step = -(-tm // n_steps)
    spare = rows_per_step * n_steps - tm
    assert (tm + spare) % 4 == 0
    slot_tok = jnp.concatenate([slot_tok, jnp.zeros((spare,), jnp.int32)])
    xbuf_rows = -(-(tm + spare) // SUBLANE) * SUBLANE

    def live(i, f, nu):
        return jnp.minimum(i, nu[0] - 1), jnp.where(i < nu[0], f, nf - 1)

    def up_map(i, f, tok, be, ns, nu):
        ii, ff = live(i, f, nu)
        return be[ii], 0, ff

    def down_map(i, f, tok, be, ns, nu):
        ii, ff = live(i, f, nu)
        return be[ii], ff, 0

    return pl.pallas_call(
        functools.partial(_moe_kernel, tm=tm, sub=sub, rows_per_step=rows_per_step, n_steps=n_steps),
        grid_spec=pltpu.PrefetchScalarGridSpec(
            num_scalar_prefetch=4,
            grid=(P // tm, nf),
            in_specs=[
                pl.BlockSpec(memory_space=pl.ANY),
                pl.BlockSpec((1, D, tf), up_map),
                pl.BlockSpec((1, D, tf), up_map),
                pl.BlockSpec((1, tf, D), down_map),
            ],
            out_specs=pl.BlockSpec((tm, D), lambda i, f, tok, be, ns, nu: (i, 0)),
            scratch_shapes=[
                pltpu.VMEM((xbuf_rows, D // 2), jnp.uint32),
                pltpu.VMEM((tm, D), BF16),
                pltpu.SemaphoreType.DMA((1,)),
            ],
        ),
        out_shape=jax.ShapeDtypeStruct((P, D), F32),
        compiler_params=_cparams("arbitrary", "arbitrary"),
        name="moe_experts",
    )(slot_tok, block_e, block_nsub, nused, xn, wg, wu, wd)


def _combine_kernel(pos_ref, h_ref, route_ref, g_ref, ys_hbm, o_ref, buf_a, buf_b, sem, *, tm, n_blocks):
    i = pl.program_id(0)
    nxt = jnp.where(i + 1 < n_blocks, i + 1, 0)

    def row_copy(blk, buf, parity, r, k):
        p = pos_ref[2 * (blk * tm + r) + k]
        return pltpu.make_async_copy(ys_hbm.at[pl.ds(p, 1)], buf.at[k, pl.ds(r, 1)], sem.at[parity, k])

    def for_rows(fn):
        def body(r, c):
            fn(r, 0)
            fn(r, 1)
            return c
        lax.fori_loop(0, tm, body, 0, unroll=8)

    @pl.when(i == 0)
    def _():
        for_rows(lambda r, k: row_copy(0, buf_a, 0, r, k).start())

    def step(parity, cur, other):
        for_rows(lambda r, k: row_copy(i, cur, parity, r, k).wait())
        for r in range(tm):
            row_copy(nxt, other, 1 - parity, r, 0).start()
            row_copy(nxt, other, 1 - parity, r, 1).start()
        route = route_ref[...]
        y = route[:, 2:3] * cur[0] + route[:, 3:4] * cur[1]
        o_ref[...] = _rms(h_ref[...] + y, g_ref[...])

        @pl.when(i == n_blocks - 1)
        def _():
            for_rows(lambda r, k: row_copy(0, other, 1 - parity, r, k).wait())

    pl.when(i % 2 == 0)(functools.partial(step, 0, buf_a, buf_b))
    pl.when(i % 2 == 1)(functools.partial(step, 1, buf_b, buf_a))


def moe_combine_norm(h, route, ys, pos, g, *, tm):
    T, D = h.shape
    tm = min(tm, T)
    return pl.pallas_call(
        functools.partial(_combine_kernel, tm=tm, n_blocks=T // tm),
        grid_spec=pltpu.PrefetchScalarGridSpec(
            num_scalar_prefetch=1,
            grid=(T // tm,),
            in_specs=[
                pl.BlockSpec((tm, D), lambda i, pos: (i, 0)),
                pl.BlockSpec((tm, LANE), lambda i, pos: (i, 0)),
                pl.BlockSpec((1, D), lambda i, pos: (0, 0)),
                pl.BlockSpec(memory_space=pl.ANY),
            ],
            out_specs=pl.BlockSpec((tm, D), lambda i, pos: (i, 0)),
            scratch_shapes=[pltpu.VMEM((2, tm, D), F32), pltpu.VMEM((2, tm, D), F32),
                            pltpu.SemaphoreType.DMA((2, 2))],
        ),
        out_shape=jax.ShapeDtypeStruct((T, D), F32),
        compiler_params=_cparams("arbitrary"),
        name="moe_combine_norm",
    )(pos, h, route, g.reshape(1, D), ys)


def _dispatch(route, tm, sub):
    T = route.shape[0]
    A = 2 * T
    P = A + N_EXPERTS * tm
    nblk = P // tm
    flat_e = route[:, :2].astype(jnp.int32).reshape(A)
    onehot = (flat_e[:, None] == jnp.arange(N_EXPERTS, dtype=jnp.int32)[None, :]).astype(jnp.int32)
    csum = jnp.cumsum(onehot, axis=0)
    counts = csum[-1]
    rank = jnp.sum(csum * onehot, axis=1) - 1
    padded = ((counts + tm - 1) // tm) * tm
    pend = jnp.cumsum(padded)
    pstart = pend - padded
    dest = (jnp.sum(onehot * pstart[None, :], axis=1) + rank).astype(jnp.int32)
    slot_tok = jnp.zeros((P,), jnp.int32).at[dest].set(jnp.arange(A, dtype=jnp.int32) // 2)
    blk_row0 = jnp.arange(nblk, dtype=jnp.int32) * tm
    block_e = jnp.minimum(jnp.searchsorted(pend, blk_row0, side="right"), N_EXPERTS - 1).astype(jnp.int32)
    live_end = pstart + ((counts + sub - 1) // sub) * sub
    block_nsub = (jnp.clip(live_end[block_e] - blk_row0, 0, tm) // sub).astype(jnp.int32)
    nused = (pend[-1] // tm).astype(jnp.int32).reshape(1)
    return slot_tok, dest, block_e, block_nsub, nused


def _rope_slabs(seq):
    pos = jnp.arange(seq, dtype=F32)
    inv = jnp.power(ROPE_THETA, -jnp.arange(0, MLA_ROPE, 2, dtype=F32) / MLA_ROPE)
    ang = pos[:, None] * inv[None, :]
    z = jnp.zeros((seq, LANE - MLA_ROPE), F32)
    ct = jnp.concatenate([jnp.cos(ang), jnp.cos(ang), z], axis=1)
    st = jnp.concatenate([jnp.sin(ang), jnp.sin(ang), z], axis=1)
    return ct, st


def _t5_bucket(rel):
    half = REL_BUCKETS // 2
    max_exact = half // 2
    ret = (rel > 0).astype(jnp.int32) * half
    n = jnp.abs(rel)
    nf = jnp.maximum(n, 1).astype(F32)
    large = max_exact + (jnp.log(nf / max_exact) / math.log(REL_MAX_DIST / max_exact)
                         * (half - max_exact)).astype(jnp.int32)
    large = jnp.minimum(large, half - 1)
    return ret + jnp.where(n < max_exact, n, large)


def _bias_tiles(rel_bias, tq):
    assert tq >= REL_MAX_DIST
    qi = jnp.arange(tq, dtype=jnp.int32)[:, None]
    ki = jnp.arange(tq, dtype=jnp.int32)[None, :]

    def lookup(rel):
        bucket = _t5_bucket(rel)[None]
        out = jnp.zeros((rel_bias.shape[1],) + rel.shape, F32)
        for b in range(REL_BUCKETS):
            out = jnp.where(bucket == b, rel_bias[b][:, None, None], out)
        return out

    far = lookup(jnp.full((1, 1), -2 * tq, jnp.int32))
    prev = lookup(ki - qi - tq) - far
    diag = jnp.where(((ki // CHUNK) <= (qi // CHUNK))[None], lookup(ki - qi) - far, NEG)
    return jnp.stack([prev, diag], axis=1)


def kernel(x, mem, rel_bias, mem_norm_g, norm_mix_g, norm_cross_g, norm_ffn_g, cross_wq, cross_wkv, cross_wo, ev_w_in, ev_conv_w, ev_conv_b, ev_ln_g, ev_ln_b, ev_q_norm_g, ev_w_uq, ev_kv_norm_g, ev_w_ukv, ev_w_out, ev_ffn_wg, ev_ffn_wu, ev_ffn_wd, od_w_in, od_lambda_q1, od_lambda_k1, od_lambda_q2, od_lambda_k2, od_subln_g, od_w_out, od_router, od_moe_wg, od_moe_wu, od_moe_wd, final_norm_g):
    B, S, D = x.shape
    T = B * S
    M = mem.shape[1]
    AW = ev_conv_w.shape[2]
    H = MLA_HEADS
    R = MLA_RANK
    h = x.reshape(T, D)

    kr0 = 2 * AW + 2 * R
    half = MLA_ROPE // 2
    w_kr = ev_w_in[0][:, kr0:kr0 + MLA_ROPE]
    w_kr = jnp.concatenate([w_kr, -w_kr[:, half:], w_kr[:, :half]], axis=1).astype(BF16)

    q_scale = (MLA_NOPE + MLA_ROPE) ** -0.5 * LOG2E
    wuq = (ev_w_uq[0] * q_scale).reshape(R, H, MLA_NOPE + MLA_ROPE)
    w_nope, w_r1, w_r2 = wuq[..., :MLA_NOPE], wuq[..., MLA_NOPE:MLA_NOPE + half], wuq[..., MLA_NOPE + half:]
    zq = jnp.zeros((R, H, MLA_QK_PAD - MLA_NOPE - MLA_ROPE), F32)
    wqa = jnp.concatenate([w_nope, w_r1, w_r2, zq], axis=-1).reshape(R, H * MLA_QK_PAD).astype(BF16)
    wqb = jnp.concatenate([-w_r2, w_r1, zq], axis=-1).reshape(R, H * LANE).astype(BF16)
    wukv = ev_w_ukv[0].reshape(R, H, MLA_NOPE + MLA_V)
    wk = wukv[..., :MLA_NOPE].reshape(R, H * MLA_NOPE).astype(BF16)
    wv = wukv[..., MLA_NOPE:].reshape(R, H * MLA_V).astype(BF16)
    ct, st = _rope_slabs(S)

    c_scale = CROSS_HD ** -0.5
    wq_c = (cross_wq * c_scale).astype(BF16)
    wkv_c = jnp.concatenate([cross_wkv[0], cross_wkv[1]], axis=1).astype(BF16)
    wo_c = cross_wo.astype(BF16)

    d_scale = DIFF_HD ** -0.5 * LOG2E
    layer = 1
    lambda_init = 0.8 - 0.6 * math.exp(-0.3 * layer)
    lam = (jnp.exp(jnp.sum(od_lambda_q1[0] * od_lambda_k1[0]))
           - jnp.exp(jnp.sum(od_lambda_q2[0] * od_lambda_k2[0])) + lambda_init).reshape(1).astype(F32)
    tq = min(256, S)
    bias_tiles = _bias_tiles(rel_bias * LOG2E, tq)

    kv_mem = norm_matmul(mem.reshape(B * M, D), mem_norm_g, wkv_c, tm=512, tn=512)
    kv_mem = kv_mem.reshape(B, M, -1)

    z, z_kr = norm_matmul(h, norm_mix_g[0], ev_w_in[0].astype(BF16), tm=1024, tn=1024, n_cols=kr0, side_w=w_kr)
    a_out = conformer_conv(z.reshape(B, S, -1), ev_conv_w[0], ev_conv_b[0], ev_ln_g[0], ev_ln_b[0], ts=512)
    q, k, v = mla_proj(z, z_kr, ev_q_norm_g[0], ev_kv_norm_g[0], wqa, wqb, wk, wv, ct, st, seq=S, tm=1024)
    b_out = mla_attention(q.reshape(B, S, -1), k.reshape(B, S, -1), v.reshape(B, S, -1), tq=tq)
    h = matmul_res(h, [a_out.reshape(T, AW), b_out.reshape(T, H * MLA_V)], ev_w_out[0], tm=1024, tn=1024)
    h = cross_attention(h.reshape(B, S, D), norm_cross_g[0], wq_c[0], kv_mem, wo_c[0], kv_blk=0, tm=1024)
    h = dense_ffn(h.reshape(T, D), norm_ffn_g[0], ev_ffn_wg[0], ev_ffn_wu[0], ev_ffn_wd[0], tm=1024, tf=256)

    qkv = norm_matmul(h, norm_mix_g[1], od_w_in[0].astype(BF16), tm=1024, tn=1024, scaled_cols=D, scale=d_scale)
    o = diff_attention(qkv.reshape(B, S, -1), lam, bias_tiles, od_subln_g[0], tq=tq,
                       out_scale=1.0 - lambda_init)
    h = matmul_res(h, [o.reshape(T, D)], od_w_out[0], tm=1024, tn=1024)
    h, xn, route = cross_attention(h.reshape(B, S, D), norm_cross_g[1], wq_c[1], kv_mem, wo_c[1], kv_blk=1,
                                   tm=1024, router=(norm_ffn_g[1], od_router[0]))
    h, xn, route = h.reshape(T, D), xn.reshape(T, D // 2), route.reshape(T, LANE)

    tm_moe, sub_moe = 1024, 256
    slot_tok, dest, block_e, block_nsub, nused = _dispatch(route, tm_moe, sub_moe)
    ys = moe_experts(xn, slot_tok, block_e, block_nsub, nused, od_moe_wg[0], od_moe_wu[0], od_moe_wd[0],
                     tm=tm_moe, sub=sub_moe, tf=512)
    out = moe_combine_norm(h, route, ys, dest, final_norm_g, tm=256)
    return out.reshape(B, S, D)
```

```python
import functools
import math

import jax
import jax.numpy as jnp
from jax import lax
from jax.experimental import pallas as pl
from jax.experimental.pallas import tpu as pltpu

F32 = jnp.float32
BF16 = jnp.bfloat16

EPS = 1e-6
NEG = -1e30
LOG2E = math.log2(math.e)
CHUNK = 64

CONV_WIDTH = 31
CONV_HALO = 32
MLA_HEADS = 8
MLA_NOPE = 128
MLA_ROPE = 64
MLA_V = 128
MLA_RANK = 512
MLA_QK_PAD = 256
ROPE_THETA = 10000.0
DIFF_HEADS = 8
DIFF_HD = 128
REL_BUCKETS = 32
REL_MAX_DIST = 128
CROSS_HEADS = 4
CROSS_HD = 128
N_EXPERTS = 8
LANE = 128
SUBLANE = 8

VMEM_LIMIT = 60 * 1024 * 1024

ROW_TILE = 1024
COL_TILE = 1024
IN_PROJ_TILE_L0 = 1536
IN_PROJ_TILE_L1 = 2048
FFN_TILE = 256
MOE_FFN_TILE = 512
MOE_SUB = 256
ATTN_TILE = 256
CONV_TILE = 512
MEM_TILE = 512
COMBINE_TILE = 256

NT_DIMS = (((1,), (1,)), ((), ()))


def _cparams(*sem):
    return pltpu.CompilerParams(dimension_semantics=sem, vmem_limit_bytes=VMEM_LIMIT)


def _rms(x, g):
    return x * lax.rsqrt(jnp.mean(x * x, axis=-1, keepdims=True) + EPS) * g


def _dot(a, b):
    return jnp.dot(a, b, preferred_element_type=F32)


def _pack_bf16_pairs(x):
    n = x.shape[1] // 2
    bits = lax.bitcast_convert_type(x.astype(BF16).astype(F32), jnp.uint32)
    return (bits[:, :n] >> 16) | bits[:, n:]


def _unpack_bf16_pairs(w):
    lo = lax.bitcast_convert_type(w << 16, F32)
    hi = lax.bitcast_convert_type(w & jnp.uint32(0xFFFF0000), F32)
    return lo.astype(BF16), hi.astype(BF16)


def _load_bf16(w_ref):
    w = w_ref[0] if len(w_ref.shape) == 3 else w_ref[...]
    return w.astype(BF16)


def _norm_matmul_kernel(x_ref, g_ref, w_ref, *refs, scaled_tiles, scale, has_side):
    if has_side:
        ws_ref, o_ref, os_ref, xn_ref = refs
    else:
        o_ref, xn_ref = refs
    j = pl.program_id(1)

    @pl.when(j == 0)
    def _():
        xn_ref[...] = _rms(x_ref[...], g_ref[...]).astype(BF16)
        if has_side:
            os_ref[...] = _dot(xn_ref[...], ws_ref[...]).astype(os_ref.dtype)

    acc = _dot(xn_ref[...], w_ref[...].astype(BF16))
    if scaled_tiles:
        acc = acc * jnp.where(j < scaled_tiles, scale, 1.0)
    o_ref[...] = acc.astype(o_ref.dtype)


def norm_matmul(x, g, w, *, tm, tn, n_cols=None, scaled_cols=0, scale=1.0, side_w=None):
    M, K = x.shape
    N = n_cols or w.shape[1]
    tm = min(tm, M)
    assert scaled_cols % tn == 0 and N % tn == 0
    in_specs = [
        pl.BlockSpec((tm, K), lambda i, j: (i, 0)),
        pl.BlockSpec((1, K), lambda i, j: (0, 0)),
        pl.BlockSpec((K, tn), lambda i, j: (0, j)),
    ]
    out_specs = [pl.BlockSpec((tm, tn), lambda i, j: (i, j))]
    out_shape = [jax.ShapeDtypeStruct((M, N), BF16)]
    args = [x, g.reshape(1, K), w]
    if side_w is not None:
        ns = side_w.shape[1]
        in_specs.append(pl.BlockSpec((K, ns), lambda i, j: (0, 0)))
        out_specs.append(pl.BlockSpec((tm, ns), lambda i, j: (i, 0)))
        out_shape.append(jax.ShapeDtypeStruct((M, ns), BF16))
        args.append(side_w)
    outs = pl.pallas_call(
        functools.partial(_norm_matmul_kernel, scaled_tiles=scaled_cols // tn, scale=scale,
                          has_side=side_w is not None),
        grid=(M // tm, N // tn),
        in_specs=in_specs,
        out_specs=out_specs,
        out_shape=out_shape,
        scratch_shapes=[pltpu.VMEM((tm, K), BF16)],
        compiler_params=_cparams("parallel", "arbitrary"),
        name="norm_matmul",
    )(*args)
    return outs if side_w is not None else outs[0]


def _matmul_res_kernel(res_ref, *refs, n):
    a_refs, w_ref, o_ref, wb_ref = refs[:n], refs[n], refs[n + 1], refs[n + 2]

    @pl.when(pl.program_id(1) == 0)
    def _():
        wb_ref[...] = w_ref[...].astype(BF16)

    ka = a_refs[0].shape[1]
    acc = res_ref[...]
    for k in range(n):
        acc = acc + _dot(a_refs[k][...], wb_ref[k * ka:(k + 1) * ka, :])
    o_ref[...] = acc


def matmul_res(res, a_list, w, *, tm, tn):
    M, N = res.shape
    K = w.shape[0]
    tm = min(tm, M)
    n = len(a_list)
    ka = a_list[0].shape[1]
    assert all(a.shape[1] == ka for a in a_list) and K == n * ka
    in_specs = [pl.BlockSpec((tm, tn), lambda j, i: (i, j))]
    in_specs += [pl.BlockSpec((tm, ka), lambda j, i: (i, 0)) for _ in a_list]
    in_specs += [pl.BlockSpec((K, tn), lambda j, i: (0, j))]
    return pl.pallas_call(
        functools.partial(_matmul_res_kernel, n=n),
        grid=(N // tn, M // tm),
        in_specs=in_specs,
        out_specs=pl.BlockSpec((tm, tn), lambda j, i: (i, j)),
        out_shape=jax.ShapeDtypeStruct((M, N), F32),
        scratch_shapes=[pltpu.VMEM((K, tn), BF16)],
        compiler_params=_cparams("parallel", "arbitrary"),
        name="matmul_res",
    )(res, *a_list, w)


def _conv_kernel(val_ref, gate_ref, w_ref, b_ref, lg_ref, lb_ref, o_ref, ubuf, shifted, *, ts):
    s = pl.program_id(1)

    @pl.when(s == 0)
    def _():
        ubuf[0:CONV_HALO, :] = jnp.zeros((CONV_HALO, ubuf.shape[1]), F32)

    @pl.when(s > 0)
    def _():
        ubuf[0:CONV_HALO, :] = ubuf[ts:ts + CONV_HALO, :]

    val = val_ref[0].astype(F32)
    gate = gate_ref[0].astype(F32)
    ubuf[CONV_HALO:CONV_HALO + ts, :] = val * jax.nn.sigmoid(gate)

    span = ts + CONV_HALO - SUBLANE
    for k in range(1, SUBLANE):
        shifted[k - 1] = ubuf[k:k + span, :]

    base = CONV_HALO - (CONV_WIDTH - 1)
    acc = jnp.zeros((ts, ubuf.shape[1]), F32) + b_ref[...]
    for j in range(CONV_WIDTH):
        k = (base + j) % SUBLANE
        a = base + j - k
        window = ubuf[a:a + ts, :] if k == 0 else shifted[k - 1, a:a + ts, :]
        acc = acc + w_ref[j:j + 1, :] * window

    mu = jnp.mean(acc, axis=-1, keepdims=True)
    xc = acc - mu
    y = xc * lax.rsqrt(jnp.mean(xc * xc, axis=-1, keepdims=True) + EPS)
    y = y * lg_ref[...] + lb_ref[...]
    o_ref[0] = (y * jax.nn.sigmoid(y)).astype(o_ref.dtype)


def conformer_conv(z, conv_w, conv_b, ln_g, ln_b, *, ts):
    B, S, _ = z.shape
    C = conv_w.shape[1]
    ts = min(ts, S)
    wpad = jnp.zeros((CONV_HALO, C), F32).at[:CONV_WIDTH].set(conv_w)
    row = lambda v: v.reshape(1, C)
    const = lambda shape: pl.BlockSpec(shape, lambda b, s: (0, 0))
    return pl.pallas_call(
        functools.partial(_conv_kernel, ts=ts),
        grid=(B, S // ts),
        in_specs=[
            pl.BlockSpec((1, ts, C), lambda b, s: (b, s, 0)),
            pl.BlockSpec((1, ts, C), lambda b, s: (b, s, 1)),
            const((CONV_HALO, C)), const((1, C)), const((1, C)), const((1, C)),
        ],
        out_specs=pl.BlockSpec((1, ts, C), lambda b, s: (b, s, 0)),
        out_shape=jax.ShapeDtypeStruct((B, S, C), BF16),
        scratch_shapes=[pltpu.VMEM((ts + CONV_HALO, C), F32),
                        pltpu.VMEM((SUBLANE - 1, ts + CONV_HALO - SUBLANE, C), F32)],
        compiler_params=_cparams("parallel", "arbitrary"),
        name="conformer_conv",
    )(z, z, wpad, row(conv_b), row(ln_g), row(ln_b))


def _mla_proj_kernel(cq_ref, ckv_ref, kr_ref, qg_ref, kvg_ref, wqa_ref, wqb_ref, wk_ref, wv_ref,
                     ct_ref, st_ref, q_ref, k_ref, v_ref):
    cqn = _rms(cq_ref[...].astype(F32), qg_ref[...]).astype(BF16)
    ckvn = _rms(ckv_ref[...].astype(F32), kvg_ref[...]).astype(BF16)
    ct = ct_ref[...]
    st = st_ref[...]

    qa = _dot(cqn, wqa_ref[...])
    qb = _dot(cqn, wqb_ref[...])
    kn = _dot(ckvn, wk_ref[...])
    v_ref[...] = _dot(ckvn, wv_ref[...]).astype(BF16)

    kr = kr_ref[...].astype(F32)
    k_rope = (kr * ct + pltpu.roll(kr, LANE // 2, axis=1) * st).astype(BF16)

    for h in range(MLA_HEADS):
        o = h * MLA_QK_PAD
        q_ref[:, o:o + LANE] = qa[:, o:o + LANE].astype(BF16)
        q_ref[:, o + LANE:o + 2 * LANE] = (
            qa[:, o + LANE:o + 2 * LANE] * ct + qb[:, h * LANE:(h + 1) * LANE] * st).astype(BF16)
        k_ref[:, o:o + LANE] = kn[:, h * LANE:(h + 1) * LANE].astype(BF16)
        k_ref[:, o + LANE:o + 2 * LANE] = k_rope


def mla_proj(z, z_kr, q_norm_g, kv_norm_g, wqa, wqb, wk, wv, ct, st, *, seq, tm):
    T = z.shape[0]
    tm = min(tm, seq)
    n_s = seq // tm
    R = MLA_RANK
    cq_blk = (2 * 1024) // R
    const = lambda shape: pl.BlockSpec(shape, lambda i: (0, 0))
    HQ = MLA_HEADS * MLA_QK_PAD
    HV = MLA_HEADS * MLA_V
    return pl.pallas_call(
        _mla_proj_kernel,
        grid=(T // tm,),
        in_specs=[
            pl.BlockSpec((tm, R), lambda i: (i, cq_blk)),
            pl.BlockSpec((tm, R), lambda i: (i, cq_blk + 1)),
            pl.BlockSpec((tm, LANE), lambda i: (i, 0)),
            const((1, R)), const((1, R)),
            const((R, HQ)), const((R, MLA_HEADS * LANE)), const((R, HV)), const((R, HV)),
            pl.BlockSpec((tm, LANE), lambda i: (i % n_s, 0)),
            pl.BlockSpec((tm, LANE), lambda i: (i % n_s, 0)),
        ],
        out_specs=[
            pl.BlockSpec((tm, HQ), lambda i: (i, 0)),
            pl.BlockSpec((tm, HQ), lambda i: (i, 0)),
            pl.BlockSpec((tm, HV), lambda i: (i, 0)),
        ],
        out_shape=[
            jax.ShapeDtypeStruct((T, HQ), BF16),
            jax.ShapeDtypeStruct((T, HQ), BF16),
            jax.ShapeDtypeStruct((T, HV), BF16),
        ],
        compiler_params=_cparams("parallel"),
        name="mla_proj",
    )(z, z, z_kr, q_norm_g.reshape(1, R), kv_norm_g.reshape(1, R), wqa, wqb, wk, wv, ct, st)


def _softmax_pv(s, v):
    m = jnp.max(s, axis=-1, keepdims=True)
    p = jnp.exp2(s - m)
    l = jnp.sum(p, axis=-1, keepdims=True)
    return _dot(p.astype(BF16), v) / l


def _mla_attn_kernel(q_ref, k_ref, v_ref, o_ref, *, tq, n_tiles):
    qc = lax.broadcasted_iota(jnp.int32, (tq, tq), 0) // CHUNK
    kc = lax.broadcasted_iota(jnp.int32, (tq, tq), 1) // CHUNK
    visible = kc <= qc

    for c in range(n_tiles):
        n = (c + 1) * tq
        rows = slice(c * tq, n)
        s = lax.dot_general(q_ref[0, rows, :], k_ref[0, :n, :], NT_DIMS, preferred_element_type=F32)
        parts = [s[:, :c * tq]] if c else []
        parts.append(jnp.where(visible, s[:, c * tq:], NEG))
        s = jnp.concatenate(parts, axis=1) if c else parts[0]
        o_ref[0, rows, :] = _softmax_pv(s, v_ref[0, :n, :]).astype(o_ref.dtype)


def mla_attention(q, k, v, *, tq):
    B, S, _ = q.shape
    tq = min(tq, S)
    head = lambda b, h: (b, 0, h)
    return pl.pallas_call(
        functools.partial(_mla_attn_kernel, tq=tq, n_tiles=S // tq),
        grid=(B, MLA_HEADS),
        in_specs=[
            pl.BlockSpec((1, S, MLA_QK_PAD), head),
            pl.BlockSpec((1, S, MLA_QK_PAD), head),
            pl.BlockSpec((1, S, MLA_V), head),
        ],
        out_specs=pl.BlockSpec((1, S, MLA_V), head),
        out_shape=jax.ShapeDtypeStruct((B, S, MLA_HEADS * MLA_V), BF16),
        compiler_params=_cparams("parallel", "parallel"),
        name="mla_attention",
    )(q, k, v)


def _diff_attn_kernel(lam_ref, q_ref, k_ref, v_ref, bias_ref, g_ref, o_ref, *, tq, n_tiles, out_scale):
    lam = lam_ref[0]

    for c in range(n_tiles):
        n = (c + 1) * tq
        rows = slice(c * tq, n)
        v = v_ref[0, :n, :]

        def half(lo):
            s = lax.dot_general(q_ref[0, rows, lo:lo + DIFF_HD], k_ref[0, :n, lo:lo + DIFF_HD], NT_DIMS,
                                preferred_element_type=F32)
            parts = []
            if c >= 2:
                parts.append(s[:, :(c - 1) * tq])
            if c >= 1:
                parts.append(s[:, (c - 1) * tq:c * tq] + bias_ref[0, 0])
            parts.append(s[:, c * tq:] + bias_ref[0, 1])
            s = jnp.concatenate(parts, axis=1) if c else parts[0]
            return _softmax_pv(s, v)

        o = half(0) - lam * half(DIFF_HD)
        o_ref[0, rows, :] = (_rms(o, g_ref[...]) * out_scale).astype(o_ref.dtype)


def diff_attention(qkv, lam, bias_tiles, subln_g, *, tq, out_scale):
    B, S, _ = qkv.shape
    H = DIFF_HEADS
    dv = 2 * DIFF_HD
    smem = pl.BlockSpec(memory_space=pltpu.SMEM)
    return pl.pallas_call(
        functools.partial(_diff_attn_kernel, tq=tq, n_tiles=S // tq, out_scale=out_scale),
        grid=(B, H),
        in_specs=[
            smem,
            pl.BlockSpec((1, S, dv), lambda b, h: (b, 0, h)),
            pl.BlockSpec((1, S, dv), lambda b, h: (b, 0, H + h)),
            pl.BlockSpec((1, S, dv), lambda b, h: (b, 0, 2 * H + h)),
            pl.BlockSpec((1, 2, tq, tq), lambda b, h: (h, 0, 0, 0)),
            pl.BlockSpec((1, dv), lambda b, h: (0, 0)),
        ],
        out_specs=pl.BlockSpec((1, S, dv), lambda b, h: (b, 0, h)),
        out_shape=jax.ShapeDtypeStruct((B, S, H * dv), BF16),
        compiler_params=_cparams("parallel", "parallel"),
        name="diff_attention",
    )(lam, qkv, qkv, qkv, bias_tiles, subln_g.reshape(1, dv))


def _cross_kernel(h_ref, g_ref, wq_ref, kv_ref, wo_ref, *rest, with_router):
    if with_router:
        rg_ref, wr_ref, o_ref, xn_ref, route_ref = rest
    else:
        (o_ref,) = rest
    h = h_ref[0]
    hn = _rms(h, g_ref[...]).astype(BF16)
    q = _dot(hn, wq_ref[...]).astype(BF16)
    kv = kv_ref[0]
    HD = CROSS_HEADS * CROSS_HD
    outs = []
    for hd in range(CROSS_HEADS):
        lo = hd * CROSS_HD
        s = lax.dot_general(q[:, lo:lo + CROSS_HD], kv[:, lo:lo + CROSS_HD], NT_DIMS,
                            preferred_element_type=F32)
        m = jnp.max(s, axis=-1, keepdims=True)
        p = jnp.exp(s - m)
        l = jnp.sum(p, axis=-1, keepdims=True)
        o = _dot(p.astype(BF16), kv[:, HD + lo:HD + lo + CROSS_HD]) / l
        outs.append(o.astype(BF16))
    o_all = jnp.concatenate(outs, axis=-1)
    h_new = h + _dot(o_all, wo_ref[...])
    o_ref[0] = h_new
    if with_router:
        xn, route_ref[0] = _route(h_new, rg_ref[...], wr_ref[...])
        xn_ref[0] = _pack_bf16_pairs(xn)


def cross_attention(h, g, wq, kv, wo, *, kv_blk, tm, router=None):
    B, S, D = h.shape
    M = kv.shape[1]
    HD = CROSS_HEADS * CROSS_HD
    tm = min(tm, S)
    const = lambda shape: pl.BlockSpec(shape, lambda b, s: (0, 0))
    tile = lambda width: pl.BlockSpec((1, tm, width), lambda b, s: (b, s, 0))
    in_specs = [tile(D), const((1, D)), const((D, HD)),
                pl.BlockSpec((1, M, 2 * HD), lambda b, s: (b, 0, kv_blk)), const((HD, D))]
    args = [h, g.reshape(1, D), wq, kv, wo]
    out_specs = [tile(D)]
    out_shape = [jax.ShapeDtypeStruct((B, S, D), F32)]
    if router is not None:
        rg, w_router = router
        in_specs += [const((1, D)), const((D, LANE))]
        args += [rg.reshape(1, D), jnp.zeros((D, LANE), BF16).at[:, :N_EXPERTS].set(w_router.astype(BF16))]
        out_specs += [tile(D // 2), tile(LANE)]
        out_shape += [jax.ShapeDtypeStruct((B, S, D // 2), jnp.uint32),
                      jax.ShapeDtypeStruct((B, S, LANE), F32)]
    outs = pl.pallas_call(
        functools.partial(_cross_kernel, with_router=router is not None),
        grid=(B, S // tm),
        in_specs=in_specs,
        out_specs=out_specs,
        out_shape=out_shape,
        compiler_params=_cparams("parallel", "arbitrary"),
        name="cross_attention",
    )(*args)
    return outs if router is not None else outs[0]


def _swiglu_step(x, wg_ref, wu_ref, wd_ref, o_ref, rows=slice(None)):
    gt = _dot(x, _load_bf16(wg_ref))
    up = _dot(x, _load_bf16(wu_ref))
    hm = (gt * jax.nn.sigmoid(gt) * up).astype(BF16)
    o_ref[rows, :] += _dot(hm, _load_bf16(wd_ref))


def _ffn_kernel(h_ref, g_ref, wg_ref, wu_ref, wd_ref, o_ref, xn_ref):
    @pl.when(pl.program_id(1) == 0)
    def _():
        xn_ref[...] = _rms(h_ref[...], g_ref[...]).astype(BF16)
        o_ref[...] = h_ref[...]

    _swiglu_step(xn_ref[...], wg_ref, wu_ref, wd_ref, o_ref)


def dense_ffn(h, g, wg, wu, wd, *, tm, tf):
    T, D = h.shape
    F = wg.shape[1]
    tm = min(tm, T)
    return pl.pallas_call(
        _ffn_kernel,
        grid=(T // tm, F // tf),
        in_specs=[
            pl.BlockSpec((tm, D), lambda i, f: (i, 0)),
            pl.BlockSpec((1, D), lambda i, f: (0, 0)),
            pl.BlockSpec((D, tf), lambda i, f: (0, f)),
            pl.BlockSpec((D, tf), lambda i, f: (0, f)),
            pl.BlockSpec((tf, D), lambda i, f: (f, 0)),
        ],
        out_specs=pl.BlockSpec((tm, D), lambda i, f: (i, 0)),
        out_shape=jax.ShapeDtypeStruct((T, D), F32),
        scratch_shapes=[pltpu.VMEM((tm, D), BF16)],
        compiler_params=_cparams("parallel", "arbitrary"),
        name="dense_ffn",
    )(h, g.reshape(1, D), wg, wu, wd)


def _route(h, g, wr):
    xn = _rms(h, g)
    logits = _dot(xn.astype(BF16), wr)
    lane = lax.broadcasted_iota(jnp.int32, logits.shape, 1)
    logits = jnp.where(lane < N_EXPERTS, logits, -jnp.inf)
    v1 = jnp.max(logits, axis=-1, keepdims=True)
    i1 = jnp.min(jnp.where(logits == v1, lane, LANE), axis=-1, keepdims=True)
    rest = jnp.where(lane == i1, -jnp.inf, logits)
    v2 = jnp.max(rest, axis=-1, keepdims=True)
    i2 = jnp.min(jnp.where(rest == v2, lane, LANE), axis=-1, keepdims=True)
    e2 = jnp.exp(v2 - v1)
    g1 = 1.0 / (1.0 + e2)
    g2 = e2 / (1.0 + e2)
    route = jnp.where(lane == 0, i1.astype(F32), 0.0)
    route = jnp.where(lane == 1, i2.astype(F32), route)
    route = jnp.where(lane == 2, g1, route)
    route = jnp.where(lane == 3, g2, route)
    return xn, route


def _moe_kernel(tok_ref, be_ref, ns_ref, nu_ref, x_hbm, wg_ref, wu_ref, wd_ref, o_ref, xbuf, xb, sem,
                *, tm, sub, rows_per_step, n_steps):
    i = pl.program_id(0)
    step = pl.program_id(1)
    nused = nu_ref[0]
    n_rows = rows_per_step * n_steps

    def row_copy(blk, r):
        tok = tok_ref[blk * tm + r]
        return pltpu.make_async_copy(x_hbm.at[pl.ds(tok, 1)], xbuf.at[pl.ds(r, 1)], sem.at[0])

    def for_rows(fn):
        def body(r, c):
            fn(r)
            return c
        lax.fori_loop(0, n_rows, body, 0, unroll=4)

    @pl.when(step == 0)
    def _():
        o_ref[...] = jnp.zeros(o_ref.shape, F32)

        @pl.when(i == 0)
        def _():
            for_rows(lambda r: row_copy(0, r).start())

        @pl.when(i <= nused)
        def _():
            for_rows(lambda r: row_copy(i, r).wait())

    @pl.when(i < nused)
    def _():
        @pl.when(step == 0)
        def _():
            half = xbuf.shape[1]
            xb[:, :half], xb[:, half:] = _unpack_bf16_pairs(xbuf[0:tm, :])

        for j in range(rows_per_step):
            row_copy(i + 1, step * rows_per_step + j).start()

        def row_range(start, size):
            rows = pl.ds(start, size)
            _swiglu_step(xb[rows, :], wg_ref, wu_ref, wd_ref, o_ref, rows)

        units = tm // sub
        assert units == 4
        row_range(0, 2 * sub)
        pl.when(ns_ref[i] == units)(functools.partial(row_range, 2 * sub, 2 * sub))
        pl.when(ns_ref[i] == units - 1)(functools.partial(row_range, 2 * sub, sub))


def moe_experts(xn, slot_tok, block_e, block_nsub, nused, wg, wu, wd, *, tm, sub, tf):
    D = wg.shape[1]
    assert xn.shape[1] * 2 == D and xn.dtype == jnp.uint32
    P = slot_tok.shape[0]
    F = wg.shape[2]
    nf = n_steps = F // tf
    rows_per_step = -(-tm // n_steps)
    spare = rows_per_step * n_steps - tm
    assert (tm + spare) % 4 == 0
    slot_tok = jnp.concatenate([slot_tok, jnp.zeros((spare,), jnp.int32)])
    xbuf_rows = -(-(tm + spare) // SUBLANE) * SUBLANE

    def live(i, f, nu):
        return jnp.minimum(i, nu[0] - 1), jnp.where(i < nu[0], f, nf - 1)

    def up_map(i, f, tok, be, ns, nu):
        ii, ff = live(i, f, nu)
        return be[ii], 0, ff

    def down_map(i, f, tok, be, ns, nu):
        ii, ff = live(i, f, nu)
        return be[ii], ff, 0

    return pl.pallas_call(
        functools.partial(_moe_kernel, tm=tm, sub=sub, rows_per_step=rows_per_step, n_steps=n_steps),
        grid_spec=pltpu.PrefetchScalarGridSpec(
            num_scalar_prefetch=4,
            grid=(P // tm, nf),
            in_specs=[
                pl.BlockSpec(memory_space=pl.ANY),
                pl.BlockSpec((1, D, tf), up_map),
                pl.BlockSpec((1, D, tf), up_map),
                pl.BlockSpec((1, tf, D), down_map),
            ],
            out_specs=pl.BlockSpec((tm, D), lambda i, f, tok, be, ns, nu: (i, 0)),
            scratch_shapes=[
                pltpu.VMEM((xbuf_rows, D // 2), jnp.uint32),
                pltpu.VMEM((tm, D), BF16),
                pltpu.SemaphoreType.DMA((1,)),
            ],
        ),
        out_shape=jax.ShapeDtypeStruct((P, D), F32),
        compiler_params=_cparams("arbitrary", "arbitrary"),
        name="moe_experts",
    )(slot_tok, block_e, block_nsub, nused, xn, wg, wu, wd)


def _combine_kernel(pos_ref, h_ref, route_ref, g_ref, ys_hbm, o_ref, buf_a, buf_b, sem, *, tm, n_blocks):
    i = pl.program_id(0)
    nxt = jnp.where(i + 1 < n_blocks, i + 1, 0)

    def row_copy(blk, buf, parity, r, k):
        p = pos_ref[2 * (blk * tm + r) + k]
        return pltpu.make_async_copy(ys_hbm.at[pl.ds(p, 1)], buf.at[k, pl.ds(r, 1)], sem.at[parity, k])

    def for_rows(fn):
        def body(r, c):
            fn(r, 0)
            fn(r, 1)
            return c
        lax.fori_loop(0, tm, body, 0, unroll=8)

    @pl.when(i == 0)
    def _():
        for_rows(lambda r, k: row_copy(0, buf_a, 0, r, k).start())

    def step(parity, cur, other):
        for_rows(lambda r, k: row_copy(i, cur, parity, r, k).wait())
        for r in range(tm):
            row_copy(nxt, other, 1 - parity, r, 0).start()
            row_copy(nxt, other, 1 - parity, r, 1).start()
        route = route_ref[...]
        y = route[:, 2:3] * cur[0] + route[:, 3:4] * cur[1]
        o_ref[...] = _rms(h_ref[...] + y, g_ref[...])

        @pl.when(i == n_blocks - 1)
        def _():
            for_rows(lambda r, k: row_copy(0, other, 1 - parity, r, k).wait())

    pl.when(i % 2 == 0)(functools.partial(step, 0, buf_a, buf_b))
    pl.when(i % 2 == 1)(functools.partial(step, 1, buf_b, buf_a))


def moe_combine_norm(h, route, ys, pos, g, *, tm):
    T, D = h.shape
    tm = min(tm, T)
    return pl.pallas_call(
        functools.partial(_combine_kernel, tm=tm, n_blocks=T // tm),
        grid_spec=pltpu.PrefetchScalarGridSpec(
            num_scalar_prefetch=1,
            grid=(T // tm,),
            in_specs=[
                pl.BlockSpec((tm, D), lambda i, pos: (i, 0)),
                pl.BlockSpec((tm, LANE), lambda i, pos: (i, 0)),
                pl.BlockSpec((1, D), lambda i, pos: (0, 0)),
                pl.BlockSpec(memory_space=pl.ANY),
            ],
            out_specs=pl.BlockSpec((tm, D), lambda i, pos: (i, 0)),
            scratch_shapes=[pltpu.VMEM((2, tm, D), F32), pltpu.VMEM((2, tm, D), F32),
                            pltpu.SemaphoreType.DMA((2, 2))],
        ),
        out_shape=jax.ShapeDtypeStruct((T, D), F32),
        compiler_params=_cparams("arbitrary"),
        name="moe_combine_norm",
    )(pos, h, route, g.reshape(1, D), ys)


def _dispatch(route, tm, sub):
    T = route.shape[0]
    A = 2 * T
    P = A + N_EXPERTS * tm
    nblk = P // tm
    flat_e = route[:, :2].astype(jnp.int32).reshape(A)
    onehot = (flat_e[:, None] == jnp.arange(N_EXPERTS, dtype=jnp.int32)[None, :]).astype(jnp.int32)
    csum = jnp.cumsum(onehot, axis=0)
    counts = csum[-1]
    rank = jnp.sum(csum * onehot, axis=1) - 1
    padded = ((counts + tm - 1) // tm) * tm
    pend = jnp.cumsum(padded)
    pstart = pend - padded
    dest = (jnp.sum(onehot * pstart[None, :], axis=1) + rank).astype(jnp.int32)
    slot_tok = jnp.zeros((P,), jnp.int32).at[dest].set(jnp.arange(A, dtype=jnp.int32) // 2)
    blk_row0 = jnp.arange(nblk, dtype=jnp.int32) * tm
    block_e = jnp.minimum(jnp.searchsorted(pend, blk_row0, side="right"), N_EXPERTS - 1).astype(jnp.int32)
    live_end = pstart + ((counts + sub - 1) // sub) * sub
    block_nsub = (jnp.clip(live_end[block_e] - blk_row0, 0, tm) // sub).astype(jnp.int32)
    nused = (pend[-1] // tm).astype(jnp.int32).reshape(1)
    return slot_tok, dest, block_e, block_nsub, nused


def _rope_slabs(seq):
    pos = jnp.arange(seq, dtype=F32)
    inv = jnp.power(ROPE_THETA, -jnp.arange(0, MLA_ROPE, 2, dtype=F32) / MLA_ROPE)
    ang = pos[:, None] * inv[None, :]
    z = jnp.zeros((seq, LANE - MLA_ROPE), F32)
    ct = jnp.concatenate([jnp.cos(ang), jnp.cos(ang), z], axis=1)
    st = jnp.concatenate([jnp.sin(ang), jnp.sin(ang), z], axis=1)
    return ct, st


def _t5_bucket(rel):
    half = REL_BUCKETS // 2
    max_exact = half // 2
    ret = (rel > 0).astype(jnp.int32) * half
    n = jnp.abs(rel)
    nf = jnp.maximum(n, 1).astype(F32)
    large = max_exact + (jnp.log(nf / max_exact) / math.log(REL_MAX_DIST / max_exact)
                         * (half - max_exact)).astype(jnp.int32)
    large = jnp.minimum(large, half - 1)
    return ret + jnp.where(n < max_exact, n, large)


def _bias_tiles(rel_bias, tq):
    assert tq >= REL_MAX_DIST
    qi = jnp.arange(tq, dtype=jnp.int32)[:, None]
    ki = jnp.arange(tq, dtype=jnp.int32)[None, :]

    def lookup(rel):
        bucket = _t5_bucket(rel)[None]
        out = jnp.zeros((rel_bias.shape[1],) + rel.shape, F32)
        for b in range(REL_BUCKETS):
            out = jnp.where(bucket == b, rel_bias[b][:, None, None], out)
        return out

    far = lookup(jnp.full((1, 1), -2 * tq, jnp.int32))
    prev = lookup(ki - qi - tq) - far
    diag = jnp.where(((ki // CHUNK) <= (qi // CHUNK))[None], lookup(ki - qi) - far, NEG)
    return jnp.stack([prev, diag], axis=1)


def kernel(x, mem, rel_bias, mem_norm_g, norm_mix_g, norm_cross_g, norm_ffn_g, cross_wq, cross_wkv, cross_wo, ev_w_in, ev_conv_w, ev_conv_b, ev_ln_g, ev_ln_b, ev_q_norm_g, ev_w_uq, ev_kv_norm_g, ev_w_ukv, ev_w_out, ev_ffn_wg, ev_ffn_wu, ev_ffn_wd, od_w_in, od_lambda_q1, od_lambda_k1, od_lambda_q2, od_lambda_k2, od_subln_g, od_w_out, od_router, od_moe_wg, od_moe_wu, od_moe_wd, final_norm_g):
    B, S, D = x.shape
    T = B * S
    M = mem.shape[1]
    AW = ev_conv_w.shape[2]
    H = MLA_HEADS
    R = MLA_RANK
    h = x.reshape(T, D)

    kr0 = 2 * AW + 2 * R
    half = MLA_ROPE // 2
    w_kr = ev_w_in[0][:, kr0:kr0 + MLA_ROPE]
    w_kr = jnp.concatenate([w_kr, -w_kr[:, half:], w_kr[:, :half]], axis=1).astype(BF16)

    q_scale = (MLA_NOPE + MLA_ROPE) ** -0.5 * LOG2E
    wuq = (ev_w_uq[0] * q_scale).reshape(R, H, MLA_NOPE + MLA_ROPE)
    w_nope, w_r1, w_r2 = wuq[..., :MLA_NOPE], wuq[..., MLA_NOPE:MLA_NOPE + half], wuq[..., MLA_NOPE + half:]
    zq = jnp.zeros((R, H, MLA_QK_PAD - MLA_NOPE - MLA_ROPE), F32)
    wqa = jnp.concatenate([w_nope, w_r1, w_r2, zq], axis=-1).reshape(R, H * MLA_QK_PAD).astype(BF16)
    wqb = jnp.concatenate([-w_r2, w_r1, zq], axis=-1).reshape(R, H * LANE).astype(BF16)
    wukv = ev_w_ukv[0].reshape(R, H, MLA_NOPE + MLA_V)
    wk = wukv[..., :MLA_NOPE].reshape(R, H * MLA_NOPE).astype(BF16)
    wv = wukv[..., MLA_NOPE:].reshape(R, H * MLA_V).astype(BF16)
    ct, st = _rope_slabs(S)

    c_scale = CROSS_HD ** -0.5
    wq_c = (cross_wq * c_scale).astype(BF16)
    wkv_c = jnp.concatenate([cross_wkv[0], cross_wkv[1]], axis=1).astype(BF16)
    wo_c = cross_wo.astype(BF16)

    d_scale = DIFF_HD ** -0.5 * LOG2E
    layer = 1
    lambda_init = 0.8 - 0.6 * math.exp(-0.3 * layer)
    lam = (jnp.exp(jnp.sum(od_lambda_q1[0] * od_lambda_k1[0]))
           - jnp.exp(jnp.sum(od_lambda_q2[0] * od_lambda_k2[0])) + lambda_init).reshape(1).astype(F32)
    tq = min(ATTN_TILE, S)
    bias_tiles = _bias_tiles(rel_bias * LOG2E, tq)

    kv_mem = norm_matmul(mem.reshape(B * M, D), mem_norm_g, wkv_c, tm=MEM_TILE, tn=MEM_TILE)
    kv_mem = kv_mem.reshape(B, M, -1)

    z, z_kr = norm_matmul(h, norm_mix_g[0], ev_w_in[0].astype(BF16), tm=ROW_TILE,
                          tn=IN_PROJ_TILE_L0, n_cols=kr0, side_w=w_kr)
    a_out = conformer_conv(z.reshape(B, S, -1), ev_conv_w[0], ev_conv_b[0], ev_ln_g[0], ev_ln_b[0],
                           ts=CONV_TILE)
    q, k, v = mla_proj(z, z_kr, ev_q_norm_g[0], ev_kv_norm_g[0], wqa, wqb, wk, wv, ct, st, seq=S,
                       tm=ROW_TILE)
    b_out = mla_attention(q.reshape(B, S, -1), k.reshape(B, S, -1), v.reshape(B, S, -1), tq=tq)
    h = matmul_res(h, [a_out.reshape(T, AW), b_out.reshape(T, H * MLA_V)], ev_w_out[0], tm=ROW_TILE,
                   tn=COL_TILE)
    h = cross_attention(h.reshape(B, S, D), norm_cross_g[0], wq_c[0], kv_mem, wo_c[0], kv_blk=0,
                        tm=ROW_TILE)
    h = dense_ffn(h.reshape(T, D), norm_ffn_g[0], ev_ffn_wg[0], ev_ffn_wu[0], ev_ffn_wd[0], tm=ROW_TILE,
                  tf=FFN_TILE)

    qkv = norm_matmul(h, norm_mix_g[1], od_w_in[0].astype(BF16), tm=ROW_TILE, tn=IN_PROJ_TILE_L1,
                      scaled_cols=D, scale=d_scale)
    o = diff_attention(qkv.reshape(B, S, -1), lam, bias_tiles, od_subln_g[0], tq=tq,
                       out_scale=1.0 - lambda_init)
    h = matmul_res(h, [o.reshape(T, D)], od_w_out[0], tm=ROW_TILE, tn=COL_TILE)
    h, xn, route = cross_attention(h.reshape(B, S, D), norm_cross_g[1], wq_c[1], kv_mem, wo_c[1], kv_blk=1,
                                   tm=ROW_TILE, router=(norm_ffn_g[1], od_router[0]))
    h, xn, route = h.reshape(T, D), xn.reshape(T, D // 2), route.reshape(T, LANE)

    slot_tok, dest, block_e, block_nsub, nused = _dispatch(route, ROW_TILE, MOE_SUB)
    ys = moe_experts(xn, slot_tok, block_e, block_nsub, nused, od_moe_wg[0], od_moe_wu[0], od_moe_wd[0],
                     tm=ROW_TILE, sub=MOE_SUB, tf=MOE_FFN_TILE)
    out = moe_combine_norm(h, route, ys, dest, final_norm_g, tm=COMBINE_TILE)
    return out.reshape(B, S, D)
```
